```python
import math, functools
import jax, jax.numpy as jnp
from jax import lax
import numpy as np

D_MODEL = 1024
BATCH = 8
SEQ = 2048
DEPTH = 2
DEC_BATCH = 32
DEC_SEQ = 4
PAST_LEN = 8192
PAGE_SIZE = 128

N_A_LAYERS = DEPTH // 2
N_B_LAYERS = DEPTH - N_A_LAYERS
CONV_W = 3
N_HEADS = 16
N_KV_HEADS = 4
HEADS_PER_GROUP = N_HEADS // N_KV_HEADS
HEAD_DIM = D_MODEL // N_HEADS
ROT_DIM = HEAD_DIM // 4
ROPE_THETA = 500000.0
CMP_BLK = 32
CMP_STRIDE = 16
CMP_HID = 2 * HEAD_DIM
SEL_BLK = 64
SEL_TOPN = 16
WINDOW = 512
Q_BLK = 16
N_EXPERTS = 32
TOP_K = 4
D_FF = D_MODEL
SWIGLU_LIMIT = 7.0
SWIGLU_ALPHA = 1.702
MOE_BLK = 64
ALPHA = (2 * DEPTH) ** 0.25
BETA = (8 * DEPTH) ** -0.25
LN_EPS = 1e-5
NEG = -1e30
FORCE = 1e4

kernel_name = 'yoco_shortconv_nsa_moe_step'


def layer_norm(x, g, b):
    xf = x.astype(jnp.float32)
    mu = jnp.mean(xf, axis=-1, keepdims=True)
    var = jnp.mean(jnp.square(xf - mu), axis=-1, keepdims=True)
    return ((xf - mu) * lax.rsqrt(var + LN_EPS) * g + b).astype(x.dtype)


def rope(x, pos):
    half = ROT_DIM // 2
    inv = ROPE_THETA ** (-jnp.arange(half, dtype=jnp.float32) * 2.0 / ROT_DIM)
    ang = pos.astype(jnp.float32)[:, None] * inv[None, :]
    shape = (1, pos.shape[0]) + (1,) * (x.ndim - 3) + (half,)
    cos = jnp.cos(ang).reshape(shape)
    sin = jnp.sin(ang).reshape(shape)
    xf = x.astype(jnp.float32)
    x1 = xf[..., :half]
    x2 = xf[..., half:ROT_DIM]
    out = jnp.concatenate([x1 * cos - x2 * sin, x1 * sin + x2 * cos, xf[..., ROT_DIM:]], axis=-1)
    return out.astype(x.dtype)


def conv_mixer(x, w_in, w_conv, w_out, prev):
    t = x.shape[1]
    b, c, xh = jnp.split(x @ w_in, 3, axis=-1)
    u = c * xh
    up = jnp.concatenate([prev.astype(u.dtype), u], axis=1)
    conv = sum(w_conv[i] * up[:, i:i + t] for i in range(CONV_W))
    return (b * conv) @ w_out, up[:, t:]


def moe_ffn(x, wr, br, w1, b1, w2, b2):
    shp = x.shape
    xf = x.reshape(-1, shp[-1])
    n = xf.shape[0]
    logits = (xf @ wr + br).astype(jnp.float32)
    top_v, top_e = lax.top_k(logits, TOP_K)
    gate = jax.nn.softmax(top_v, axis=-1)
    flat_e = top_e.reshape(-1)
    nk = n * TOP_K
    order = jnp.argsort(flat_e)
    e_sorted = flat_e[order]
    tok_sorted = order // TOP_K
    counts = jnp.bincount(flat_e, length=N_EXPERTS)
    padded = (counts + MOE_BLK - 1) // MOE_BLK * MOE_BLK
    pad_end = jnp.cumsum(padded)
    pad_start = pad_end - padded
    start = jnp.cumsum(counts) - counts
    dest = pad_start[e_sorted] + jnp.arange(nk) - start[e_sorted]
    n_blocks = (nk + N_EXPERTS * (MOE_BLK - 1) + MOE_BLK - 1) // MOE_BLK
    n_rows = n_blocks * MOE_BLK
    row_tok = jnp.full((n_rows,), n, jnp.int32).at[dest].set(tok_sorted.astype(jnp.int32))
    blk_e = jnp.minimum(jnp.searchsorted(pad_end, jnp.arange(n_blocks) * MOE_BLK, side='right'), N_EXPERTS - 1)
    xpad = jnp.concatenate([xf, jnp.zeros((1, xf.shape[1]), xf.dtype)], axis=0)
    xb = xpad[row_tok].reshape(n_blocks, MOE_BLK, xf.shape[1])

    def expert_block(args):
        xblk, e = args
        hgu = xblk @ w1[e] + b1[e]
        g, u = jnp.split(hgu, 2, axis=-1)
        g = jnp.minimum(g, SWIGLU_LIMIT)
        u = jnp.clip(u, -SWIGLU_LIMIT, SWIGLU_LIMIT)
        a = (u + 1.0) * (g * jax.nn.sigmoid(SWIGLU_ALPHA * g))
        return a @ w2[e] + b2[e]

    yb = lax.map(expert_block, (xb, blk_e)).reshape(n_rows, xf.shape[1])
    y_assign = yb[dest].astype(jnp.float32) * gate.reshape(-1)[order][:, None]
    y = jax.ops.segment_sum(y_assign, tok_sorted, num_segments=n)
    return y.astype(x.dtype).reshape(shp)


def shared_kv_rows(h, pos, kv_w):
    bsz, t, _ = h.shape
    kv = (h @ kv_w).reshape(bsz, t, 3, 2, N_KV_HEADS, HEAD_DIM)
    k = rope(kv[:, :, :, 0], pos)
    kv = jnp.stack([k, kv[:, :, :, 1]], axis=3)
    return kv[:, :, 0], kv[:, :, 1], kv[:, :, 2]


def compress_kv(rows, pe, w1, b1, w2):
    bsz, t = rows.shape[:2]
    n_chunk = t // CMP_STRIDE
    ch = rows[:, :n_chunk * CMP_STRIDE].reshape(bsz, n_chunk, CMP_STRIDE, 2, N_KV_HEADS, HEAD_DIM)
    pa = jnp.einsum('bnckgd,kcdh->bnkgh', ch, w1[:, :CMP_STRIDE])
    pb = jnp.einsum('bnckgd,kcdh->bnkgh', ch, w1[:, CMP_STRIDE:])
    pe_term = jnp.einsum('kcd,kcdh->kh', pe, w1) + b1
    hid = jax.nn.gelu(pa[:, :-1] + pb[:, 1:] + pe_term[:, None, :])
    return jnp.einsum('bnkgh,khd->bnkgd', hid, w2)


def to_sel_blocks(rows):
    bsz, t = rows.shape[:2]
    n_sel = -(-t // SEL_BLK)
    r = jnp.pad(rows, ((0, 0), (0, n_sel * SEL_BLK - t), (0, 0), (0, 0), (0, 0)))
    return r.reshape(bsz, n_sel, SEL_BLK, 2, N_KV_HEADS, HEAD_DIM).transpose(0, 4, 1, 2, 3, 5)


def masked_attend(q, k, v, mask):
    s = jnp.einsum('btgqd,blgd->btgql', q, k).astype(jnp.float32) * (HEAD_DIM ** -0.5)
    m = mask[None, :, None, None, :]
    p = jax.nn.softmax(jnp.where(m, s, NEG), axis=-1) * m
    return jnp.einsum('btgql,blgd->btgqd', p.astype(v.dtype), v), p


def nsa_core(q, g, qpos, kvc, blk, kvw, wpos):
    n_cmp = kvc.shape[1]
    n_sel = blk.shape[2]
    cmp_end = jnp.arange(n_cmp, dtype=jnp.int32) * CMP_STRIDE + (CMP_BLK - 1)
    o_c, p_c = masked_attend(q, kvc[:, :, 0], kvc[:, :, 1], cmp_end[None, :] <= qpos[:, None])
    c0 = jnp.arange(n_cmp)[:, None] * CMP_STRIDE
    s0 = jnp.arange(n_sel)[None, :] * SEL_BLK
    overlap = jnp.clip(jnp.minimum(c0 + CMP_BLK, s0 + SEL_BLK) - jnp.maximum(c0, s0), 0, None).astype(jnp.float32) / CMP_BLK
    imp = jnp.einsum('btgqc,cs->btgs', p_c, overlap)
    j = jnp.arange(n_sel)[None, :]
    cur = (qpos // SEL_BLK)[:, None]
    valid = j <= cur
    forced = (j == 0) | (j == cur) | (j == cur - 1)
    score = jnp.where(valid[None, :, None, :], jnp.where(forced[None, :, None, :], FORCE, imp), NEG)
    top_s, idx = lax.top_k(score, min(SEL_TOPN, n_sel))
    idx_g = idx.transpose(0, 2, 1, 3)
    ok = (top_s > 0.5 * NEG).transpose(0, 2, 1, 3)
    kv_sel = jax.vmap(jax.vmap(lambda b_, i_: b_[i_]))(blk, idx_g)
    kpos = idx_g[..., None] * SEL_BLK + jnp.arange(SEL_BLK)
    m_s = (ok[..., None] & (kpos <= qpos[None, None, :, None, None])).transpose(0, 2, 1, 3, 4)[:, :, :, None]
    s = jnp.einsum('btgqd,bgtnkd->btgqnk', q, kv_sel[..., 0, :]).astype(jnp.float32) * (HEAD_DIM ** -0.5)
    s = jnp.where(m_s, s, NEG)
    p = jax.nn.softmax(s.reshape(s.shape[:4] + (-1,)), axis=-1).reshape(s.shape)
    o_s = jnp.einsum('btgqnk,bgtnkd->btgqd', p.astype(q.dtype), kv_sel[..., 1, :])
    mask_w = (wpos[None, :] <= qpos[:, None]) & (wpos[None, :] > qpos[:, None] - WINDOW) & (wpos[None, :] >= 0)
    o_w, _ = masked_attend(q, kvw[:, :, 0], kvw[:, :, 1], mask_w)
    return g[..., 0:1] * o_c + g[..., 1:2] * o_s + g[..., 2:3] * o_w


def nsa_queries(h, pos, wq, wg):
    bsz, t, _ = h.shape
    q = rope((h @ wq).reshape(bsz, t, N_KV_HEADS, HEADS_PER_GROUP, HEAD_DIM), pos)
    g = jax.nn.sigmoid((h @ wg).astype(jnp.float32)).astype(h.dtype)
    return q, g.reshape(bsz, t, N_KV_HEADS, HEADS_PER_GROUP, 3)


def nsa_prompt(h, pos, ctx, wq, wg, wo):
    bsz, t, _ = h.shape
    kvc, blk, kvw_pad = ctx
    q, g = nsa_queries(h, pos, wq, wg)
    n_qb = t // Q_BLK
    split = lambda a: a.reshape((bsz, n_qb, Q_BLK) + a.shape[2:]).swapaxes(0, 1)

    def step(args):
        qi, gi, st = args
        qpos = st + jnp.arange(Q_BLK, dtype=jnp.int32)
        slab = lax.dynamic_slice_in_dim(kvw_pad, st, WINDOW + Q_BLK, axis=1)
        wpos = st - WINDOW + jnp.arange(WINDOW + Q_BLK, dtype=jnp.int32)
        return nsa_core(qi, gi, qpos, kvc, blk, slab, wpos)

    o = lax.map(step, (split(q), split(g), jnp.arange(n_qb, dtype=jnp.int32) * Q_BLK))
    o = o.swapaxes(0, 1).reshape(bsz, t, N_HEADS * HEAD_DIM)
    return o @ wo


def nsa_sample(h, pos, ctx, wq, wg, wo):
    bsz, t, _ = h.shape
    kvc, blk, kvw, wpos = ctx
    q, g = nsa_queries(h, pos, wq, wg)
    o = nsa_core(q, g, pos, kvc, blk, kvw, wpos)
    return o.reshape(bsz, t, N_HEADS * HEAD_DIM) @ wo


def setup_inputs(seed: int = 0) -> dict:
    key = jax.random.key(seed)
    ks = jax.random.split(key, 32)
    n_pages = PAST_LEN // PAGE_SIZE
    n_pool = (DEC_BATCH * n_pages * 5) // 4
    w_buf = min(WINDOW, PAST_LEN)
    nrm = lambda k, shape, scale: jax.random.normal(k, shape, jnp.float32) * scale
    kv_scale = jnp.array([1.0, BETA], jnp.float32).reshape(1, 1, 2, 1, 1)
    perm = jax.random.permutation(ks[6], n_pool).astype(jnp.int32)
    kv_w = nrm(ks[10], (D_MODEL, 3, 2, N_KV_HEADS * HEAD_DIM), D_MODEL ** -0.5)
    kv_w = (kv_w * jnp.array([1.0, BETA], jnp.float32).reshape(1, 1, 2, 1)).reshape(D_MODEL, 6 * N_KV_HEADS * HEAD_DIM)
    return {
        'x_prompt': nrm(ks[0], (BATCH, SEQ, D_MODEL), 1.0),
        'x_sample': nrm(ks[1], (DEC_BATCH, DEC_SEQ, D_MODEL), 1.0),
        'state_conv': nrm(ks[2], (N_A_LAYERS, DEC_BATCH, CONV_W - 1, D_MODEL), 1.0),
        'cache_kv_cmp': nrm(ks[3], (n_pool, PAGE_SIZE, 2, N_KV_HEADS, HEAD_DIM), 1.0) * kv_scale,
        'cache_kv_sel': nrm(ks[4], (n_pool, PAGE_SIZE, 2, N_KV_HEADS, HEAD_DIM), 1.0) * kv_scale,
        'cache_kv_win': nrm(ks[5], (DEC_BATCH, w_buf, 2, N_KV_HEADS, HEAD_DIM), 1.0) * kv_scale,
        'page_table': perm[:DEC_BATCH * n_pages].reshape(DEC_BATCH, n_pages),
        'conv_w_in': nrm(ks[7], (N_A_LAYERS, D_MODEL, 3 * D_MODEL), D_MODEL ** -0.5),
        'conv_w': nrm(ks[8], (N_A_LAYERS, CONV_W, D_MODEL), CONV_W ** -0.5),
        'conv_w_out': nrm(ks[9], (N_A_LAYERS, D_MODEL, D_MODEL), D_MODEL ** -0.5 * BETA),
        'kv_w': kv_w,
        'cmp_pe': nrm(ks[11], (2, CMP_BLK, HEAD_DIM), 0.1),
        'cmp_w1': nrm(ks[12], (2, CMP_BLK, HEAD_DIM, CMP_HID), (CMP_BLK * HEAD_DIM) ** -0.5),
        'cmp_b1': nrm(ks[13], (2, CMP_HID), 0.02),
        'cmp_w2': nrm(ks[14], (2, CMP_HID, HEAD_DIM), CMP_HID ** -0.5),
        'nsa_wq': nrm(ks[15], (N_B_LAYERS, D_MODEL, N_HEADS * HEAD_DIM), D_MODEL ** -0.5),
        'nsa_wg': nrm(ks[16], (N_B_LAYERS, D_MODEL, N_HEADS * 3), D_MODEL ** -0.5),
        'nsa_wo': nrm(ks[17], (N_B_LAYERS, N_HEADS * HEAD_DIM, D_MODEL), (N_HEADS * HEAD_DIM) ** -0.5 * BETA),
        'moe_wr': nrm(ks[18], (DEPTH, D_MODEL, N_EXPERTS), D_MODEL ** -0.5),
        'moe_br': nrm(ks[19], (DEPTH, N_EXPERTS), 0.01),
        'moe_w1': nrm(ks[20], (DEPTH, N_EXPERTS, D_MODEL, 2 * D_FF), D_MODEL ** -0.5),
        'moe_b1': nrm(ks[21], (DEPTH, N_EXPERTS, 2 * D_FF), 0.02),
        'moe_w2': nrm(ks[22], (DEPTH, N_EXPERTS, D_FF, D_MODEL), D_FF ** -0.5 * BETA),
        'moe_b2': nrm(ks[23], (DEPTH, N_EXPERTS, D_MODEL), 0.02),
        'ln_g': 1.0 + nrm(ks[24], (DEPTH, 2, D_MODEL), 0.02),
        'ln_b': nrm(ks[25], (DEPTH, 2, D_MODEL), 0.02),
    }


def reference(x_prompt, x_sample, state_conv, cache_kv_cmp, cache_kv_sel, cache_kv_win, page_table,
              conv_w_in, conv_w, conv_w_out, kv_w, cmp_pe, cmp_w1, cmp_b1, cmp_w2,
              nsa_wq, nsa_wg, nsa_wo, moe_wr, moe_br, moe_w1, moe_b1, moe_w2, moe_b2, ln_g, ln_b):

    def trunk(x, pos, conv_prev, make_ctx, mix_b):
        h = x
        conv_new = []
        ctx = None
        kv_new = None
        for layer in range(DEPTH):
            if layer < N_A_LAYERS:
                y, st = conv_mixer(h, conv_w_in[layer], conv_w[layer], conv_w_out[layer], conv_prev[layer])
                conv_new.append(st)
            else:
                if ctx is None:
                    ctx, kv_new = make_ctx(h)
                bi = layer - N_A_LAYERS
                y = mix_b(h, pos, ctx, nsa_wq[bi], nsa_wg[bi], nsa_wo[bi])
            h = layer_norm(ALPHA * h + y, ln_g[layer, 0], ln_b[layer, 0])
            f = moe_ffn(h, moe_wr[layer], moe_br[layer], moe_w1[layer], moe_b1[layer], moe_w2[layer], moe_b2[layer])
            h = layer_norm(ALPHA * h + f, ln_g[layer, 1], ln_b[layer, 1])
        return h, jnp.stack(conv_new, axis=0), kv_new

    bp, sp, _ = x_prompt.shape
    pos_p = jnp.arange(sp, dtype=jnp.int32)
    conv0 = jnp.zeros((N_A_LAYERS, bp, CONV_W - 1, x_prompt.shape[2]), x_prompt.dtype)

    def prompt_ctx(h):
        rc, rs, rw = shared_kv_rows(h, pos_p, kv_w)
        rw_pad = jnp.pad(rw, ((0, 0), (WINDOW, 0), (0, 0), (0, 0), (0, 0)))
        ctx = (compress_kv(rc, cmp_pe, cmp_w1, cmp_b1, cmp_w2), to_sel_blocks(rs), rw_pad)
        return ctx, (rc, rs, rw[:, -min(WINDOW, sp):])

    y_prompt, conv_p, kv_p = trunk(x_prompt, pos_p, conv0, prompt_ctx, nsa_prompt)
    kv_cmp_p, kv_sel_p, kv_win_p = kv_p

    bd, sd, _ = x_sample.shape
    pos_s = PAST_LEN + jnp.arange(sd, dtype=jnp.int32)
    n_pages = PAST_LEN // PAGE_SIZE
    w_buf = cache_kv_win.shape[1]

    def gather_pages(pool):
        return pool[page_table].reshape((bd, n_pages * PAGE_SIZE) + pool.shape[2:])

    def sample_ctx(h):
        rc, rs, rw = shared_kv_rows(h, pos_s, kv_w)
        full_c = jnp.concatenate([gather_pages(cache_kv_cmp).astype(rc.dtype), rc], axis=1)
        full_s = jnp.concatenate([gather_pages(cache_kv_sel).astype(rs.dtype), rs], axis=1)
        kvw = jnp.concatenate([cache_kv_win.astype(rw.dtype), rw], axis=1)
        wpos = PAST_LEN - w_buf + jnp.arange(w_buf + sd, dtype=jnp.int32)
        ctx = (compress_kv(full_c, cmp_pe, cmp_w1, cmp_b1, cmp_w2), to_sel_blocks(full_s), kvw, wpos)
        return ctx, (rc, rs, kvw[:, -w_buf:])

    y_sample, conv_s, kv_s = trunk(x_sample, pos_s, state_conv, sample_ctx, nsa_sample)
    kv_cmp_s, kv_sel_s, kv_win_s = kv_s

    return (y_prompt, y_sample, conv_p, kv_cmp_p, kv_sel_p, kv_win_p, conv_s, kv_cmp_s, kv_sel_s, kv_win_s)
```

```python
import functools

import numpy as np
import jax
import jax.numpy as jnp
from jax import lax
from jax.experimental import pallas as pl
from jax.experimental.pallas import tpu as pltpu

F32 = jnp.float32
BF16 = jnp.bfloat16
HIGHEST = lax.Precision.HIGHEST

D_MODEL = 1024
DEPTH = 2
PAST_LEN = 8192
PAGE_SIZE = 128
N_HEADS = 16
N_KV_HEADS = 4
HEADS_PER_GROUP = N_HEADS // N_KV_HEADS
HEAD_DIM = D_MODEL // N_HEADS
ROT_DIM = HEAD_DIM // 4
ROPE_THETA = 500000.0
CMP_BLK = 32
CMP_STRIDE = 16
CMP_HID = 2 * HEAD_DIM
SEL_BLK = 64
SEL_TOPN = 16
WINDOW = 512
N_EXPERTS = 32
TOP_K = 4
D_FF = D_MODEL
SWIGLU_LIMIT = 7.0
SWIGLU_ALPHA = 1.702
ALPHA = (2 * DEPTH) ** 0.25
LN_EPS = 1e-5
NEG = -1e30
FORCE = 1e4

KV_COLS = 2 * N_KV_HEADS * HEAD_DIM
LANES = 128
VMEM_LIMIT = 56 * 2 ** 20


def _params(sem, vmem=VMEM_LIMIT):
    return pltpu.CompilerParams(dimension_semantics=sem, vmem_limit_bytes=vmem)


def _ln(x, g, b):
    mu = jnp.mean(x, axis=-1, keepdims=True)
    xc = x - mu
    var = jnp.mean(xc * xc, axis=-1, keepdims=True)
    return xc * lax.rsqrt(var + LN_EPS) * g + b


def _nt(a, b):
    return lax.dot_general(a, b, (((1,), (1,)), ((), ())), preferred_element_type=F32)


def _router(h, wr_ref, br_ref):
    return jnp.dot(h, wr_ref[...], precision=HIGHEST, preferred_element_type=F32) + br_ref[...]


def _full(shape):
    return pl.BlockSpec(shape, lambda *_: (0,) * len(shape))


def _conv_body(x_ref, p_ref, win_ref, wc_ref, wout_ref, lng_ref, lnb_ref, wr_ref, br_ref,
               h_ref, st_ref, lg_ref, carry_ref, *, seq, tm, rows_prev):
    i = pl.program_id(0)

    @pl.when(i == 0)
    def _():
        carry_ref[...] = jnp.zeros_like(carry_ref)

    x = x_ref[...]
    d = x.shape[1]
    z = jnp.dot(x.astype(BF16), win_ref[...], preferred_element_type=F32)
    bg, c, xh = z[:, :d], z[:, d:2 * d], z[:, 2 * d:]
    u = c * xh
    row = lax.broadcasted_iota(jnp.int32, (tm, 1), 0)
    t = (i * tm + row) & (seq - 1)
    um1 = pltpu.roll(u, 1, 0)
    um2 = pltpu.roll(u, 2, 0)
    c0 = carry_ref[0:1, :]
    c1 = carry_ref[1:2, :]
    um1 = jnp.where(row == 0, c1, um1)
    um2 = jnp.where(row == 0, c0, jnp.where(row == 1, c1, um2))
    if rows_prev:
        p0, p1 = p_ref[0], p_ref[1]
    else:
        p0, p1 = p_ref[0, 0:1, :], p_ref[0, 1:2, :]
    um1 = jnp.where(t >= 1, um1, p1)
    um2 = jnp.where(t >= 2, um2, jnp.where(t == 1, p1, p0))
    conv = wc_ref[0:1, :] * um2 + wc_ref[1:2, :] * um1 + wc_ref[2:3, :] * u
    y = jnp.dot((bg * conv).astype(BF16), wout_ref[...], preferred_element_type=F32)
    h = _ln(ALPHA * x + y, lng_ref[...], lnb_ref[...])
    h_ref[...] = h
    lg_ref[...] = _router(h, wr_ref, br_ref)
    carry_ref[0:2, :] = u[tm - 2:tm, :]
    if rows_prev:
        st_ref[...] = u
    else:
        st_ref[0] = u[tm - 2:tm, :]


def _conv_layer(x, prev, w_in, w_conv, w_out, lng, lnb, wr, br, *, seq, tm, rows_prev):
    n, d = x.shape
    assert n % tm == 0 and seq >= 2 and seq & (seq - 1) == 0
    assert (seq % tm == 0) if not rows_prev else (tm % seq == 0 and n == tm)
    if rows_prev:
        p_spec = pl.BlockSpec((2, tm, d), lambda i: (0, i, 0))
        st_shape = jax.ShapeDtypeStruct((n, d), F32)
        st_spec = pl.BlockSpec((tm, d), lambda i: (i, 0))
    else:
        per = seq // tm
        p_spec = pl.BlockSpec((1, 2, d), lambda i: (i // per, 0, 0))
        st_shape = jax.ShapeDtypeStruct((n // seq, 2, d), F32)
        st_spec = pl.BlockSpec((1, 2, d), lambda i: (i // per, 0, 0))
    return pl.pallas_call(
        functools.partial(_conv_body, seq=seq, tm=tm, rows_prev=rows_prev),
        grid=(n // tm,),
        in_specs=[pl.BlockSpec((tm, d), lambda i: (i, 0)), p_spec,
                  _full((d, 3 * d)), _full((3, d)), _full((d, d)), _full((1, d)), _full((1, d)),
                  _full((d, N_EXPERTS)), _full((1, N_EXPERTS))],
        out_specs=[pl.BlockSpec((tm, d), lambda i: (i, 0)), st_spec,
                   pl.BlockSpec((tm, N_EXPERTS), lambda i: (i, 0))],
        out_shape=[jax.ShapeDtypeStruct((n, d), F32), st_shape,
                   jax.ShapeDtypeStruct((n, N_EXPERTS), F32)],
        scratch_shapes=[pltpu.VMEM((8, d), F32)],
        compiler_params=_params(("arbitrary",)),
        name="conv_mixer",
    )(x, prev, w_in.astype(BF16), w_conv, w_out.astype(BF16), lng, lnb, wr, br)


def _moe_body(te_ref, nv_ref, x_ref, w1_ref, b1_ref, w2_ref, b2_ref, o_ref, w1b, w2b):
    i = pl.program_id(0)
    e = te_ref[i]
    prev = te_ref[jnp.maximum(i - 1, 0)]
    valid = i < nv_ref[0]

    @pl.when(valid & ((i == 0) | (e != prev)))
    def _():
        w1b[...] = w1_ref[0].astype(BF16)
        w2b[...] = w2_ref[0].astype(BF16)

    @pl.when(valid)
    def _():
        hgu = jnp.dot(x_ref[...], w1b[...], preferred_element_type=F32) + b1_ref[0]
        g = jnp.minimum(hgu[:, :D_FF], SWIGLU_LIMIT)
        u = jnp.clip(hgu[:, D_FF:], -SWIGLU_LIMIT, SWIGLU_LIMIT)
        a = (u + 1.0) * (g * jax.nn.sigmoid(SWIGLU_ALPHA * g))
        o_ref[...] = jnp.dot(a.astype(BF16), w2b[...], preferred_element_type=F32) + b2_ref[0]

    @pl.when(jnp.logical_not(valid))
    def _():
        o_ref[...] = jnp.zeros_like(o_ref)


def _moe_experts(xb, tile_e, n_valid, w1, b1, w2, b2, *, tm):
    n_rows, d = xb.shape
    n_tiles = n_rows // tm
    grid_spec = pltpu.PrefetchScalarGridSpec(
        num_scalar_prefetch=2,
        grid=(n_tiles,),
        in_specs=[pl.BlockSpec((tm, d), lambda i, te, nv: (i, 0)),
                  pl.BlockSpec((1, d, 2 * D_FF), lambda i, te, nv: (te[i], 0, 0)),
                  pl.BlockSpec((1, 1, 2 * D_FF), lambda i, te, nv: (te[i], 0, 0)),
                  pl.BlockSpec((1, D_FF, d), lambda i, te, nv: (te[i], 0, 0)),
                  pl.BlockSpec((1, 1, d), lambda i, te, nv: (te[i], 0, 0))],
        out_specs=pl.BlockSpec((tm, d), lambda i, te, nv: (i, 0)),
        scratch_shapes=[pltpu.VMEM((d, 2 * D_FF), BF16), pltpu.VMEM((D_FF, d), BF16)],
    )
    return pl.pallas_call(
        _moe_body, grid_spec=grid_spec,
        out_shape=jax.ShapeDtypeStruct((n_rows, d), F32),
        compiler_params=_params(("arbitrary",)),
        name="moe_experts",
    )(tile_e, n_valid, xb, w1, b1[:, None, :], w2, b2[:, None, :])


def _combine_body(h_ref, yg_ref, gate_ref, lng_ref, lnb_ref, o_ref):
    gate = gate_ref[...]
    f = gate[:, 0:1] * yg_ref[0]
    for k in range(1, TOP_K):
        f = f + gate[:, k:k + 1] * yg_ref[k]
    o_ref[...] = _ln(ALPHA * h_ref[...] + f, lng_ref[...], lnb_ref[...])


def _moe_combine(h, yg, gate, lng, lnb, *, tm):
    n, d = h.shape
    return pl.pallas_call(
        _combine_body, grid=(n // tm,),
        in_specs=[pl.BlockSpec((tm, d), lambda i: (i, 0)),
                  pl.BlockSpec((TOP_K, tm, d), lambda i: (0, i, 0)),
                  pl.BlockSpec((tm, TOP_K), lambda i: (i, 0)),
                  _full((1, d)), _full((1, d))],
        out_specs=pl.BlockSpec((tm, d), lambda i: (i, 0)),
        out_shape=jax.ShapeDtypeStruct((n, d), F32),
        compiler_params=_params(("arbitrary",)),
        name="moe_combine",
    )(h, yg, gate, lng, lnb)


def _moe_layer(h, logits, w1, b1, w2, b2, lng, lnb, *, tm_rows, tm_tok):
    n, d = h.shape
    top_v, top_e = lax.top_k(logits, TOP_K)
    gate = jax.nn.softmax(top_v, axis=-1)
    flat_e = top_e.reshape(-1)
    nk = n * TOP_K
    onehot = (flat_e[:, None] == jnp.arange(N_EXPERTS, dtype=jnp.int32)[None, :]).astype(jnp.int32)
    csum = jnp.cumsum(onehot, axis=0)
    rank = jnp.sum((csum - onehot) * onehot, axis=-1)
    counts = csum[-1]
    padded = (counts + tm_rows - 1) // tm_rows * tm_rows
    pad_end = jnp.cumsum(padded)
    pad_start = pad_end - padded
    dest = (pad_start[flat_e] + rank).astype(jnp.int32)
    n_tiles = (nk + N_EXPERTS * (tm_rows - 1) + tm_rows - 1) // tm_rows
    n_rows = n_tiles * tm_rows
    tile_e = jnp.minimum(jnp.searchsorted(pad_end, jnp.arange(n_tiles, dtype=jnp.int32) * tm_rows, side='right'),
                         N_EXPERTS - 1).astype(jnp.int32)
    n_valid = (pad_end[-1] // tm_rows).astype(jnp.int32).reshape(1)
    row_tok = jnp.full((n_rows,), n, jnp.int32).at[dest].set(jnp.arange(nk, dtype=jnp.int32) // TOP_K)
    xpad = jnp.concatenate([h.astype(BF16), jnp.zeros((1, d), BF16)], axis=0)
    xb = xpad[row_tok]
    yb = _moe_experts(xb, tile_e, n_valid, w1, b1, w2, b2, tm=tm_rows)
    yg = yb[dest.reshape(n, TOP_K).T]
    return _moe_combine(h, yg, gate, lng, lnb, tm=tm_tok)


def _proj_body(h_ref, w_ref, rc_ref, rs1_ref, rs2_ref, *out_refs, heads_out):
    z = jnp.dot(h_ref[...].astype(BF16), w_ref[...], preferred_element_type=F32)
    cc, s1, s2 = rc_ref[...], rs1_ref[...], rs2_ref[...]

    def rope(x):
        return x * cc + pltpu.roll(x, LANES - ROT_DIM // 2, 1) * s1 + pltpu.roll(x, ROT_DIM // 2, 1) * s2

    def rope_cols(lo, hi):
        return [rope(z[:, c:c + LANES]) for c in range(lo, hi, LANES)]

    if heads_out:
        rows_c, rows_s, rows_w, g_ref, q_ref, ks_ref, vs_ref, kw_ref, vw_ref = out_refs
    else:
        rows_c, rows_s, rows_w, g_ref, q_ref = out_refs
    half = KV_COLS // 2
    for br, rows in enumerate((rows_c, rows_s, rows_w)):
        base = br * KV_COLS
        k = jnp.concatenate(rope_cols(base, base + half), axis=1)
        v = z[:, base + half:base + KV_COLS]
        rows[...] = jnp.concatenate([k, v], axis=1)
        if heads_out and br >= 1:
            k_ref, v_ref = (ks_ref, vs_ref) if br == 1 else (kw_ref, vw_ref)
            for g in range(N_KV_HEADS):
                k_ref[0, g] = k[:, g * HEAD_DIM:(g + 1) * HEAD_DIM].astype(BF16)
                v_ref[0, g] = v[:, g * HEAD_DIM:(g + 1) * HEAD_DIM].astype(BF16)
    qb = 3 * KV_COLS
    q = jnp.concatenate(rope_cols(qb, qb + D_MODEL), axis=1) * (HEAD_DIM ** -0.5)
    if heads_out:
        for hd in range(N_HEADS):
            q_ref[0, hd] = q[:, hd * HEAD_DIM:(hd + 1) * HEAD_DIM].astype(BF16)
    else:
        q_ref[...] = q
    g_ref[...] = jax.nn.sigmoid(z[:, qb + D_MODEL:])


def _rope_tables(pos):
    half = ROT_DIM // 2
    inv = ROPE_THETA ** (-jnp.arange(half, dtype=F32) * 2.0 / ROT_DIM)
    ang = pos.astype(F32)[:, None] * inv[None, :]
    cos, sin = jnp.cos(ang), jnp.sin(ang)
    zeros = jnp.zeros((pos.shape[0], HEAD_DIM - ROT_DIM), F32)
    ones = jnp.ones_like(zeros)
    z8 = jnp.zeros_like(sin)
    c = jnp.concatenate([cos, cos, ones], axis=1)
    s1 = jnp.concatenate([-sin, z8, zeros], axis=1)
    s2 = jnp.concatenate([z8, sin, zeros], axis=1)
    rep = LANES // HEAD_DIM
    return tuple(jnp.tile(a, (1, rep)) for a in (c, s1, s2))


def _proj_weights(kv_w, wq, wg):
    wg4 = wg.reshape(D_MODEL, N_KV_HEADS, HEADS_PER_GROUP * 3)
    wg4 = jnp.pad(wg4, ((0, 0), (0, 0), (0, LANES - HEADS_PER_GROUP * 3))).reshape(D_MODEL, N_KV_HEADS * LANES)
    return jnp.concatenate([kv_w, wq, wg4], axis=1).astype(BF16)


def _nsa_proj(h, w_cat, tables, *, tm, seq, heads_out):
    n, d = h.shape
    ncol = w_cat.shape[1]
    per = tables[0].shape[0] // tm
    tab_spec = pl.BlockSpec((tm, LANES), lambda i: (i % per, 0))
    row_spec = lambda w: pl.BlockSpec((tm, w), lambda i: (i, 0))
    out_specs = [row_spec(KV_COLS)] * 3 + [row_spec(N_KV_HEADS * LANES)]
    out_shape = [jax.ShapeDtypeStruct((n, KV_COLS), F32)] * 3 + [jax.ShapeDtypeStruct((n, N_KV_HEADS * LANES), F32)]
    if heads_out:
        bsz = n // seq
        tps = seq // tm
        hspec = lambda nh: pl.BlockSpec((1, nh, tm, HEAD_DIM), lambda i: (i // tps, 0, i % tps, 0))
        out_specs += [hspec(N_HEADS)] + [hspec(N_KV_HEADS)] * 4
        out_shape += [jax.ShapeDtypeStruct((bsz, N_HEADS, seq, HEAD_DIM), BF16)]
        out_shape += [jax.ShapeDtypeStruct((bsz, N_KV_HEADS, seq, HEAD_DIM), BF16)] * 4
    else:
        out_specs += [row_spec(d)]
        out_shape += [jax.ShapeDtypeStruct((n, d), F32)]
    return pl.pallas_call(
        functools.partial(_proj_body, heads_out=heads_out), grid=(n // tm,),
        in_specs=[row_spec(d), _full((d, ncol)), tab_spec, tab_spec, tab_spec],
        out_specs=out_specs, out_shape=out_shape,
        compiler_params=_params(("arbitrary",)),
        name="nsa_proj",
    )(h, w_cat, *tables)


def _cmp_body(r_ref, w1_ref, pe_ref, w2_ref, o_ref, *, nch):
    k = pl.program_id(1) // 2
    x = jnp.concatenate(
        [r_ref[0, pl.ds(c, nch, stride=CMP_STRIDE), :].astype(BF16) for c in range(CMP_STRIDE)],
        axis=1)
    acc = jnp.dot(x, w1_ref[0], preferred_element_type=F32)
    pe = pe_ref[pl.ds(k, 1), :]
    hid = []
    for gs in range(2):
        pa = acc[:, gs * 2 * CMP_HID:gs * 2 * CMP_HID + CMP_HID]
        pb = acc[:, gs * 2 * CMP_HID + CMP_HID:(gs + 1) * 2 * CMP_HID]
        pb_next = pltpu.roll(pb, nch - 1, 0)
        hid.append(jax.nn.gelu(pa + pb_next + pe))
    hid = jnp.concatenate(hid, axis=1).astype(BF16)
    o_ref[0] = jnp.dot(hid, w2_ref[0], preferred_element_type=F32)


def _cmp_weights(cmp_pe, cmp_w1, cmp_b1, cmp_w2):
    eye = jnp.eye(2, dtype=F32)
    w1r = cmp_w1.reshape(2, 2, CMP_STRIDE, HEAD_DIM, CMP_HID)
    w1bd = jnp.einsum('ab,kncdh->kcadbnh', eye, w1r).reshape(2, CMP_STRIDE * 2 * HEAD_DIM, 4 * CMP_HID)
    w2bd = jnp.einsum('ab,khd->kahbd', eye, cmp_w2).reshape(2, 2 * CMP_HID, 2 * HEAD_DIM)
    pe_term = jnp.einsum('kcd,kcdh->kh', cmp_pe, cmp_w1, precision=HIGHEST) + cmp_b1
    return w1bd.astype(BF16), pe_term, w2bd.astype(BF16)


def _compress(rows, cw):
    bsz, t, _ = rows.shape
    nch = t // CMP_STRIDE
    w1bd, pe_term, w2bd = cw
    return pl.pallas_call(
        functools.partial(_cmp_body, nch=nch), grid=(bsz, KV_COLS // LANES),
        in_specs=[pl.BlockSpec((1, t, LANES), lambda b, j: (b, 0, j)),
                  pl.BlockSpec((1,) + w1bd.shape[1:], lambda b, j: (j // 2, 0, 0)),
                  _full(pe_term.shape),
                  pl.BlockSpec((1,) + w2bd.shape[1:], lambda b, j: (j // 2, 0, 0))],
        out_specs=pl.BlockSpec((1, nch, LANES), lambda b, j: (b, 0, j)),
        out_shape=jax.ShapeDtypeStruct((bsz, nch, KV_COLS), F32),
        compiler_params=_params(("arbitrary", "arbitrary")),
        name="compress_kv",
    )(rows, w1bd, pe_term, w2bd)


def _softmax_probs(s, m):
    s = jnp.where(m, s, NEG)
    mx = jnp.max(s, axis=-1, keepdims=True)
    e = jnp.where(m, jnp.exp(s - mx), 0.0)
    den = jnp.sum(e, axis=-1, keepdims=True)
    return e / jnp.maximum(den, 1e-30)


def _select_blocks(imp_t, qpos_row, n_rows):
    lanes = imp_t.shape[1]
    j = lax.broadcasted_iota(jnp.int32, (n_rows, lanes), 0)
    cur = lax.shift_right_logical(qpos_row, 6)
    valid = j <= cur
    forced = (j == 0) | (j == cur) | (j == cur - 1)
    sc = jnp.where(valid, jnp.where(forced, FORCE, imp_t[:n_rows]), NEG)
    return sc, j, valid


def _overlap_t(n_sel_pad, n_cmp_pad):
    c0 = np.arange(n_cmp_pad)[None, :] * CMP_STRIDE
    s0 = np.arange(n_sel_pad)[:, None] * SEL_BLK
    ov = np.clip(np.minimum(c0 + CMP_BLK, s0 + SEL_BLK) - np.maximum(c0, s0), 0, None).astype(np.float32) / CMP_BLK
    return ov


def _attn_p_body(q_ref, kck_ref, kcv_ref, ks_ref, vs_ref, kw_ref, vw_ref, g_ref, h_ref,
                 ovt_ref, exp_ref, wo_ref, lng_ref, lnb_ref, wr_ref, br_ref,
                 o_ref, lg_ref, o_scr, *, tq, seq):
    i = pl.program_id(1)
    g = pl.program_id(2)
    t0 = i * tq
    rows = HEADS_PER_GROUP * tq
    n_sel = seq // SEL_BLK
    q = q_ref[0].reshape(rows, HEAD_DIM)
    tpos = t0 + (lax.broadcasted_iota(jnp.int32, (rows, 1), 0) & (tq - 1))

    n_cmp = kck_ref.shape[2]
    s = _nt(q, kck_ref[0, 0])
    cend = lax.broadcasted_iota(jnp.int32, (1, n_cmp), 1) * CMP_STRIDE + (CMP_BLK - 1)
    p_c = _softmax_probs(s, cend <= tpos)
    o_c = jnp.dot(p_c.astype(BF16), kcv_ref[0, 0], preferred_element_type=F32)
    psum = p_c[0:tq]
    for qh in range(1, HEADS_PER_GROUP):
        psum = psum + p_c[qh * tq:(qh + 1) * tq]
    imp_t = lax.dot_general(ovt_ref[...], psum, (((1,), (1,)), ((), ())),
                            precision=HIGHEST, preferred_element_type=F32)

    qrow = t0 + lax.broadcasted_iota(jnp.int32, (1, tq), 1)
    sc, j, valid = _select_blocks(imp_t, qrow, n_sel)
    rank = jnp.zeros((n_sel, tq), jnp.int32)
    for jp in range(n_sel):
        r = sc[jp:jp + 1, :]
        beats = (r > sc) | ((r == sc) & (j > jp))
        rank = rank + beats.astype(jnp.int32)
    sel_t = (valid & (rank < SEL_TOPN)).astype(F32)
    sel_t = jnp.concatenate([sel_t, jnp.zeros((LANES - n_sel, tq), F32)], axis=0)
    sel = sel_t.T.astype(BF16)
    mk = jnp.dot(sel, exp_ref[...], preferred_element_type=F32)
    mk = jnp.concatenate([mk] * HEADS_PER_GROUP, axis=0) > 0.5

    s = _nt(q, ks_ref[0, 0])
    kpos = lax.broadcasted_iota(jnp.int32, (1, seq), 1)
    m = mk & (kpos <= tpos)
    s = jnp.where(m, s, NEG)
    e = jnp.where(m, jnp.exp(s - jnp.max(s, axis=-1, keepdims=True)), 0.0)
    o_s = jnp.dot(e.astype(BF16), vs_ref[0, 0], preferred_element_type=F32)
    o_s = o_s / jnp.sum(e, axis=-1, keepdims=True)

    span = WINDOW + tq
    ws = pl.multiple_of(jnp.maximum(t0 - WINDOW, 0), tq)
    kwin = kw_ref[0, 0, pl.ds(ws, span), :]
    vwin = vw_ref[0, 0, pl.ds(ws, span), :]
    s = _nt(q, kwin)
    wpos = ws + lax.broadcasted_iota(jnp.int32, (1, span), 1)
    m = (wpos <= tpos) & (wpos > tpos - WINDOW)
    s = jnp.where(m, s, NEG)
    e = jnp.where(m, jnp.exp(s - jnp.max(s, axis=-1, keepdims=True)), 0.0)
    o_w = jnp.dot(e.astype(BF16), vwin, preferred_element_type=F32)
    o_w = o_w / jnp.sum(e, axis=-1, keepdims=True)

    gates = g_ref[...]
    for qh in range(HEADS_PER_GROUP):
        sl = slice(qh * tq, (qh + 1) * tq)
        o_h = (gates[:, 3 * qh:3 * qh + 1] * o_c[sl] + gates[:, 3 * qh + 1:3 * qh + 2] * o_s[sl]
               + gates[:, 3 * qh + 2:3 * qh + 3] * o_w[sl])
        o_scr[g, :, qh * HEAD_DIM:(qh + 1) * HEAD_DIM] = o_h

    @pl.when(g == N_KV_HEADS - 1)
    def _():
        o = jnp.concatenate([o_scr[gg] for gg in range(N_KV_HEADS)], axis=1).astype(BF16)
        y = jnp.dot(o, wo_ref[...], preferred_element_type=F32)
        h = _ln(ALPHA * h_ref[...] + y, lng_ref[...], lnb_ref[...])
        o_ref[...] = h
        lg_ref[...] = _router(h, wr_ref, br_ref)


def _nsa_prompt(h, qh, kck, kcv, ks, vs, kw, vw, gates, wo, lng, lnb, wr, br, *, tq):
    n, d = h.shape
    bsz, _, seq, _ = ks.shape
    nt = seq // tq
    n_cmp = kck.shape[2]
    ovt = jnp.asarray(_overlap_t(LANES, n_cmp))
    expand = jnp.asarray((np.arange(LANES)[:, None] == (np.arange(seq)[None, :] // SEL_BLK)), BF16)
    grp = lambda rows_: pl.BlockSpec((1, 1, rows_, HEAD_DIM), lambda b, i, g: (b, g, 0, 0))
    row = lambda w: pl.BlockSpec((tq, w), lambda b, i, g: (b * nt + i, 0))
    return pl.pallas_call(
        functools.partial(_attn_p_body, tq=tq, seq=seq),
        grid=(bsz, nt, N_KV_HEADS),
        in_specs=[pl.BlockSpec((1, HEADS_PER_GROUP, tq, HEAD_DIM), lambda b, i, g: (b, g, i, 0)),
                  grp(n_cmp), grp(n_cmp), grp(seq), grp(seq), grp(seq), grp(seq),
                  pl.BlockSpec((tq, LANES), lambda b, i, g: (b * nt + i, g)), row(d),
                  _full(ovt.shape), _full(expand.shape), _full((d, d)), _full((1, d)), _full((1, d)),
                  _full((d, N_EXPERTS)), _full((1, N_EXPERTS))],
        out_specs=[row(d), row(N_EXPERTS)],
        out_shape=[jax.ShapeDtypeStruct((n, d), F32), jax.ShapeDtypeStruct((n, N_EXPERTS), F32)],
        scratch_shapes=[pltpu.VMEM((N_KV_HEADS, tq, HEADS_PER_GROUP * HEAD_DIM), F32)],
        compiler_params=_params(("arbitrary", "arbitrary", "arbitrary")),
        name="nsa_prompt",
    )(qh, kck, kcv, ks, vs, kw, vw, gates, h, ovt, expand, wo.astype(BF16), lng, lnb, wr, br)


def _attn_s_body(q_ref, gt_ref, kvc_ref, ps_ref, ns_ref, cw_ref, nw_ref, ovt_ref, exp_ref,
                 o_ref, sc_ref, *, dec, past):
    nrow = q_ref.shape[1]
    half = KV_COLS // 2
    q = q_ref[0]
    row = lax.broadcasted_iota(jnp.int32, (nrow, 1), 0)
    tok = row & (dec - 1)
    gsel = lax.shift_right_logical(row, 2) & (N_KV_HEADS - 1)
    qpos = past + tok

    def pick(o):
        out = jnp.zeros((nrow, HEAD_DIM), F32)
        for gg in range(N_KV_HEADS):
            out = out + jnp.where(gsel == gg, o[:, gg * HEAD_DIM:(gg + 1) * HEAD_DIM], 0.0)
        return out

    n_cmp = kvc_ref.shape[1]
    kc = kvc_ref[0, :, 0:half].astype(BF16)
    vc = kvc_ref[0, :, half:KV_COLS].astype(BF16)
    cend = lax.broadcasted_iota(jnp.int32, (1, n_cmp), 1) * CMP_STRIDE + (CMP_BLK - 1)
    p_c = _softmax_probs(_nt(q, kc), cend <= qpos)
    o_c = pick(jnp.dot(p_c.astype(BF16), vc, preferred_element_type=F32))
    ngt = N_KV_HEADS * dec
    psum = p_c[0:ngt]
    for qh in range(1, HEADS_PER_GROUP):
        psum = psum + p_c[qh * ngt:(qh + 1) * ngt]
    psum = jnp.concatenate([psum, jnp.zeros((LANES - ngt, n_cmp), F32)], axis=0)
    imp_t = lax.dot_general(ovt_ref[...], psum, (((1,), (1,)), ((), ())),
                            precision=HIGHEST, preferred_element_type=F32)

    n_sel = past // SEL_BLK + 1
    n_sel8 = (n_sel + 7) // 8 * 8
    lane_tok = lax.broadcasted_iota(jnp.int32, (1, LANES), 1) & (dec - 1)
    sc, j, valid = _select_blocks(imp_t, past + lane_tok, n_sel8)
    sc_ref[...] = sc

    def rank_step(jp, rank):
        r = sc_ref[pl.ds(jp, 1), :]
        beats = (r > sc) | ((r == sc) & (j > jp))
        return rank + beats.astype(jnp.int32)

    rank = lax.fori_loop(0, n_sel, rank_step, jnp.zeros((n_sel8, LANES), jnp.int32))
    sel_t = (valid & (rank < SEL_TOPN)).astype(F32)
    n_blk_pad = ovt_ref.shape[0]
    sel_t = jnp.concatenate([sel_t, jnp.zeros((n_blk_pad - n_sel8, LANES), F32)], axis=0)
    sel = sel_t.T[0:ngt, 0:past // SEL_BLK].astype(BF16)
    sel = jnp.concatenate([sel] * HEADS_PER_GROUP, axis=0)
    mk = jnp.dot(sel, exp_ref[...], preferred_element_type=F32) > 0.5

    def two_part(k_old, v_old, m_old, new_ref):
        k_new = new_ref[0, :, 0:half].astype(BF16)
        v_new = new_ref[0, :, half:KV_COLS].astype(BF16)
        s_o = jnp.where(m_old, _nt(q, k_old), NEG)
        m_new = lax.broadcasted_iota(jnp.int32, (1, new_ref.shape[1]), 1) <= tok
        s_n = jnp.where(m_new, _nt(q, k_new), NEG)
        mx = jnp.maximum(jnp.max(s_o, axis=-1, keepdims=True), jnp.max(s_n, axis=-1, keepdims=True))
        e_o = jnp.where(m_old, jnp.exp(s_o - mx), 0.0)
        e_n = jnp.where(m_new, jnp.exp(s_n - mx), 0.0)
        den = jnp.sum(e_o, axis=-1, keepdims=True) + jnp.sum(e_n, axis=-1, keepdims=True)
        o = (jnp.dot(e_o.astype(BF16), v_old, preferred_element_type=F32)
             + jnp.dot(e_n.astype(BF16), v_new, preferred_element_type=F32))
        return pick(o / den)

    o_s = two_part(ps_ref[0, :, 0:half], ps_ref[0, :, half:KV_COLS], mk, ns_ref)
    n_win = cw_ref.shape[1]
    m_w = lax.broadcasted_iota(jnp.int32, (1, n_win), 1) > tok + (n_win - WINDOW)
    o_w = two_part(cw_ref[0, :, 0:half].astype(BF16), cw_ref[0, :, half:KV_COLS].astype(BF16), m_w, nw_ref)
    gt = gt_ref[0]
    o_ref[0] = gt[:, 0:1] * o_c + gt[:, 1:2] * o_s + gt[:, 2:3] * o_w


def _nsa_sample(qbd, gt, kvc, past_s, new_s, cache_w, new_w, *, dec, past):
    bsz, nrow, _ = qbd.shape
    n_cmp = kvc.shape[1]
    n_blk_pad = 2 * LANES
    ovt = jnp.asarray(_overlap_t(n_blk_pad, n_cmp))
    expand = jnp.asarray((np.arange(past // SEL_BLK)[:, None] == (np.arange(past)[None, :] // SEL_BLK)), BF16)
    n_sel8 = (past // SEL_BLK + 1 + 7) // 8 * 8
    b3 = lambda a: pl.BlockSpec((1,) + a.shape[1:], lambda b: (b, 0, 0))
    return pl.pallas_call(
        functools.partial(_attn_s_body, dec=dec, past=past), grid=(bsz,),
        in_specs=[b3(qbd), b3(gt), b3(kvc), b3(past_s), b3(new_s), b3(cache_w), b3(new_w),
                  _full(ovt.shape), _full(expand.shape)],
        out_specs=pl.BlockSpec((1, nrow, HEAD_DIM), lambda b: (b, 0, 0)),
        out_shape=jax.ShapeDtypeStruct((bsz, nrow, HEAD_DIM), F32),
        scratch_shapes=[pltpu.VMEM((n_sel8, LANES), F32)],
        compiler_params=_params(("arbitrary",)),
        name="nsa_sample",
    )(qbd, gt, kvc, past_s, new_s, cache_w, new_w, ovt, expand)


def _out_body(o_ref, h_ref, wo_ref, lng_ref, lnb_ref, wr_ref, br_ref, y_ref, lg_ref):
    y = jnp.dot(o_ref[...].astype(BF16), wo_ref[...], preferred_element_type=F32)
    h = _ln(ALPHA * h_ref[...] + y, lng_ref[...], lnb_ref[...])
    y_ref[...] = h
    lg_ref[...] = _router(h, wr_ref, br_ref)


def _out_proj(o, h, wo, lng, lnb, wr, br):
    n, d = h.shape
    return pl.pallas_call(
        _out_body, grid=(1,),
        in_specs=[_full((n, d)), _full((n, d)), _full((d, d)), _full((1, d)), _full((1, d)),
                  _full((d, N_EXPERTS)), _full((1, N_EXPERTS))],
        out_specs=[_full((n, d)), _full((n, N_EXPERTS))],
        out_shape=[jax.ShapeDtypeStruct((n, d), F32), jax.ShapeDtypeStruct((n, N_EXPERTS), F32)],
        compiler_params=_params(("arbitrary",)),
        name="nsa_out_proj",
    )(o, h, wo.astype(BF16), lng, lnb, wr, br)


def kernel(x_prompt, x_sample, state_conv, cache_kv_cmp, cache_kv_sel, cache_kv_win, page_table,
           conv_w_in, conv_w, conv_w_out, kv_w, cmp_pe, cmp_w1, cmp_b1, cmp_w2,
           nsa_wq, nsa_wg, nsa_wo, moe_wr, moe_br, moe_w1, moe_b1, moe_w2, moe_b2, ln_g, ln_b):
    bp, sp, d = x_prompt.shape
    bd, sd, _ = x_sample.shape
    kv_shape = (2, N_KV_HEADS, HEAD_DIM)
    lng = lambda l, s: ln_g[l, s][None, :]
    lnb = lambda l, s: ln_b[l, s][None, :]
    moe = lambda l, h, lg, **kw: _moe_layer(h, lg, moe_w1[l], moe_b1[l], moe_w2[l], moe_b2[l],
                                            lng(l, 1), lnb(l, 1), **kw)
    w_cat = _proj_weights(kv_w, nsa_wq[0], nsa_wg[0])
    cw = _cmp_weights(cmp_pe, cmp_w1, cmp_b1, cmp_w2)
    br = lambda l: moe_br[l][None, :]

    n_p = bp * sp
    xp = x_prompt.reshape(n_p, d)
    h, conv_p, lg = _conv_layer(xp, jnp.zeros((bp, 2, d), F32), conv_w_in[0], conv_w[0], conv_w_out[0],
                                lng(0, 0), lnb(0, 0), moe_wr[0], br(0), seq=sp, tm=256, rows_prev=False)
    h = moe(0, h, lg, tm_rows=256, tm_tok=256)
    tabs = _rope_tables(jnp.arange(sp, dtype=jnp.int32))
    rc, rs, rw, gates, qh, ks, vs, kw, vw = _nsa_proj(h, w_cat, tabs, tm=256, seq=sp, heads_out=True)
    kvc = _compress(rc.reshape(bp, sp, KV_COLS), cw)
    kvc_h = kvc.reshape(bp, -1, 2, N_KV_HEADS, HEAD_DIM).transpose(2, 0, 3, 1, 4).astype(BF16)
    h, lg = _nsa_prompt(h, qh, kvc_h[0], kvc_h[1], ks, vs, kw, vw, gates, nsa_wo[0],
                        lng(1, 0), lnb(1, 0), moe_wr[1], br(1), tq=128)
    y_prompt = moe(1, h, lg, tm_rows=256, tm_tok=256).reshape(bp, sp, d)
    kv_cmp_p = rc.reshape((bp, sp) + kv_shape)
    kv_sel_p = rs.reshape((bp, sp) + kv_shape)
    kv_win_p = rw.reshape((bp, sp) + kv_shape)[:, -min(WINDOW, sp):]

    n_s = bd * sd
    xs = x_sample.reshape(n_s, d)
    prev_rows = jnp.repeat(state_conv[0].transpose(1, 0, 2), sd, axis=1)
    h, u_s, lg = _conv_layer(xs, prev_rows, conv_w_in[0], conv_w[0], conv_w_out[0],
                             lng(0, 0), lnb(0, 0), moe_wr[0], br(0), seq=sd, tm=n_s, rows_prev=True)
    conv_s = u_s.reshape(bd, sd, d)[:, sd - 2:]
    h = moe(0, h, lg, tm_rows=64, tm_tok=n_s)
    pos_s = PAST_LEN + (jnp.arange(n_s, dtype=jnp.int32) % sd)
    rc_s, rs_s, rw_s, gates_s, q_s = _nsa_proj(h, w_cat, _rope_tables(pos_s), tm=n_s, seq=sd, heads_out=False)

    n_pages = PAST_LEN // PAGE_SIZE
    gather = lambda pool: pool.reshape(pool.shape[0], PAGE_SIZE, KV_COLS)[page_table].reshape(
        bd, n_pages * PAGE_SIZE, KV_COLS)
    kvc_s = _compress(gather(cache_kv_cmp), cw)
    past_s = gather(cache_kv_sel).astype(BF16)
    q5 = q_s.reshape(bd, sd, N_KV_HEADS, HEADS_PER_GROUP, HEAD_DIM).transpose(0, 3, 2, 1, 4)
    eye = jnp.eye(N_KV_HEADS, dtype=F32)
    qbd = jnp.einsum('bqgtd,gh->bqgthd', q5, eye).reshape(bd, N_HEADS * sd, N_KV_HEADS * HEAD_DIM).astype(BF16)
    g5 = gates_s.reshape(bd, sd, N_KV_HEADS, LANES)[..., :HEADS_PER_GROUP * 3]
    g5 = g5.reshape(bd, sd, N_KV_HEADS, HEADS_PER_GROUP, 3).transpose(0, 3, 2, 1, 4).reshape(bd, N_HEADS * sd, 3)
    gt = jnp.pad(g5, ((0, 0), (0, 0), (0, 5)))
    pad_new = lambda r: jnp.pad(r.reshape(bd, sd, KV_COLS), ((0, 0), (0, 16 - sd), (0, 0)))
    w_buf = cache_kv_win.shape[1]
    cache_w = cache_kv_win.reshape(bd, w_buf, KV_COLS)
    o_s = _nsa_sample(qbd, gt, kvc_s, past_s, pad_new(rs_s), cache_w, pad_new(rw_s), dec=sd, past=PAST_LEN)
    o_s = o_s.reshape(bd, HEADS_PER_GROUP, N_KV_HEADS, sd, HEAD_DIM).transpose(0, 3, 2, 1, 4).reshape(n_s, d)
    h, lg = _out_proj(o_s, h, nsa_wo[0], lng(1, 0), lnb(1, 0), moe_wr[1], br(1))
    y_sample = moe(1, h, lg, tm_rows=64, tm_tok=n_s).reshape(bd, sd, d)
    kv_cmp_s = rc_s.reshape((bd, sd) + kv_shape)
    kv_sel_s = rs_s.reshape((bd, sd) + kv_shape)
    kv_win_s = jnp.concatenate([cache_kv_win, rw_s.reshape((bd, sd) + kv_shape)], axis=1)[:, -w_buf:]

    return (y_prompt, y_sample, conv_p[None], kv_cmp_p, kv_sel_p, kv_win_p,
            conv_s[None], kv_cmp_s, kv_sel_s, kv_win_s)
```

```python
import functools

import numpy as np
import jax
import jax.numpy as jnp
from jax import lax
from jax.experimental import pallas as pl
from jax.experimental.pallas import tpu as pltpu

F32 = jnp.float32
BF16 = jnp.bfloat16
HIGHEST = lax.Precision.HIGHEST

D_MODEL = 1024
DEPTH = 2
PAST_LEN = 8192
PAGE_SIZE = 128
N_HEADS = 16
N_KV_HEADS = 4
HEADS_PER_GROUP = N_HEADS // N_KV_HEADS
HEAD_DIM = D_MODEL // N_HEADS
ROT_DIM = HEAD_DIM // 4
ROPE_THETA = 500000.0
CMP_BLK = 32
CMP_STRIDE = 16
CMP_HID = 2 * HEAD_DIM
SEL_BLK = 64
SEL_TOPN = 16
WINDOW = 512
N_EXPERTS = 32
TOP_K = 4
D_FF = D_MODEL
SWIGLU_LIMIT = 7.0
SWIGLU_ALPHA = 1.702
ALPHA = (2 * DEPTH) ** 0.25
LN_EPS = 1e-5
NEG = -1e30
FORCE = 1e4

KV_COLS = 2 * N_KV_HEADS * HEAD_DIM
LANES = 128
VMEM_LIMIT = 56 * 2 ** 20


def _params(sem, vmem=VMEM_LIMIT):
    return pltpu.CompilerParams(dimension_semantics=sem, vmem_limit_bytes=vmem)


def _ln(x, g, b):
    mu = jnp.mean(x, axis=-1, keepdims=True)
    xc = x - mu
    var = jnp.mean(xc * xc, axis=-1, keepdims=True)
    return xc * lax.rsqrt(var + LN_EPS) * g + b


def _nt(a, b):
    return lax.dot_general(a, b, (((1,), (1,)), ((), ())), preferred_element_type=F32)


def _router(h, wr_ref, br_ref):
    return jnp.dot(h, wr_ref[...], precision=HIGHEST, preferred_element_type=F32) + br_ref[...]


def _full(shape):
    return pl.BlockSpec(shape, lambda *_: (0,) * len(shape))


def _conv_body(x_ref, p_ref, win_ref, wc_ref, wout_ref, lng_ref, lnb_ref, wr_ref, br_ref,
               h_ref, st_ref, lg_ref, carry_ref, *, seq, tm, rows_prev):
    i = pl.program_id(0)

    @pl.when(i == 0)
    def _():
        carry_ref[...] = jnp.zeros_like(carry_ref)

    x = x_ref[...]
    d = x.shape[1]
    z = jnp.dot(x.astype(BF16), win_ref[...], preferred_element_type=F32)
    bg, c, xh = z[:, :d], z[:, d:2 * d], z[:, 2 * d:]
    u = c * xh
    row = lax.broadcasted_iota(jnp.int32, (tm, 1), 0)
    t = (i * tm + row) & (seq - 1)
    um1 = pltpu.roll(u, 1, 0)
    um2 = pltpu.roll(u, 2, 0)
    c0 = carry_ref[0:1, :]
    c1 = carry_ref[1:2, :]
    um1 = jnp.where(row == 0, c1, um1)
    um2 = jnp.where(row == 0, c0, jnp.where(row == 1, c1, um2))
    if rows_prev:
        p0, p1 = p_ref[0], p_ref[1]
    else:
        p0, p1 = p_ref[0, 0:1, :], p_ref[0, 1:2, :]
    um1 = jnp.where(t >= 1, um1, p1)
    um2 = jnp.where(t >= 2, um2, jnp.where(t == 1, p1, p0))
    conv = wc_ref[0:1, :] * um2 + wc_ref[1:2, :] * um1 + wc_ref[2:3, :] * u
    y = jnp.dot((bg * conv).astype(BF16), wout_ref[...], preferred_element_type=F32)
    h = _ln(ALPHA * x + y, lng_ref[...], lnb_ref[...])
    h_ref[...] = h
    lg_ref[...] = _router(h, wr_ref, br_ref)
    carry_ref[0:2, :] = u[tm - 2:tm, :]
    if rows_prev:
        st_ref[...] = u
    else:
        st_ref[0] = u[tm - 2:tm, :]


def _conv_layer(x, prev, w_in, w_conv, w_out, lng, lnb, wr, br, *, seq, tm, rows_prev):
    n, d = x.shape
    assert n % tm == 0 and seq >= 2 and seq & (seq - 1) == 0
    assert (seq % tm == 0) if not rows_prev else (tm % seq == 0 and n == tm)
    if rows_prev:
        p_spec = pl.BlockSpec((2, tm, d), lambda i: (0, i, 0))
        st_shape = jax.ShapeDtypeStruct((n, d), F32)
        st_spec = pl.BlockSpec((tm, d), lambda i: (i, 0))
    else:
        per = seq // tm
        p_spec = pl.BlockSpec((1, 2, d), lambda i: (i // per, 0, 0))
        st_shape = jax.ShapeDtypeStruct((n // seq, 2, d), F32)
        st_spec = pl.BlockSpec((1, 2, d), lambda i: (i // per, 0, 0))
    return pl.pallas_call(
        functools.partial(_conv_body, seq=seq, tm=tm, rows_prev=rows_prev),
        grid=(n // tm,),
        in_specs=[pl.BlockSpec((tm, d), lambda i: (i, 0)), p_spec,
                  _full((d, 3 * d)), _full((3, d)), _full((d, d)), _full((1, d)), _full((1, d)),
                  _full((d, N_EXPERTS)), _full((1, N_EXPERTS))],
        out_specs=[pl.BlockSpec((tm, d), lambda i: (i, 0)), st_spec,
                   pl.BlockSpec((tm, N_EXPERTS), lambda i: (i, 0))],
        out_shape=[jax.ShapeDtypeStruct((n, d), F32), st_shape,
                   jax.ShapeDtypeStruct((n, N_EXPERTS), F32)],
        scratch_shapes=[pltpu.VMEM((8, d), F32)],
        compiler_params=_params(("arbitrary",)),
        name="conv_mixer",
    )(x, prev, w_in.astype(BF16), w_conv, w_out.astype(BF16), lng, lnb, wr, br)


def _moe_body(te_ref, nv_ref, x_ref, w1_ref, b1_ref, w2_ref, b2_ref, o_ref, w1b, w2b):
    i = pl.program_id(0)
    e = te_ref[i]
    prev = te_ref[jnp.maximum(i - 1, 0)]
    valid = i < nv_ref[0]

    @pl.when(valid & ((i == 0) | (e != prev)))
    def _():
        w1b[...] = w1_ref[0, 0].astype(BF16)
        w2b[...] = w2_ref[0, 0].astype(BF16)

    @pl.when(valid)
    def _():
        hgu = jnp.dot(x_ref[...].astype(BF16), w1b[...], preferred_element_type=F32) + b1_ref[0, 0]
        g = jnp.minimum(hgu[:, :D_FF], SWIGLU_LIMIT)
        u = jnp.clip(hgu[:, D_FF:], -SWIGLU_LIMIT, SWIGLU_LIMIT)
        a = (u + 1.0) * (g * jax.nn.sigmoid(SWIGLU_ALPHA * g))
        o_ref[...] = jnp.dot(a.astype(BF16), w2b[...], preferred_element_type=F32) + b2_ref[0, 0]

    @pl.when(jnp.logical_not(valid))
    def _():
        o_ref[...] = jnp.zeros_like(o_ref)


def _moe_experts(xb, tile_e, n_valid, w1, b1, w2, b2, *, layer, tm):
    n_rows, d = xb.shape
    n_tiles = n_rows // tm
    grid_spec = pltpu.PrefetchScalarGridSpec(
        num_scalar_prefetch=2,
        grid=(n_tiles,),
        in_specs=[pl.BlockSpec((tm, d), lambda i, te, nv: (i, 0)),
                  pl.BlockSpec((1, 1, d, 2 * D_FF), lambda i, te, nv: (layer, te[i], 0, 0)),
                  pl.BlockSpec((1, 1, 1, 2 * D_FF), lambda i, te, nv: (layer, te[i], 0, 0)),
                  pl.BlockSpec((1, 1, D_FF, d), lambda i, te, nv: (layer, te[i], 0, 0)),
                  pl.BlockSpec((1, 1, 1, d), lambda i, te, nv: (layer, te[i], 0, 0))],
        out_specs=pl.BlockSpec((tm, d), lambda i, te, nv: (i, 0)),
        scratch_shapes=[pltpu.VMEM((d, 2 * D_FF), BF16), pltpu.VMEM((D_FF, d), BF16)],
    )
    return pl.pallas_call(
        _moe_body, grid_spec=grid_spec,
        out_shape=jax.ShapeDtypeStruct((n_rows, d), F32),
        compiler_params=_params(("arbitrary",)),
        name="moe_experts",
    )(tile_e, n_valid, xb, w1, b1[:, :, None, :], w2, b2[:, :, None, :])


def _combine_body(h_ref, yg_ref, gate_ref, lng_ref, lnb_ref, o_ref):
    gate = gate_ref[...]
    f = gate[:, 0:1] * yg_ref[0]
    for k in range(1, TOP_K):
        f = f + gate[:, k:k + 1] * yg_ref[k]
    o_ref[...] = _ln(ALPHA * h_ref[...] + f, lng_ref[...], lnb_ref[...])


def _moe_combine(h, yg, gate, lng, lnb, *, tm):
    n, d = h.shape
    return pl.pallas_call(
        _combine_body, grid=(n // tm,),
        in_specs=[pl.BlockSpec((tm, d), lambda i: (i, 0)),
                  pl.BlockSpec((TOP_K, tm, d), lambda i: (0, i, 0)),
                  pl.BlockSpec((tm, TOP_K), lambda i: (i, 0)),
                  _full((1, d)), _full((1, d))],
        out_specs=pl.BlockSpec((tm, d), lambda i: (i, 0)),
        out_shape=jax.ShapeDtypeStruct((n, d), F32),
        compiler_params=_params(("arbitrary",)),
        name="moe_combine",
    )(h, yg, gate, lng, lnb)


def _moe_layer(h, logits, w1, b1, w2, b2, lng, lnb, *, layer, tm_rows, tm_tok):
    n, d = h.shape
    top_v, top_e = lax.top_k(logits, TOP_K)
    gate = jax.nn.softmax(top_v, axis=-1)
    flat_e = top_e.reshape(-1)
    nk = n * TOP_K
    onehot = (flat_e[:, None] == jnp.arange(N_EXPERTS, dtype=jnp.int32)[None, :]).astype(jnp.int32)
    csum = jnp.cumsum(onehot, axis=0)
    rank = jnp.sum((csum - onehot) * onehot, axis=-1)
    counts = csum[-1]
    padded = (counts + tm_rows - 1) // tm_rows * tm_rows
    pad_end = jnp.cumsum(padded)
    pad_start = pad_end - padded
    dest = (pad_start[flat_e] + rank).astype(jnp.int32)
    n_tiles = (nk + N_EXPERTS * (tm_rows - 1) + tm_rows - 1) // tm_rows
    n_rows = n_tiles * tm_rows
    tile_start = jnp.arange(n_tiles, dtype=jnp.int32) * tm_rows
    tile_e = jnp.minimum(jnp.sum((pad_end[None, :] <= tile_start[:, None]).astype(jnp.int32), axis=1), N_EXPERTS - 1)
    n_valid = (pad_end[-1] // tm_rows).astype(jnp.int32).reshape(1)
    row_tok = jnp.full((n_rows,), n, jnp.int32).at[dest].set(jnp.arange(nk, dtype=jnp.int32) // TOP_K)
    xpad = jnp.concatenate([h, jnp.zeros((1, d), F32)], axis=0)
    xb = xpad[row_tok]
    yb = _moe_experts(xb, tile_e, n_valid, w1, b1, w2, b2, layer=layer, tm=tm_rows)
    yg = yb[dest.reshape(n, TOP_K).T]
    return _moe_combine(h, yg, gate, lng, lnb, tm=tm_tok)


def _proj_body(h_ref, w_ref, rc_ref, rs1_ref, rs2_ref, *out_refs, heads_out):
    z = jnp.dot(h_ref[...].astype(BF16), w_ref[...], preferred_element_type=F32)
    cc, s1, s2 = rc_ref[...], rs1_ref[...], rs2_ref[...]

    def rope(x):
        return x * cc + pltpu.roll(x, LANES - ROT_DIM // 2, 1) * s1 + pltpu.roll(x, ROT_DIM // 2, 1) * s2

    def rope_cols(lo, hi):
        return [rope(z[:, c:c + LANES]) for c in range(lo, hi, LANES)]

    if heads_out:
        rows_c, rows_s, rows_w, g_ref, q_ref, ks_ref, vs_ref, kw_ref, vw_ref = out_refs
    else:
        rows_c, rows_s, rows_w, g_ref, q_ref = out_refs
    half = KV_COLS // 2
    for br, rows in enumerate((rows_c, rows_s, rows_w)):
        base = br * KV_COLS
        k = jnp.concatenate(rope_cols(base, base + half), axis=1)
        v = z[:, base + half:base + KV_COLS]
        rows[...] = jnp.concatenate([k, v], axis=1)
        if heads_out and br >= 1:
            k_ref, v_ref = (ks_ref, vs_ref) if br == 1 else (kw_ref, vw_ref)
            for g in range(N_KV_HEADS):
                k_ref[0, g] = k[:, g * HEAD_DIM:(g + 1) * HEAD_DIM].astype(BF16)
                v_ref[0, g] = v[:, g * HEAD_DIM:(g + 1) * HEAD_DIM].astype(BF16)
    qb = 3 * KV_COLS
    q = jnp.concatenate(rope_cols(qb, qb + D_MODEL), axis=1) * (HEAD_DIM ** -0.5)
    if heads_out:
        for hd in range(N_HEADS):
            q_ref[0, hd] = q[:, hd * HEAD_DIM:(hd + 1) * HEAD_DIM].astype(BF16)
    else:
        q_ref[...] = q
    g_ref[...] = jax.nn.sigmoid(z[:, qb + D_MODEL:])


def _rope_tables(pos):
    half = ROT_DIM // 2
    inv = ROPE_THETA ** (-jnp.arange(half, dtype=F32) * 2.0 / ROT_DIM)
    ang = pos.astype(F32)[:, None] * inv[None, :]
    cos, sin = jnp.cos(ang), jnp.sin(ang)
    zeros = jnp.zeros((pos.shape[0], HEAD_DIM - ROT_DIM), F32)
    ones = jnp.ones_like(zeros)
    z8 = jnp.zeros_like(sin)
    c = jnp.concatenate([cos, cos, ones], axis=1)
    s1 = jnp.concatenate([-sin, z8, zeros], axis=1)
    s2 = jnp.concatenate([z8, sin, zeros], axis=1)
    rep = LANES // HEAD_DIM
    return tuple(jnp.tile(a, (1, rep)) for a in (c, s1, s2))


def _proj_weights(kv_w, wq, wg):
    wg4 = wg.reshape(D_MODEL, N_KV_HEADS, HEADS_PER_GROUP * 3)
    wg4 = jnp.pad(wg4, ((0, 0), (0, 0), (0, LANES - HEADS_PER_GROUP * 3))).reshape(D_MODEL, N_KV_HEADS * LANES)
    return jnp.concatenate([kv_w, wq, wg4], axis=1).astype(BF16)


def _nsa_proj(h, w_cat, tables, *, tm, seq, heads_out):
    n, d = h.shape
    ncol = w_cat.shape[1]
    per = tables[0].shape[0] // tm
    tab_spec = pl.BlockSpec((tm, LANES), lambda i: (i % per, 0))
    row_spec = lambda w: pl.BlockSpec((tm, w), lambda i: (i, 0))
    out_specs = [row_spec(KV_COLS)] * 3 + [row_spec(N_KV_HEADS * LANES)]
    out_shape = [jax.ShapeDtypeStruct((n, KV_COLS), F32)] * 3 + [jax.ShapeDtypeStruct((n, N_KV_HEADS * LANES), F32)]
    if heads_out:
        bsz = n // seq
        tps = seq // tm
        hspec = lambda nh: pl.BlockSpec((1, nh, tm, HEAD_DIM), lambda i: (i // tps, 0, i % tps, 0))
        out_specs += [hspec(N_HEADS)] + [hspec(N_KV_HEADS)] * 4
        out_shape += [jax.ShapeDtypeStruct((bsz, N_HEADS, seq, HEAD_DIM), BF16)]
        out_shape += [jax.ShapeDtypeStruct((bsz, N_KV_HEADS, seq, HEAD_DIM), BF16)] * 4
    else:
        out_specs += [row_spec(d)]
        out_shape += [jax.ShapeDtypeStruct((n, d), F32)]
    return pl.pallas_call(
        functools.partial(_proj_body, heads_out=heads_out), grid=(n // tm,),
        in_specs=[row_spec(d), _full((d, ncol)), tab_spec, tab_spec, tab_spec],
        out_specs=out_specs, out_shape=out_shape,
        compiler_params=_params(("arbitrary",)),
        name="nsa_proj",
    )(h, w_cat, *tables)


def _cmp_body(r_ref, w1_ref, pe_ref, w2_ref, o_ref, *, nch):
    k = pl.program_id(1) // 2
    x = jnp.concatenate(
        [r_ref[0, pl.ds(c, nch, stride=CMP_STRIDE), :].astype(BF16) for c in range(CMP_STRIDE)],
        axis=1)
    acc = jnp.dot(x, w1_ref[0], preferred_element_type=F32)
    pe = pe_ref[pl.ds(k, 1), :]
    hid = []
    for gs in range(2):
        pa = acc[:, gs * 2 * CMP_HID:gs * 2 * CMP_HID + CMP_HID]
        pb = acc[:, gs * 2 * CMP_HID + CMP_HID:(gs + 1) * 2 * CMP_HID]
        pb_next = pltpu.roll(pb, nch - 1, 0)
        hid.append(jax.nn.gelu(pa + pb_next + pe))
    hid = jnp.concatenate(hid, axis=1).astype(BF16)
    o_ref[0] = jnp.dot(hid, w2_ref[0], preferred_element_type=F32)


def _cmp_weights(cmp_pe, cmp_w1, cmp_b1, cmp_w2):
    eye = jnp.eye(2, dtype=F32)
    w1r = cmp_w1.reshape(2, 2, CMP_STRIDE, HEAD_DIM, CMP_HID)
    w1bd = jnp.einsum('ab,kncdh->kcadbnh', eye, w1r).reshape(2, CMP_STRIDE * 2 * HEAD_DIM, 4 * CMP_HID)
    w2bd = jnp.einsum('ab,khd->kahbd', eye, cmp_w2).reshape(2, 2 * CMP_HID, 2 * HEAD_DIM)
    pe_term = jnp.einsum('kcd,kcdh->kh', cmp_pe, cmp_w1, precision=HIGHEST) + cmp_b1
    return w1bd.astype(BF16), pe_term, w2bd.astype(BF16)


def _compress(rows, cw):
    bsz, t, _ = rows.shape
    nch = t // CMP_STRIDE
    w1bd, pe_term, w2bd = cw
    return pl.pallas_call(
        functools.partial(_cmp_body, nch=nch), grid=(bsz, KV_COLS // LANES),
        in_specs=[pl.BlockSpec((1, t, LANES), lambda b, j: (b, 0, j)),
                  pl.BlockSpec((1,) + w1bd.shape[1:], lambda b, j: (j // 2, 0, 0)),
                  _full(pe_term.shape),
                  pl.BlockSpec((1,) + w2bd.shape[1:], lambda b, j: (j // 2, 0, 0))],
        out_specs=pl.BlockSpec((1, nch, LANES), lambda b, j: (b, 0, j)),
        out_shape=jax.ShapeDtypeStruct((bsz, nch, KV_COLS), F32),
        compiler_params=_params(("arbitrary", "arbitrary")),
        name="compress_kv",
    )(rows, w1bd, pe_term, w2bd)


def _softmax_probs(s, m):
    s = jnp.where(m, s, NEG)
    mx = jnp.max(s, axis=-1, keepdims=True)
    e = jnp.where(m, jnp.exp(s - mx), 0.0)
    den = jnp.sum(e, axis=-1, keepdims=True)
    return e / jnp.maximum(den, 1e-30)


def _select_blocks(imp_t, qpos_row, n_rows):
    lanes = imp_t.shape[1]
    j = lax.broadcasted_iota(jnp.int32, (n_rows, lanes), 0)
    cur = lax.shift_right_logical(qpos_row, 6)
    valid = j <= cur
    forced = (j == 0) | (j == cur) | (j == cur - 1)
    sc = jnp.where(valid, jnp.where(forced, FORCE, imp_t[:n_rows]), NEG)
    return sc, j, valid


def _overlap_t(n_sel_pad, n_cmp_pad):
    c0 = np.arange(n_cmp_pad)[None, :] * CMP_STRIDE
    s0 = np.arange(n_sel_pad)[:, None] * SEL_BLK
    ov = np.clip(np.minimum(c0 + CMP_BLK, s0 + SEL_BLK) - np.maximum(c0, s0), 0, None).astype(np.float32) / CMP_BLK
    return ov


def _attn_p_body(q_ref, kck_ref, kcv_ref, ks_ref, vs_ref, kw_ref, vw_ref, g_ref, h_ref,
                 ovt_ref, exp_ref, wo_ref, lng_ref, lnb_ref, wr_ref, br_ref,
                 o_ref, lg_ref, o_scr, *, tq, seq):
    i = pl.program_id(1)
    g = pl.program_id(2)
    t0 = i * tq
    rows = HEADS_PER_GROUP * tq
    n_sel = seq // SEL_BLK
    q = q_ref[0].reshape(rows, HEAD_DIM)
    tpos = t0 + (lax.broadcasted_iota(jnp.int32, (rows, 1), 0) & (tq - 1))

    n_cmp = kck_ref.shape[2]
    s = _nt(q, kck_ref[0, 0])
    cend = lax.broadcasted_iota(jnp.int32, (1, n_cmp), 1) * CMP_STRIDE + (CMP_BLK - 1)
    p_c = _softmax_probs(s, cend <= tpos)
    o_c = jnp.dot(p_c.astype(BF16), kcv_ref[0, 0], preferred_element_type=F32)
    psum = p_c[0:tq]
    for qh in range(1, HEADS_PER_GROUP):
        psum = psum + p_c[qh * tq:(qh + 1) * tq]
    imp_t = lax.dot_general(ovt_ref[...], psum, (((1,), (1,)), ((), ())),
                            precision=HIGHEST, preferred_element_type=F32)

    qrow = t0 + lax.broadcasted_iota(jnp.int32, (1, tq), 1)
    sc, j, valid = _select_blocks(imp_t, qrow, n_sel)
    rank = jnp.zeros((n_sel, tq), jnp.int32)
    for jp in range(n_sel):
        r = sc[jp:jp + 1, :]
        beats = (r > sc) | ((r == sc) & (j > jp))
        rank = rank + beats.astype(jnp.int32)
    sel_t = (valid & (rank < SEL_TOPN)).astype(F32)
    sel_t = jnp.concatenate([sel_t, jnp.zeros((LANES - n_sel, tq), F32)], axis=0)
    sel = sel_t.T.astype(BF16)
    tq_pos = t0 + lax.broadcasted_iota(jnp.int32, (tq, 1), 0)

    kc = exp_ref.shape[2]

    def sel_chunk(c, carry):
        m_run, l_run, acc = carry
        k0 = pl.multiple_of(c * kc, kc)
        mk = jnp.dot(sel, exp_ref[c], preferred_element_type=F32)
        kpos = k0 + lax.broadcasted_iota(jnp.int32, (1, kc), 1)
        bias = jnp.where((mk > 0.5) & (kpos <= tq_pos), 0.0, NEG)
        s = _nt(q, ks_ref[0, 0, pl.ds(k0, kc), :]) + jnp.concatenate([bias] * HEADS_PER_GROUP, axis=0)
        m_new = jnp.maximum(m_run, jnp.max(s, axis=-1, keepdims=True))
        alpha = jnp.exp(m_run - m_new)
        e = jnp.exp(s - m_new)
        l_new = alpha * l_run + jnp.sum(e, axis=-1, keepdims=True)
        pv = jnp.dot(e.astype(BF16), vs_ref[0, 0, pl.ds(k0, kc), :], preferred_element_type=F32)
        return m_new, l_new, alpha * acc + pv

    n_kc = (t0 + tq + kc - 1) // kc
    init = (jnp.full((rows, 1), NEG, F32), jnp.zeros((rows, 1), F32), jnp.zeros((rows, HEAD_DIM), F32))
    _, l_s, acc_s = lax.fori_loop(0, n_kc, sel_chunk, init)
    o_s = acc_s / l_s

    span = WINDOW + tq
    ws = pl.multiple_of(jnp.maximum(t0 - WINDOW, 0), tq)
    wpos = ws + lax.broadcasted_iota(jnp.int32, (1, span), 1)
    bias = jnp.where((wpos <= tq_pos) & (wpos > tq_pos - WINDOW), 0.0, NEG)
    s = _nt(q, kw_ref[0, 0, pl.ds(ws, span), :]) + jnp.concatenate([bias] * HEADS_PER_GROUP, axis=0)
    e = jnp.exp(s - jnp.max(s, axis=-1, keepdims=True))
    o_w = jnp.dot(e.astype(BF16), vw_ref[0, 0, pl.ds(ws, span), :], preferred_element_type=F32)
    o_w = o_w / jnp.sum(e, axis=-1, keepdims=True)

    gates = g_ref[...]
    for qh in range(HEADS_PER_GROUP):
        sl = slice(qh * tq, (qh + 1) * tq)
        o_h = (gates[:, 3 * qh:3 * qh + 1] * o_c[sl] + gates[:, 3 * qh + 1:3 * qh + 2] * o_s[sl]
               + gates[:, 3 * qh + 2:3 * qh + 3] * o_w[sl])
        o_scr[g, :, qh * HEAD_DIM:(qh + 1) * HEAD_DIM] = o_h

    @pl.when(g == N_KV_HEADS - 1)
    def _():
        o = jnp.concatenate([o_scr[gg] for gg in range(N_KV_HEADS)], axis=1).astype(BF16)
        y = jnp.dot(o, wo_ref[...], preferred_element_type=F32)
        h = _ln(ALPHA * h_ref[...] + y, lng_ref[...], lnb_ref[...])
        o_ref[...] = h
        lg_ref[...] = _router(h, wr_ref, br_ref)


def _nsa_prompt(h, qh, kck, kcv, ks, vs, kw, vw, gates, wo, lng, lnb, wr, br, *, tq):
    n, d = h.shape
    bsz, _, seq, _ = ks.shape
    nt = seq // tq
    n_cmp = kck.shape[2]
    ovt = jnp.asarray(_overlap_t(LANES, n_cmp))
    kc = 4 * tq
    kblk = (np.arange(seq) // SEL_BLK).reshape(seq // kc, 1, kc)
    expand = jnp.asarray(np.arange(LANES)[None, :, None] == kblk, BF16)
    grp = lambda rows_: pl.BlockSpec((1, 1, rows_, HEAD_DIM), lambda b, i, g: (b, g, 0, 0))
    row = lambda w: pl.BlockSpec((tq, w), lambda b, i, g: (b * nt + i, 0))
    return pl.pallas_call(
        functools.partial(_attn_p_body, tq=tq, seq=seq),
        grid=(bsz, nt, N_KV_HEADS),
        in_specs=[pl.BlockSpec((1, HEADS_PER_GROUP, tq, HEAD_DIM), lambda b, i, g: (b, g, i, 0)),
                  grp(n_cmp), grp(n_cmp), grp(seq), grp(seq), grp(seq), grp(seq),
                  pl.BlockSpec((tq, LANES), lambda b, i, g: (b * nt + i, g)), row(d),
                  _full(ovt.shape), _full(expand.shape), _full((d, d)), _full((1, d)), _full((1, d)),
                  _full((d, N_EXPERTS)), _full((1, N_EXPERTS))],
        out_specs=[row(d), row(N_EXPERTS)],
        out_shape=[jax.ShapeDtypeStruct((n, d), F32), jax.ShapeDtypeStruct((n, N_EXPERTS), F32)],
        scratch_shapes=[pltpu.VMEM((N_KV_HEADS, tq, HEADS_PER_GROUP * HEAD_DIM), F32)],
        compiler_params=_params(("arbitrary", "arbitrary", "arbitrary")),
        name="nsa_prompt",
    )(qh, kck, kcv, ks, vs, kw, vw, gates, h, ovt, expand, wo.astype(BF16), lng, lnb, wr, br)


def _attn_s_body(q_ref, gt_ref, kvc_ref, ps_ref, ns_ref, cw_ref, nw_ref, ovt_ref, exp_ref,
                 o_ref, sc_ref, *, dec, past):
    nrow = q_ref.shape[1]
    half = KV_COLS // 2
    q = q_ref[0]
    row = lax.broadcasted_iota(jnp.int32, (nrow, 1), 0)
    tok = row & (dec - 1)
    gsel = lax.shift_right_logical(row, 2) & (N_KV_HEADS - 1)
    qpos = past + tok

    def pick(o):
        out = jnp.zeros((nrow, HEAD_DIM), F32)
        for gg in range(N_KV_HEADS):
            out = out + jnp.where(gsel == gg, o[:, gg * HEAD_DIM:(gg + 1) * HEAD_DIM], 0.0)
        return out

    n_cmp = kvc_ref.shape[1]
    kc = kvc_ref[0, :, 0:half].astype(BF16)
    vc = kvc_ref[0, :, half:KV_COLS].astype(BF16)
    cend = lax.broadcasted_iota(jnp.int32, (1, n_cmp), 1) * CMP_STRIDE + (CMP_BLK - 1)
    p_c = _softmax_probs(_nt(q, kc), cend <= qpos)
    o_c = pick(jnp.dot(p_c.astype(BF16), vc, preferred_element_type=F32))
    ngt = N_KV_HEADS * dec
    psum = p_c[0:ngt]
    for qh in range(1, HEADS_PER_GROUP):
        psum = psum + p_c[qh * ngt:(qh + 1) * ngt]
    psum = jnp.concatenate([psum, jnp.zeros((LANES - ngt, n_cmp), F32)], axis=0)
    imp_t = lax.dot_general(ovt_ref[...], psum, (((1,), (1,)), ((), ())),
                            precision=HIGHEST, preferred_element_type=F32)

    n_sel = past // SEL_BLK + 1
    n_sel8 = (n_sel + 7) // 8 * 8
    lane_tok = lax.broadcasted_iota(jnp.int32, (1, LANES), 1) & (dec - 1)
    sc, j, valid = _select_blocks(imp_t, past + lane_tok, n_sel8)
    sc_ref[...] = sc

    def rank_step(jp, rank):
        r = sc_ref[pl.ds(jp, 1), :]
        beats = (r > sc) | ((r == sc) & (j > jp))
        return rank + beats.astype(jnp.int32)

    rank = lax.fori_loop(0, n_sel, rank_step, jnp.zeros((n_sel8, LANES), jnp.int32))
    sel_t = (valid & (rank < SEL_TOPN)).astype(F32)
    n_blk_pad = ovt_ref.shape[0]
    sel_t = jnp.concatenate([sel_t, jnp.zeros((n_blk_pad - n_sel8, LANES), F32)], axis=0)
    sel = sel_t.T[0:ngt, 0:past // SEL_BLK].astype(BF16)
    sel = jnp.concatenate([sel] * HEADS_PER_GROUP, axis=0)
    mk = jnp.dot(sel, exp_ref[...], preferred_element_type=F32) > 0.5

    def two_part(k_old, v_old, m_old, new_ref):
        k_new = new_ref[0, :, 0:half].astype(BF16)
        v_new = new_ref[0, :, half:KV_COLS].astype(BF16)
        s_o = jnp.where(m_old, _nt(q, k_old), NEG)
        m_new = lax.broadcasted_iota(jnp.int32, (1, new_ref.shape[1]), 1) <= tok
        s_n = jnp.where(m_new, _nt(q, k_new), NEG)
        mx = jnp.maximum(jnp.max(s_o, axis=-1, keepdims=True), jnp.max(s_n, axis=-1, keepdims=True))
        e_o = jnp.where(m_old, jnp.exp(s_o - mx), 0.0)
        e_n = jnp.where(m_new, jnp.exp(s_n - mx), 0.0)
        den = jnp.sum(e_o, axis=-1, keepdims=True) + jnp.sum(e_n, axis=-1, keepdims=True)
        o = (jnp.dot(e_o.astype(BF16), v_old, preferred_element_type=F32)
             + jnp.dot(e_n.astype(BF16), v_new, preferred_element_type=F32))
        return pick(o / den)

    o_s = two_part(ps_ref[0, :, 0:half], ps_ref[0, :, half:KV_COLS], mk, ns_ref)
    n_win = cw_ref.shape[1]
    m_w = lax.broadcasted_iota(jnp.int32, (1, n_win), 1) > tok + (n_win - WINDOW)
    o_w = two_part(cw_ref[0, :, 0:half].astype(BF16), cw_ref[0, :, half:KV_COLS].astype(BF16), m_w, nw_ref)
    gt = gt_ref[0]
    o_ref[0] = gt[:, 0:1] * o_c + gt[:, 1:2] * o_s + gt[:, 2:3] * o_w


def _nsa_sample(qbd, gt, kvc, past_s, new_s, cache_w, new_w, *, dec, past):
    bsz, nrow, _ = qbd.shape
    n_cmp = kvc.shape[1]
    n_blk_pad = 2 * LANES
    ovt = jnp.asarray(_overlap_t(n_blk_pad, n_cmp))
    expand = jnp.asarray((np.arange(past // SEL_BLK)[:, None] == (np.arange(past)[None, :] // SEL_BLK)), BF16)
    n_sel8 = (past // SEL_BLK + 1 + 7) // 8 * 8
    b3 = lambda a: pl.BlockSpec((1,) + a.shape[1:], lambda b: (b, 0, 0))
    return pl.pallas_call(
        functools.partial(_attn_s_body, dec=dec, past=past), grid=(bsz,),
        in_specs=[b3(qbd), b3(gt), b3(kvc), b3(past_s), b3(new_s), b3(cache_w), b3(new_w),
                  _full(ovt.shape), _full(expand.shape)],
        out_specs=pl.BlockSpec((1, nrow, HEAD_DIM), lambda b: (b, 0, 0)),
        out_shape=jax.ShapeDtypeStruct((bsz, nrow, HEAD_DIM), F32),
        scratch_shapes=[pltpu.VMEM((n_sel8, LANES), F32)],
        compiler_params=_params(("arbitrary",)),
        name="nsa_sample",
    )(qbd, gt, kvc, past_s, new_s, cache_w, new_w, ovt, expand)


def _out_body(o_ref, h_ref, wo_ref, lng_ref, lnb_ref, wr_ref, br_ref, y_ref, lg_ref):
    y = jnp.dot(o_ref[...].astype(BF16), wo_ref[...], preferred_element_type=F32)
    h = _ln(ALPHA * h_ref[...] + y, lng_ref[...], lnb_ref[...])
    y_ref[...] = h
    lg_ref[...] = _router(h, wr_ref, br_ref)


def _out_proj(o, h, wo, lng, lnb, wr, br):
    n, d = h.shape
    return pl.pallas_call(
        _out_body, grid=(1,),
        in_specs=[_full((n, d)), _full((n, d)), _full((d, d)), _full((1, d)), _full((1, d)),
                  _full((d, N_EXPERTS)), _full((1, N_EXPERTS))],
        out_specs=[_full((n, d)), _full((n, N_EXPERTS))],
        out_shape=[jax.ShapeDtypeStruct((n, d), F32), jax.ShapeDtypeStruct((n, N_EXPERTS), F32)],
        compiler_params=_params(("arbitrary",)),
        name="nsa_out_proj",
    )(o, h, wo.astype(BF16), lng, lnb, wr, br)


def kernel(x_prompt, x_sample, state_conv, cache_kv_cmp, cache_kv_sel, cache_kv_win, page_table,
           conv_w_in, conv_w, conv_w_out, kv_w, cmp_pe, cmp_w1, cmp_b1, cmp_w2,
           nsa_wq, nsa_wg, nsa_wo, moe_wr, moe_br, moe_w1, moe_b1, moe_w2, moe_b2, ln_g, ln_b):
    bp, sp, d = x_prompt.shape
    bd, sd, _ = x_sample.shape
    kv_shape = (2, N_KV_HEADS, HEAD_DIM)
    lng = lambda l, s: ln_g[l, s][None, :]
    lnb = lambda l, s: ln_b[l, s][None, :]
    moe = lambda l, h, lg, **kw: _moe_layer(h, lg, moe_w1, moe_b1, moe_w2, moe_b2,
                                            lng(l, 1), lnb(l, 1), layer=l, **kw)
    w_cat = _proj_weights(kv_w, nsa_wq[0], nsa_wg[0])
    cw = _cmp_weights(cmp_pe, cmp_w1, cmp_b1, cmp_w2)
    br = lambda l: moe_br[l][None, :]

    n_p = bp * sp
    xp = x_prompt.reshape(n_p, d)
    h, conv_p, lg = _conv_layer(xp, jnp.zeros((bp, 2, d), F32), conv_w_in[0], conv_w[0], conv_w_out[0],
                                lng(0, 0), lnb(0, 0), moe_wr[0], br(0), seq=sp, tm=256, rows_prev=False)
    h = moe(0, h, lg, tm_rows=256, tm_tok=256)
    tabs = _rope_tables(jnp.arange(sp, dtype=jnp.int32))
    rc, rs, rw, gates, qh, ks, vs, kw, vw = _nsa_proj(h, w_cat, tabs, tm=256, seq=sp, heads_out=True)
    kvc = _compress(rc.reshape(bp, sp, KV_COLS), cw)
    kvc_h = kvc.reshape(bp, -1, 2, N_KV_HEADS, HEAD_DIM).transpose(2, 0, 3, 1, 4).astype(BF16)
    h, lg = _nsa_prompt(h, qh, kvc_h[0], kvc_h[1], ks, vs, kw, vw, gates, nsa_wo[0],
                        lng(1, 0), lnb(1, 0), moe_wr[1], br(1), tq=128)
    y_prompt = moe(1, h, lg, tm_rows=256, tm_tok=256).reshape(bp, sp, d)
    kv_cmp_p = rc.reshape((bp, sp) + kv_shape)
    kv_sel_p = rs.reshape((bp, sp) + kv_shape)
    kv_win_p = rw.reshape((bp, sp) + kv_shape)[:, -min(WINDOW, sp):]

    n_s = bd * sd
    xs = x_sample.reshape(n_s, d)
    prev_rows = jnp.repeat(state_conv[0].transpose(1, 0, 2), sd, axis=1)
    h, u_s, lg = _conv_layer(xs, prev_rows, conv_w_in[0], conv_w[0], conv_w_out[0],
                             lng(0, 0), lnb(0, 0), moe_wr[0], br(0), seq=sd, tm=n_s, rows_prev=True)
    conv_s = u_s.reshape(bd, sd, d)[:, sd - 2:]
    h = moe(0, h, lg, tm_rows=64, tm_tok=n_s)
    pos_s = PAST_LEN + (jnp.arange(n_s, dtype=jnp.int32) % sd)
    rc_s, rs_s, rw_s, gates_s, q_s = _nsa_proj(h, w_cat, _rope_tables(pos_s), tm=n_s, seq=sd, heads_out=False)

    n_pages = PAST_LEN // PAGE_SIZE
    gather = lambda pool: pool.reshape(pool.shape[0], PAGE_SIZE, KV_COLS)[page_table].reshape(
        bd, n_pages * PAGE_SIZE, KV_COLS)
    kvc_s = _compress(gather(cache_kv_cmp), cw)
    past_s = lax.optimization_barrier(gather(cache_kv_sel)).astype(BF16)
    q5 = q_s.reshape(bd, sd, N_KV_HEADS, HEADS_PER_GROUP, HEAD_DIM).transpose(0, 3, 2, 1, 4)
    eye = jnp.eye(N_KV_HEADS, dtype=F32)
    qbd = jnp.einsum('bqgtd,gh->bqgthd', q5, eye).reshape(bd, N_HEADS * sd, N_KV_HEADS * HEAD_DIM).astype(BF16)
    g5 = gates_s.reshape(bd, sd, N_KV_HEADS, LANES)[..., :HEADS_PER_GROUP * 3]
    g5 = g5.reshape(bd, sd, N_KV_HEADS, HEADS_PER_GROUP, 3).transpose(0, 3, 2, 1, 4).reshape(bd, N_HEADS * sd, 3)
    gt = jnp.pad(g5, ((0, 0), (0, 0), (0, 5)))
    pad_new = lambda r: jnp.pad(r.reshape(bd, sd, KV_COLS), ((0, 0), (0, 16 - sd), (0, 0)))
    w_buf = cache_kv_win.shape[1]
    cache_w = cache_kv_win.reshape(bd, w_buf, KV_COLS)
    o_s = _nsa_sample(qbd, gt, kvc_s, past_s, pad_new(rs_s), cache_w, pad_new(rw_s), dec=sd, past=PAST_LEN)
    o_s = o_s.reshape(bd, HEADS_PER_GROUP, N_KV_HEADS, sd, HEAD_DIM).transpose(0, 3, 2, 1, 4).reshape(n_s, d)
    h, lg = _out_proj(o_s, h, nsa_wo[0], lng(1, 0), lnb(1, 0), moe_wr[1], br(1))
    y_sample = moe(1, h, lg, tm_rows=64, tm_tok=n_s).reshape(bd, sd, d)
    kv_cmp_s = rc_s.reshape((bd, sd) + kv_shape)
    kv_sel_s = rs_s.reshape((bd, sd) + kv_shape)
    kv_win_s = jnp.concatenate([cache_kv_win, rw_s.reshape((bd, sd) + kv_shape)], axis=1)[:, -w_buf:]

    return (y_prompt, y_sample, conv_p[None], kv_cmp_p, kv_sel_p, kv_win_p,
            conv_s[None], kv_cmp_s, kv_sel_s, kv_win_s)
```

```python
import functools

import numpy as np
import jax
import jax.numpy as jnp
from jax import lax
from jax.experimental import pallas as pl
from jax.experimental.pallas import tpu as pltpu

F32 = jnp.float32
BF16 = jnp.bfloat16
HIGHEST = lax.Precision.HIGHEST

D_MODEL = 1024
DEPTH = 2
PAST_LEN = 8192
PAGE_SIZE = 128
N_HEADS = 16
N_KV_HEADS = 4
HEADS_PER_GROUP = N_HEADS // N_KV_HEADS
HEAD_DIM = D_MODEL // N_HEADS
ROT_DIM = HEAD_DIM // 4
ROPE_THETA = 500000.0
CMP_BLK = 32
CMP_STRIDE = 16
CMP_HID = 2 * HEAD_DIM
SEL_BLK = 64
SEL_TOPN = 16
WINDOW = 512
N_EXPERTS = 32
TOP_K = 4
D_FF = D_MODEL
SWIGLU_LIMIT = 7.0
SWIGLU_ALPHA = 1.702
ALPHA = (2 * DEPTH) ** 0.25
LN_EPS = 1e-5
NEG = -1e30
FORCE = 1e4

KV_COLS = 2 * N_KV_HEADS * HEAD_DIM
LANES = 128
VMEM_LIMIT = 56 * 2 ** 20


def _params(sem, vmem=VMEM_LIMIT):
    return pltpu.CompilerParams(dimension_semantics=sem, vmem_limit_bytes=vmem)


def _ln(x, g, b):
    mu = jnp.mean(x, axis=-1, keepdims=True)
    xc = x - mu
    var = jnp.mean(xc * xc, axis=-1, keepdims=True)
    return xc * lax.rsqrt(var + LN_EPS) * g + b


def _nt(a, b):
    return lax.dot_general(a, b, (((1,), (1,)), ((), ())), preferred_element_type=F32)


def _router(h, wr_ref, br_ref):
    return jnp.dot(h, wr_ref[...], precision=HIGHEST, preferred_element_type=F32) + br_ref[...]


def _full(shape):
    return pl.BlockSpec(shape, lambda *_: (0,) * len(shape))


def _conv_body(x_ref, p_ref, win_ref, wc_ref, wout_ref, lng_ref, lnb_ref, wr_ref, br_ref,
               h_ref, st_ref, lg_ref, carry_ref, *, seq, tm, rows_prev):
    i = pl.program_id(0)

    @pl.when(i == 0)
    def _():
        carry_ref[...] = jnp.zeros_like(carry_ref)

    x = x_ref[...]
    d = x.shape[1]
    z = jnp.dot(x.astype(BF16), win_ref[...], preferred_element_type=F32)
    bg, c, xh = z[:, :d], z[:, d:2 * d], z[:, 2 * d:]
    u = c * xh
    row = lax.broadcasted_iota(jnp.int32, (tm, 1), 0)
    t = (i * tm + row) & (seq - 1)
    um1 = pltpu.roll(u, 1, 0)
    um2 = pltpu.roll(u, 2, 0)
    c0 = carry_ref[0:1, :]
    c1 = carry_ref[1:2, :]
    um1 = jnp.where(row == 0, c1, um1)
    um2 = jnp.where(row == 0, c0, jnp.where(row == 1, c1, um2))
    if rows_prev:
        p0, p1 = p_ref[0], p_ref[1]
    else:
        p0, p1 = p_ref[0, 0:1, :], p_ref[0, 1:2, :]
    um1 = jnp.where(t >= 1, um1, p1)
    um2 = jnp.where(t >= 2, um2, jnp.where(t == 1, p1, p0))
    conv = wc_ref[0:1, :] * um2 + wc_ref[1:2, :] * um1 + wc_ref[2:3, :] * u
    y = jnp.dot((bg * conv).astype(BF16), wout_ref[...], preferred_element_type=F32)
    h = _ln(ALPHA * x + y, lng_ref[...], lnb_ref[...])
    h_ref[...] = h
    lg_ref[...] = _router(h, wr_ref, br_ref)
    carry_ref[0:2, :] = u[tm - 2:tm, :]
    if rows_prev:
        st_ref[...] = u
    else:
        st_ref[0] = u[tm - 2:tm, :]


def _conv_layer(x, prev, w_in, w_conv, w_out, lng, lnb, wr, br, *, seq, tm, rows_prev):
    n, d = x.shape
    assert n % tm == 0 and seq >= 2 and seq & (seq - 1) == 0
    assert (seq % tm == 0) if not rows_prev else (tm % seq == 0 and n == tm)
    if rows_prev:
        p_spec = pl.BlockSpec((2, tm, d), lambda i: (0, i, 0))
        st_shape = jax.ShapeDtypeStruct((n, d), F32)
        st_spec = pl.BlockSpec((tm, d), lambda i: (i, 0))
    else:
        per = seq // tm
        p_spec = pl.BlockSpec((1, 2, d), lambda i: (i // per, 0, 0))
        st_shape = jax.ShapeDtypeStruct((n // seq, 2, d), F32)
        st_spec = pl.BlockSpec((1, 2, d), lambda i: (i // per, 0, 0))
    return pl.pallas_call(
        functools.partial(_conv_body, seq=seq, tm=tm, rows_prev=rows_prev),
        grid=(n // tm,),
        in_specs=[pl.BlockSpec((tm, d), lambda i: (i, 0)), p_spec,
                  _full((d, 3 * d)), _full((3, d)), _full((d, d)), _full((1, d)), _full((1, d)),
                  _full((d, N_EXPERTS)), _full((1, N_EXPERTS))],
        out_specs=[pl.BlockSpec((tm, d), lambda i: (i, 0)), st_spec,
                   pl.BlockSpec((tm, N_EXPERTS), lambda i: (i, 0))],
        out_shape=[jax.ShapeDtypeStruct((n, d), F32), st_shape,
                   jax.ShapeDtypeStruct((n, N_EXPERTS), F32)],
        scratch_shapes=[pltpu.VMEM((8, d), F32)],
        compiler_params=_params(("arbitrary",)),
        name="conv_mixer",
    )(x, prev, w_in.astype(BF16), w_conv, w_out.astype(BF16), lng, lnb, wr, br)


def _moe_body(te_ref, nv_ref, x_ref, w1_ref, b1_ref, w2_ref, b2_ref, o_ref, w1b, w2b):
    i = pl.program_id(0)
    e = te_ref[i]
    prev = te_ref[jnp.maximum(i - 1, 0)]
    valid = i < nv_ref[0]

    @pl.when(valid & ((i == 0) | (e != prev)))
    def _():
        w1b[...] = w1_ref[0, 0].astype(BF16)
        w2b[...] = w2_ref[0, 0].astype(BF16)

    @pl.when(valid)
    def _():
        hgu = jnp.dot(x_ref[...].astype(BF16), w1b[...], preferred_element_type=F32) + b1_ref[0, 0]
        g = jnp.minimum(hgu[:, :D_FF], SWIGLU_LIMIT)
        u = jnp.clip(hgu[:, D_FF:], -SWIGLU_LIMIT, SWIGLU_LIMIT)
        a = (u + 1.0) * (g * jax.nn.sigmoid(SWIGLU_ALPHA * g))
        o_ref[...] = jnp.dot(a.astype(BF16), w2b[...], preferred_element_type=F32) + b2_ref[0, 0]

    @pl.when(jnp.logical_not(valid))
    def _():
        o_ref[...] = jnp.zeros_like(o_ref)


def _moe_experts(xb, tile_e, n_valid, w1, b1, w2, b2, *, layer, tm):
    n_rows, d = xb.shape
    n_tiles = n_rows // tm
    grid_spec = pltpu.PrefetchScalarGridSpec(
        num_scalar_prefetch=2,
        grid=(n_tiles,),
        in_specs=[pl.BlockSpec((tm, d), lambda i, te, nv: (i, 0)),
                  pl.BlockSpec((1, 1, d, 2 * D_FF), lambda i, te, nv: (layer, te[i], 0, 0)),
                  pl.BlockSpec((1, 1, 1, 2 * D_FF), lambda i, te, nv: (layer, te[i], 0, 0)),
                  pl.BlockSpec((1, 1, D_FF, d), lambda i, te, nv: (layer, te[i], 0, 0)),
                  pl.BlockSpec((1, 1, 1, d), lambda i, te, nv: (layer, te[i], 0, 0))],
        out_specs=pl.BlockSpec((tm, d), lambda i, te, nv: (i, 0)),
        scratch_shapes=[pltpu.VMEM((d, 2 * D_FF), BF16), pltpu.VMEM((D_FF, d), BF16)],
    )
    return pl.pallas_call(
        _moe_body, grid_spec=grid_spec,
        out_shape=jax.ShapeDtypeStruct((n_rows, d), F32),
        compiler_params=_params(("arbitrary",)),
        name="moe_experts",
    )(tile_e, n_valid, xb, w1, b1[:, :, None, :], w2, b2[:, :, None, :])


def _combine_body(h_ref, yg_ref, gate_ref, lng_ref, lnb_ref, o_ref):
    gate = gate_ref[...]
    f = gate[:, 0:1] * yg_ref[0]
    for k in range(1, TOP_K):
        f = f + gate[:, k:k + 1] * yg_ref[k]
    o_ref[...] = _ln(ALPHA * h_ref[...] + f, lng_ref[...], lnb_ref[...])


def _moe_combine(h, yg, gate, lng, lnb, *, tm):
    n, d = h.shape
    return pl.pallas_call(
        _combine_body, grid=(n // tm,),
        in_specs=[pl.BlockSpec((tm, d), lambda i: (i, 0)),
                  pl.BlockSpec((TOP_K, tm, d), lambda i: (0, i, 0)),
                  pl.BlockSpec((tm, TOP_K), lambda i: (i, 0)),
                  _full((1, d)), _full((1, d))],
        out_specs=pl.BlockSpec((tm, d), lambda i: (i, 0)),
        out_shape=jax.ShapeDtypeStruct((n, d), F32),
        compiler_params=_params(("arbitrary",)),
        name="moe_combine",
    )(h, yg, gate, lng, lnb)


def _moe_layer(h, logits, w1, b1, w2, b2, lng, lnb, *, layer, tm_rows, tm_tok):
    n, d = h.shape
    top_v, top_e = lax.top_k(logits, TOP_K)
    gate = jax.nn.softmax(top_v, axis=-1)
    flat_e = top_e.reshape(-1)
    nk = n * TOP_K
    onehot = (flat_e[:, None] == jnp.arange(N_EXPERTS, dtype=jnp.int32)[None, :]).astype(jnp.int32)
    csum = jnp.cumsum(onehot, axis=0)
    rank = jnp.sum((csum - onehot) * onehot, axis=-1)
    counts = csum[-1]
    padded = (counts + tm_rows - 1) // tm_rows * tm_rows
    pad_end = jnp.cumsum(padded)
    pad_start = pad_end - padded
    dest = (pad_start[flat_e] + rank).astype(jnp.int32)
    n_tiles = (nk + N_EXPERTS * (tm_rows - 1) + tm_rows - 1) // tm_rows
    n_rows = n_tiles * tm_rows
    tile_start = jnp.arange(n_tiles, dtype=jnp.int32) * tm_rows
    tile_e = jnp.minimum(jnp.sum((pad_end[None, :] <= tile_start[:, None]).astype(jnp.int32), axis=1), N_EXPERTS - 1)
    n_valid = (pad_end[-1] // tm_rows).astype(jnp.int32).reshape(1)
    row_tok = jnp.zeros((n_rows,), jnp.int32).at[dest].set(jnp.arange(nk, dtype=jnp.int32) // TOP_K)
    xb = h[row_tok]
    yb = _moe_experts(xb, tile_e, n_valid, w1, b1, w2, b2, layer=layer, tm=tm_rows)
    yg = yb[dest.reshape(n, TOP_K).T]
    return _moe_combine(h, yg, gate, lng, lnb, tm=tm_tok)


def _proj_body(h_ref, w_ref, rc_ref, rs1_ref, rs2_ref, *out_refs, heads_out, seq, q_scale):
    z = jnp.dot(h_ref[...].astype(BF16), w_ref[...], preferred_element_type=F32)
    cc, s1, s2 = rc_ref[...], rs1_ref[...], rs2_ref[...]

    def rope(x):
        return x * cc + pltpu.roll(x, LANES - ROT_DIM // 2, 1) * s1 + pltpu.roll(x, ROT_DIM // 2, 1) * s2

    def rope_cols(lo, hi):
        return [rope(z[:, c:c + LANES]) for c in range(lo, hi, LANES)]

    if heads_out:
        rows_c, rows_s, rows_w, g_ref, q_ref, ks_ref, vs_ref, kw_ref, vw_ref = out_refs
    else:
        rows_c, rows_s, rows_w, g_ref, q_ref = out_refs
    half = KV_COLS // 2
    for br, rows in enumerate((rows_c, rows_s, rows_w)):
        base = br * KV_COLS
        k = jnp.concatenate(rope_cols(base, base + half), axis=1)
        v = z[:, base + half:base + KV_COLS]
        rows[...] = jnp.concatenate([k, v], axis=1)
        if heads_out and br >= 1:
            tm = k.shape[0]
            t = (pl.program_id(0) % (seq // tm)) * tm + lax.broadcasted_iota(jnp.int32, (tm, HEAD_DIM), 0)
            lane = lax.broadcasted_iota(jnp.int32, (tm, HEAD_DIM), 1)
            blk_onehot = (lane == lax.shift_right_logical(t, 6)).astype(BF16)
            one_lane = (lane == 0).astype(BF16)
            k_ref, v_ref = (ks_ref, vs_ref) if br == 1 else (kw_ref, vw_ref)
            for g in range(N_KV_HEADS):
                kg = k[:, g * HEAD_DIM:(g + 1) * HEAD_DIM].astype(BF16)
                vg = v[:, g * HEAD_DIM:(g + 1) * HEAD_DIM].astype(BF16)
                k_ref[0, g] = jnp.concatenate([kg, blk_onehot], axis=1) if br == 1 else kg
                v_ref[0, g] = jnp.concatenate([vg, one_lane], axis=1)
    qb = 3 * KV_COLS
    q = jnp.concatenate(rope_cols(qb, qb + D_MODEL), axis=1) * q_scale
    if heads_out:
        for hd in range(N_HEADS):
            q_ref[0, hd] = q[:, hd * HEAD_DIM:(hd + 1) * HEAD_DIM].astype(BF16)
    else:
        q_ref[...] = q
    g_ref[...] = jax.nn.sigmoid(z[:, qb + D_MODEL:])


def _rope_tables(pos):
    half = ROT_DIM // 2
    inv = ROPE_THETA ** (-jnp.arange(half, dtype=F32) * 2.0 / ROT_DIM)
    ang = pos.astype(F32)[:, None] * inv[None, :]
    cos, sin = jnp.cos(ang), jnp.sin(ang)
    zeros = jnp.zeros((pos.shape[0], HEAD_DIM - ROT_DIM), F32)
    ones = jnp.ones_like(zeros)
    z8 = jnp.zeros_like(sin)
    c = jnp.concatenate([cos, cos, ones], axis=1)
    s1 = jnp.concatenate([-sin, z8, zeros], axis=1)
    s2 = jnp.concatenate([z8, sin, zeros], axis=1)
    rep = LANES // HEAD_DIM
    return tuple(jnp.tile(a, (1, rep)) for a in (c, s1, s2))


def _proj_weights(kv_w, wq, wg):
    wg4 = wg.reshape(D_MODEL, N_KV_HEADS, HEADS_PER_GROUP * 3)
    wg4 = jnp.pad(wg4, ((0, 0), (0, 0), (0, LANES - HEADS_PER_GROUP * 3))).reshape(D_MODEL, N_KV_HEADS * LANES)
    return jnp.concatenate([kv_w, wq, wg4], axis=1).astype(BF16)


def _nsa_proj(h, w_cat, tables, *, tm, seq, heads_out):
    n, d = h.shape
    ncol = w_cat.shape[1]
    per = tables[0].shape[0] // tm
    tab_spec = pl.BlockSpec((tm, LANES), lambda i: (i % per, 0))
    row_spec = lambda w: pl.BlockSpec((tm, w), lambda i: (i, 0))
    out_specs = [row_spec(KV_COLS)] * 3 + [row_spec(N_KV_HEADS * LANES)]
    out_shape = [jax.ShapeDtypeStruct((n, KV_COLS), F32)] * 3 + [jax.ShapeDtypeStruct((n, N_KV_HEADS * LANES), F32)]
    if heads_out:
        bsz = n // seq
        tps = seq // tm
        assert seq // SEL_BLK <= HEAD_DIM
        hspec = lambda nh, w: pl.BlockSpec((1, nh, tm, w), lambda i: (i // tps, 0, i % tps, 0))
        hshape = lambda nh, w: jax.ShapeDtypeStruct((bsz, nh, seq, w), BF16)
        widths = (2 * HEAD_DIM, 2 * HEAD_DIM, HEAD_DIM, 2 * HEAD_DIM)
        out_specs += [hspec(N_HEADS, HEAD_DIM)] + [hspec(N_KV_HEADS, w) for w in widths]
        out_shape += [hshape(N_HEADS, HEAD_DIM)] + [hshape(N_KV_HEADS, w) for w in widths]
    else:
        out_specs += [row_spec(d)]
        out_shape += [jax.ShapeDtypeStruct((n, d), F32)]
    q_scale = HEAD_DIM ** -0.5 * (float(np.log2(np.e)) if heads_out else 1.0)
    return pl.pallas_call(
        functools.partial(_proj_body, heads_out=heads_out, seq=seq, q_scale=q_scale), grid=(n // tm,),
        in_specs=[row_spec(d), _full((d, ncol)), tab_spec, tab_spec, tab_spec],
        out_specs=out_specs, out_shape=out_shape,
        compiler_params=_params(("arbitrary",)),
        name="nsa_proj",
    )(h, w_cat, *tables)


def _cmp_body(r_ref, w1_ref, pe_ref, w2_ref, o_ref, *, nch):
    k = pl.program_id(1) // 2
    x = jnp.concatenate(
        [r_ref[0, pl.ds(c, nch, stride=CMP_STRIDE), :].astype(BF16) for c in range(CMP_STRIDE)],
        axis=1)
    acc = jnp.dot(x, w1_ref[0], preferred_element_type=F32)
    pe = pe_ref[pl.ds(k, 1), :]
    hid = []
    for gs in range(2):
        pa = acc[:, gs * 2 * CMP_HID:gs * 2 * CMP_HID + CMP_HID]
        pb = acc[:, gs * 2 * CMP_HID + CMP_HID:(gs + 1) * 2 * CMP_HID]
        pb_next = pltpu.roll(pb, nch - 1, 0)
        hid.append(jax.nn.gelu(pa + pb_next + pe))
    hid = jnp.concatenate(hid, axis=1).astype(BF16)
    o_ref[0] = jnp.dot(hid, w2_ref[0], preferred_element_type=F32)


def _cmp_weights(cmp_pe, cmp_w1, cmp_b1, cmp_w2):
    eye = jnp.eye(2, dtype=F32)
    w1r = cmp_w1.reshape(2, 2, CMP_STRIDE, HEAD_DIM, CMP_HID)
    w1bd = jnp.einsum('ab,kncdh->kcadbnh', eye, w1r).reshape(2, CMP_STRIDE * 2 * HEAD_DIM, 4 * CMP_HID)
    w2bd = jnp.einsum('ab,khd->kahbd', eye, cmp_w2).reshape(2, 2 * CMP_HID, 2 * HEAD_DIM)
    pe_term = jnp.einsum('kcd,kcdh->kh', cmp_pe, cmp_w1, precision=HIGHEST) + cmp_b1
    return w1bd.astype(BF16), pe_term, w2bd.astype(BF16)


def _compress(rows, cw):
    bsz, t, _ = rows.shape
    nch = t // CMP_STRIDE
    w1bd, pe_term, w2bd = cw
    return pl.pallas_call(
        functools.partial(_cmp_body, nch=nch), grid=(bsz, KV_COLS // LANES),
        in_specs=[pl.BlockSpec((1, t, LANES), lambda b, j: (b, 0, j)),
                  pl.BlockSpec((1,) + w1bd.shape[1:], lambda b, j: (j // 2, 0, 0)),
                  _full(pe_term.shape),
                  pl.BlockSpec((1,) + w2bd.shape[1:], lambda b, j: (j // 2, 0, 0))],
        out_specs=pl.BlockSpec((1, nch, LANES), lambda b, j: (b, 0, j)),
        out_shape=jax.ShapeDtypeStruct((bsz, nch, KV_COLS), F32),
        compiler_params=_params(("arbitrary", "arbitrary")),
        name="compress_kv",
    )(rows, w1bd, pe_term, w2bd)


def _softmax_probs(s, m, exp_fn=jnp.exp):
    s = jnp.where(m, s, NEG)
    mx = jnp.max(s, axis=-1, keepdims=True)
    e = jnp.where(m, exp_fn(s - mx), 0.0)
    den = jnp.sum(e, axis=-1, keepdims=True)
    return e / jnp.maximum(den, 1e-30)


def _select_blocks(imp_t, qpos_row, n_rows):
    lanes = imp_t.shape[1]
    j = lax.broadcasted_iota(jnp.int32, (n_rows, lanes), 0)
    cur = lax.shift_right_logical(qpos_row, 6)
    valid = j <= cur
    forced = (j == 0) | (j == cur) | (j == cur - 1)
    sc = jnp.where(valid, jnp.where(forced, FORCE, imp_t[:n_rows]), NEG)
    return sc, j, valid


def _overlap_t(n_sel_pad, n_cmp_pad):
    c0 = np.arange(n_cmp_pad)[None, :] * CMP_STRIDE
    s0 = np.arange(n_sel_pad)[:, None] * SEL_BLK
    ov = np.clip(np.minimum(c0 + CMP_BLK, s0 + SEL_BLK) - np.maximum(c0, s0), 0, None).astype(np.float32) / CMP_BLK
    return ov


def _attn_p_body(q_ref, kck_ref, kcv_ref, ks_ref, vs_ref, kw_ref, vw_ref, g_ref, h_ref,
                 ovt_ref, wo_ref, lng_ref, lnb_ref, wr_ref, br_ref,
                 o_ref, lg_ref, o_scr, *, tq, seq, kc):
    i = pl.program_id(1)
    g = pl.program_id(2)
    t0 = i * tq
    rows = HEADS_PER_GROUP * tq
    n_sel = seq // SEL_BLK
    q = q_ref[0].reshape(rows, HEAD_DIM)
    tpos = t0 + (lax.broadcasted_iota(jnp.int32, (rows, 1), 0) & (tq - 1))

    n_cmp = kck_ref.shape[2]
    s = _nt(q, kck_ref[0, 0])
    cend = lax.broadcasted_iota(jnp.int32, (1, n_cmp), 1) * CMP_STRIDE + (CMP_BLK - 1)
    p_c = _softmax_probs(s, cend <= tpos, jnp.exp2)
    o_c = jnp.dot(p_c.astype(BF16), kcv_ref[0, 0], preferred_element_type=F32)
    psum = p_c[0:tq]
    for qh in range(1, HEADS_PER_GROUP):
        psum = psum + p_c[qh * tq:(qh + 1) * tq]
    imp_t = lax.dot_general(ovt_ref[...], psum, (((1,), (1,)), ((), ())),
                            precision=HIGHEST, preferred_element_type=F32)

    qrow = t0 + lax.broadcasted_iota(jnp.int32, (1, tq), 1)
    sc, j, valid = _select_blocks(imp_t, qrow, n_sel)
    rank = jnp.zeros((n_sel, tq), jnp.int32)
    for jp in range(n_sel):
        r = sc[jp:jp + 1, :]
        beats = (r > sc) | ((r == sc) & (j > jp))
        rank = rank + beats.astype(jnp.int32)
    sel_t = (valid & (rank < SEL_TOPN)).astype(F32)
    sel_t = jnp.concatenate([sel_t, jnp.zeros((LANES - n_sel, tq), F32)], axis=0)
    sel = sel_t.T[:, 0:HEAD_DIM]
    blk_lane = lax.broadcasted_iota(jnp.int32, (1, HEAD_DIM), 1)
    r_i = lax.broadcasted_iota(jnp.int32, (tq, 1), 0)
    tq_pos = t0 + r_i
    t0a = pl.multiple_of(t0, tq)
    causal = jnp.where(lax.broadcasted_iota(jnp.int32, (1, tq), 1) <= r_i, 0.0, NEG)
    causal = jnp.concatenate([causal] * HEADS_PER_GROUP, axis=0)

    s = _nt(q, ks_ref[0, 0, pl.ds(t0a, tq), 0:HEAD_DIM]) + causal
    m0 = jnp.max(s, axis=-1, keepdims=True)
    acc0 = jnp.dot(jnp.exp2(s - m0).astype(BF16), vs_ref[0, 0, pl.ds(t0a, tq), :], preferred_element_type=F32)
    sweep = jnp.where((sel > 0.5) & (blk_lane < lax.shift_right_logical(t0, 6)), 0.0, NEG).astype(BF16)
    q_aug = jnp.concatenate([q, jnp.concatenate([sweep] * HEADS_PER_GROUP, axis=0)], axis=1)

    def sweep_chunk(c, carry):
        m_run, acc = carry
        k0 = pl.multiple_of(c * kc, kc)
        s = _nt(q_aug, ks_ref[0, 0, pl.ds(k0, kc), :])
        m_new = jnp.maximum(m_run, jnp.max(s, axis=-1, keepdims=True))
        pv = jnp.dot(jnp.exp2(s - m_new).astype(BF16), vs_ref[0, 0, pl.ds(k0, kc), :],
                     preferred_element_type=F32)
        return m_new, jnp.exp2(m_run - m_new) * acc + pv

    _, acc = lax.fori_loop(0, (t0 + kc - 1) // kc, sweep_chunk, (m0, acc0))
    o_s = acc[:, 0:HEAD_DIM] / acc[:, HEAD_DIM:HEAD_DIM + 1]

    ws = pl.multiple_of(jnp.maximum(t0 - WINDOW, 0), tq)
    wi = ws + lax.broadcasted_iota(jnp.int32, (1, WINDOW), 1)
    wbias = jnp.where((wi > tq_pos - WINDOW) & (wi < t0), 0.0, NEG)
    s_d = _nt(q, kw_ref[0, 0, pl.ds(t0a, tq), :]) + causal
    s_w = _nt(q, kw_ref[0, 0, pl.ds(ws, WINDOW), :]) + jnp.concatenate([wbias] * HEADS_PER_GROUP, axis=0)
    m = jnp.maximum(jnp.max(s_d, axis=-1, keepdims=True), jnp.max(s_w, axis=-1, keepdims=True))
    acc = (jnp.dot(jnp.exp2(s_d - m).astype(BF16), vw_ref[0, 0, pl.ds(t0a, tq), :], preferred_element_type=F32)
           + jnp.dot(jnp.exp2(s_w - m).astype(BF16), vw_ref[0, 0, pl.ds(ws, WINDOW), :],
                     preferred_element_type=F32))
    o_w = acc[:, 0:HEAD_DIM] / acc[:, HEAD_DIM:HEAD_DIM + 1]

    gates = g_ref[...]
    for qh in range(HEADS_PER_GROUP):
        sl = slice(qh * tq, (qh + 1) * tq)
        o_h = (gates[:, 3 * qh:3 * qh + 1] * o_c[sl] + gates[:, 3 * qh + 1:3 * qh + 2] * o_s[sl]
               + gates[:, 3 * qh + 2:3 * qh + 3] * o_w[sl])
        o_scr[g, :, qh * HEAD_DIM:(qh + 1) * HEAD_DIM] = o_h

    @pl.when(g == N_KV_HEADS - 1)
    def _():
        o = jnp.concatenate([o_scr[gg] for gg in range(N_KV_HEADS)], axis=1).astype(BF16)
        y = jnp.dot(o, wo_ref[...], preferred_element_type=F32)
        h = _ln(ALPHA * h_ref[...] + y, lng_ref[...], lnb_ref[...])
        o_ref[...] = h
        lg_ref[...] = _router(h, wr_ref, br_ref)


def _nsa_prompt(h, qh, kck, kcv, ks, vs, kw, vw, gates, wo, lng, lnb, wr, br, *, tq, kc):
    n, d = h.shape
    bsz, _, seq, _ = ks.shape
    nt = seq // tq
    n_cmp = kck.shape[2]
    ovt = jnp.asarray(_overlap_t(LANES, n_cmp))
    assert kc % tq == 0 and seq % kc == 0 and SEL_BLK % 64 == 0 and tq % SEL_BLK == 0
    grp = lambda a: pl.BlockSpec((1, 1) + a.shape[2:], lambda b, i, g: (b, g, 0, 0))
    row = lambda w: pl.BlockSpec((tq, w), lambda b, i, g: (b * nt + i, 0))
    return pl.pallas_call(
        functools.partial(_attn_p_body, tq=tq, seq=seq, kc=kc),
        grid=(bsz, nt, N_KV_HEADS),
        in_specs=[pl.BlockSpec((1, HEADS_PER_GROUP, tq, HEAD_DIM), lambda b, i, g: (b, g, i, 0)),
                  grp(kck), grp(kcv), grp(ks), grp(vs), grp(kw), grp(vw),
                  pl.BlockSpec((tq, LANES), lambda b, i, g: (b * nt + i, g)), row(d),
                  _full(ovt.shape), _full((d, d)), _full((1, d)), _full((1, d)),
                  _full((d, N_EXPERTS)), _full((1, N_EXPERTS))],
        out_specs=[row(d), row(N_EXPERTS)],
        out_shape=[jax.ShapeDtypeStruct((n, d), F32), jax.ShapeDtypeStruct((n, N_EXPERTS), F32)],
        scratch_shapes=[pltpu.VMEM((N_KV_HEADS, tq, HEADS_PER_GROUP * HEAD_DIM), F32)],
        compiler_params=_params(("arbitrary", "arbitrary", "arbitrary")),
        name="nsa_prompt",
    )(qh, kck, kcv, ks, vs, kw, vw, gates, h, ovt, wo.astype(BF16), lng, lnb, wr, br)


def _attn_s_body(q_ref, gt_ref, kvc_ref, ps_ref, ns_ref, cw_ref, nw_ref, ovt_ref, exp_ref,
                 o_ref, sc_ref, *, dec, past):
    nrow = q_ref.shape[1]
    half = KV_COLS // 2
    q = q_ref[0]
    row = lax.broadcasted_iota(jnp.int32, (nrow, 1), 0)
    tok = row & (dec - 1)
    gsel = lax.shift_right_logical(row, 2) & (N_KV_HEADS - 1)
    qpos = past + tok

    def pick(o):
        out = jnp.zeros((nrow, HEAD_DIM), F32)
        for gg in range(N_KV_HEADS):
            out = out + jnp.where(gsel == gg, o[:, gg * HEAD_DIM:(gg + 1) * HEAD_DIM], 0.0)
        return out

    n_cmp = kvc_ref.shape[1]
    kc = kvc_ref[0, :, 0:half].astype(BF16)
    vc = kvc_ref[0, :, half:KV_COLS].astype(BF16)
    cend = lax.broadcasted_iota(jnp.int32, (1, n_cmp), 1) * CMP_STRIDE + (CMP_BLK - 1)
    p_c = _softmax_probs(_nt(q, kc), cend <= qpos)
    o_c = pick(jnp.dot(p_c.astype(BF16), vc, preferred_element_type=F32))
    ngt = N_KV_HEADS * dec
    psum = p_c[0:ngt]
    for qh in range(1, HEADS_PER_GROUP):
        psum = psum + p_c[qh * ngt:(qh + 1) * ngt]
    psum = jnp.concatenate([psum, jnp.zeros((LANES - ngt, n_cmp), F32)], axis=0)
    imp_t = lax.dot_general(ovt_ref[...], psum, (((1,), (1,)), ((), ())),
                            precision=HIGHEST, preferred_element_type=F32)

    n_sel = past // SEL_BLK + 1
    n_sel8 = (n_sel + 7) // 8 * 8
    lane_tok = lax.broadcasted_iota(jnp.int32, (1, LANES), 1) & (dec - 1)
    sc, j, valid = _select_blocks(imp_t, past + lane_tok, n_sel8)
    sc_ref[...] = sc

    def rank_step(jp, rank):
        r = sc_ref[pl.ds(jp, 1), :]
        beats = (r > sc) | ((r == sc) & (j > jp))
        return rank + beats.astype(jnp.int32)

    rank = lax.fori_loop(0, n_sel, rank_step, jnp.zeros((n_sel8, LANES), jnp.int32))
    sel_t = (valid & (rank < SEL_TOPN)).astype(F32)
    n_blk_pad = ovt_ref.shape[0]
    sel_t = jnp.concatenate([sel_t, jnp.zeros((n_blk_pad - n_sel8, LANES), F32)], axis=0)
    sel = sel_t.T[0:ngt, 0:past // SEL_BLK].astype(BF16)
    sel = jnp.concatenate([sel] * HEADS_PER_GROUP, axis=0)
    mk = jnp.dot(sel, exp_ref[...], preferred_element_type=F32) > 0.5

    def two_part(k_old, v_old, m_old, new_ref):
        k_new = new_ref[0, :, 0:half].astype(BF16)
        v_new = new_ref[0, :, half:KV_COLS].astype(BF16)
        s_o = jnp.where(m_old, _nt(q, k_old), NEG)
        m_new = lax.broadcasted_iota(jnp.int32, (1, new_ref.shape[1]), 1) <= tok
        s_n = jnp.where(m_new, _nt(q, k_new), NEG)
        mx = jnp.maximum(jnp.max(s_o, axis=-1, keepdims=True), jnp.max(s_n, axis=-1, keepdims=True))
        e_o = jnp.where(m_old, jnp.exp(s_o - mx), 0.0)
        e_n = jnp.where(m_new, jnp.exp(s_n - mx), 0.0)
        den = jnp.sum(e_o, axis=-1, keepdims=True) + jnp.sum(e_n, axis=-1, keepdims=True)
        o = (jnp.dot(e_o.astype(BF16), v_old, preferred_element_type=F32)
             + jnp.dot(e_n.astype(BF16), v_new, preferred_element_type=F32))
        return pick(o / den)

    o_s = two_part(ps_ref[0, :, 0:half].astype(BF16), ps_ref[0, :, half:KV_COLS].astype(BF16), mk, ns_ref)
    n_win = cw_ref.shape[1]
    m_w = lax.broadcasted_iota(jnp.int32, (1, n_win), 1) > tok + (n_win - WINDOW)
    o_w = two_part(cw_ref[0, :, 0:half].astype(BF16), cw_ref[0, :, half:KV_COLS].astype(BF16), m_w, nw_ref)
    gt = gt_ref[0]
    o_ref[0] = gt[:, 0:1] * o_c + gt[:, 1:2] * o_s + gt[:, 2:3] * o_w


def _nsa_sample(qbd, gt, kvc, past_s, new_s, cache_w, new_w, *, dec, past):
    bsz, nrow, _ = qbd.shape
    n_cmp = kvc.shape[1]
    n_blk_pad = 2 * LANES
    ovt = jnp.asarray(_overlap_t(n_blk_pad, n_cmp))
    expand = jnp.asarray((np.arange(past // SEL_BLK)[:, None] == (np.arange(past)[None, :] // SEL_BLK)), BF16)
    n_sel8 = (past // SEL_BLK + 1 + 7) // 8 * 8
    b3 = lambda a: pl.BlockSpec((1,) + a.shape[1:], lambda b: (b, 0, 0))
    return pl.pallas_call(
        functools.partial(_attn_s_body, dec=dec, past=past), grid=(bsz,),
        in_specs=[b3(qbd), b3(gt), b3(kvc), b3(past_s), b3(new_s), b3(cache_w), b3(new_w),
                  _full(ovt.shape), _full(expand.shape)],
        out_specs=pl.BlockSpec((1, nrow, HEAD_DIM), lambda b: (b, 0, 0)),
        out_shape=jax.ShapeDtypeStruct((bsz, nrow, HEAD_DIM), F32),
        scratch_shapes=[pltpu.VMEM((n_sel8, LANES), F32)],
        compiler_params=_params(("arbitrary",)),
        name="nsa_sample",
    )(qbd, gt, kvc, past_s, new_s, cache_w, new_w, ovt, expand)


def _out_body(o_ref, h_ref, wo_ref, lng_ref, lnb_ref, wr_ref, br_ref, y_ref, lg_ref):
    y = jnp.dot(o_ref[...].astype(BF16), wo_ref[...], preferred_element_type=F32)
    h = _ln(ALPHA * h_ref[...] + y, lng_ref[...], lnb_ref[...])
    y_ref[...] = h
    lg_ref[...] = _router(h, wr_ref, br_ref)


def _out_proj(o, h, wo, lng, lnb, wr, br):
    n, d = h.shape
    return pl.pallas_call(
        _out_body, grid=(1,),
        in_specs=[_full((n, d)), _full((n, d)), _full((d, d)), _full((1, d)), _full((1, d)),
                  _full((d, N_EXPERTS)), _full((1, N_EXPERTS))],
        out_specs=[_full((n, d)), _full((n, N_EXPERTS))],
        out_shape=[jax.ShapeDtypeStruct((n, d), F32), jax.ShapeDtypeStruct((n, N_EXPERTS), F32)],
        compiler_params=_params(("arbitrary",)),
        name="nsa_out_proj",
    )(o, h, wo.astype(BF16), lng, lnb, wr, br)


def kernel(x_prompt, x_sample, state_conv, cache_kv_cmp, cache_kv_sel, cache_kv_win, page_table,
           conv_w_in, conv_w, conv_w_out, kv_w, cmp_pe, cmp_w1, cmp_b1, cmp_w2,
           nsa_wq, nsa_wg, nsa_wo, moe_wr, moe_br, moe_w1, moe_b1, moe_w2, moe_b2, ln_g, ln_b):
    bp, sp, d = x_prompt.shape
    bd, sd, _ = x_sample.shape
    kv_shape = (2, N_KV_HEADS, HEAD_DIM)
    lng = lambda l, s: ln_g[l, s][None, :]
    lnb = lambda l, s: ln_b[l, s][None, :]
    moe = lambda l, h, lg, **kw: _moe_layer(h, lg, moe_w1, moe_b1, moe_w2, moe_b2,
                                            lng(l, 1), lnb(l, 1), layer=l, **kw)
    w_cat = _proj_weights(kv_w, nsa_wq[0], nsa_wg[0])
    cw = _cmp_weights(cmp_pe, cmp_w1, cmp_b1, cmp_w2)
    br = lambda l: moe_br[l][None, :]

    n_p = bp * sp
    xp = x_prompt.reshape(n_p, d)
    h, conv_p, lg = _conv_layer(xp, jnp.zeros((bp, 2, d), F32), conv_w_in[0], conv_w[0], conv_w_out[0],
                                lng(0, 0), lnb(0, 0), moe_wr[0], br(0), seq=sp, tm=256, rows_prev=False)
    h = moe(0, h, lg, tm_rows=256, tm_tok=256)
    tabs = _rope_tables(jnp.arange(sp, dtype=jnp.int32))
    rc, rs, rw, gates, qh, ks, vs, kw, vw = _nsa_proj(h, w_cat, tabs, tm=256, seq=sp, heads_out=True)
    kvc = _compress(rc.reshape(bp, sp, KV_COLS), cw)
    kvc_h = kvc.reshape(bp, -1, 2, N_KV_HEADS, HEAD_DIM).transpose(2, 0, 3, 1, 4).astype(BF16)
    h, lg = _nsa_prompt(h, qh, kvc_h[0], kvc_h[1], ks, vs, kw, vw, gates, nsa_wo[0],
                        lng(1, 0), lnb(1, 0), moe_wr[1], br(1), tq=256, kc=512)
    y_prompt = moe(1, h, lg, tm_rows=256, tm_tok=256).reshape(bp, sp, d)
    kv_cmp_p = rc.reshape((bp, sp) + kv_shape)
    kv_sel_p = rs.reshape((bp, sp) + kv_shape)
    kv_win_p = rw.reshape((bp, sp) + kv_shape)[:, -min(WINDOW, sp):]

    n_s = bd * sd
    xs = x_sample.reshape(n_s, d)
    prev_rows = jnp.repeat(state_conv[0].transpose(1, 0, 2), sd, axis=1)
    h, u_s, lg = _conv_layer(xs, prev_rows, conv_w_in[0], conv_w[0], conv_w_out[0],
                             lng(0, 0), lnb(0, 0), moe_wr[0], br(0), seq=sd, tm=n_s, rows_prev=True)
    conv_s = u_s.reshape(bd, sd, d)[:, sd - 2:]
    h = moe(0, h, lg, tm_rows=64, tm_tok=n_s)
    pos_s = PAST_LEN + (jnp.arange(n_s, dtype=jnp.int32) % sd)
    rc_s, rs_s, rw_s, gates_s, q_s = _nsa_proj(h, w_cat, _rope_tables(pos_s), tm=n_s, seq=sd, heads_out=False)

    n_pages = PAST_LEN // PAGE_SIZE
    gather = lambda pool: pool.reshape(pool.shape[0], PAGE_SIZE, KV_COLS)[page_table].reshape(
        bd, n_pages * PAGE_SIZE, KV_COLS)
    kvc_s = _compress(gather(cache_kv_cmp), cw)
    past_s = gather(cache_kv_sel)
    q5 = q_s.reshape(bd, sd, N_KV_HEADS, HEADS_PER_GROUP, HEAD_DIM).transpose(0, 3, 2, 1, 4)
    eye = jnp.eye(N_KV_HEADS, dtype=F32)
    qbd = jnp.einsum('bqgtd,gh->bqgthd', q5, eye).reshape(bd, N_HEADS * sd, N_KV_HEADS * HEAD_DIM).astype(BF16)
    g5 = gates_s.reshape(bd, sd, N_KV_HEADS, LANES)[..., :HEADS_PER_GROUP * 3]
    g5 = g5.reshape(bd, sd, N_KV_HEADS, HEADS_PER_GROUP, 3).transpose(0, 3, 2, 1, 4).reshape(bd, N_HEADS * sd, 3)
    gt = jnp.pad(g5, ((0, 0), (0, 0), (0, 5)))
    pad_new = lambda r: jnp.pad(r.reshape(bd, sd, KV_COLS), ((0, 0), (0, 16 - sd), (0, 0)))
    w_buf = cache_kv_win.shape[1]
    cache_w = cache_kv_win.reshape(bd, w_buf, KV_COLS)
    o_s = _nsa_sample(qbd, gt, kvc_s, past_s, pad_new(rs_s), cache_w, pad_new(rw_s), dec=sd, past=PAST_LEN)
    o_s = o_s.reshape(bd, HEADS_PER_GROUP, N_KV_HEADS, sd, HEAD_DIM).transpose(0, 3, 2, 1, 4).reshape(n_s, d)
    h, lg = _out_proj(o_s, h, nsa_wo[0], lng(1, 0), lnb(1, 0), moe_wr[1], br(1))
    y_sample = moe(1, h, lg, tm_rows=64, tm_tok=n_s).reshape(bd, sd, d)
    kv_cmp_s = rc_s.reshape((bd, sd) + kv_shape)
    kv_sel_s = rs_s.reshape((bd, sd) + kv_shape)
    kv_win_s = jnp.concatenate([cache_kv_win, rw_s.reshape((bd, sd) + kv_shape)], axis=1)[:, -w_buf:]

    return (y_prompt, y_sample, conv_p[None], kv_cmp_p, kv_sel_p, kv_win_p,
            conv_s[None], kv_cmp_s, kv_sel_s, kv_win_s)
```

```python
import functools

import numpy as np
import jax
import jax.numpy as jnp
from jax import lax
from jax.experimental import pallas as pl
from jax.experimental.pallas import tpu as pltpu

F32 = jnp.float32
BF16 = jnp.bfloat16
HIGHEST = lax.Precision.HIGHEST

D_MODEL = 1024
DEPTH = 2
PAST_LEN = 8192
PAGE_SIZE = 128
N_HEADS = 16
N_KV_HEADS = 4
HEADS_PER_GROUP = N_HEADS // N_KV_HEADS
HEAD_DIM = D_MODEL // N_HEADS
ROT_DIM = HEAD_DIM // 4
ROPE_THETA = 500000.0
CMP_BLK = 32
CMP_STRIDE = 16
CMP_HID = 2 * HEAD_DIM
SEL_BLK = 64
SEL_TOPN = 16
WINDOW = 512
N_EXPERTS = 32
TOP_K = 4
D_FF = D_MODEL
SWIGLU_LIMIT = 7.0
SWIGLU_ALPHA = 1.702
ALPHA = (2 * DEPTH) ** 0.25
LN_EPS = 1e-5
NEG = -1e30
FORCE = 1e4

KV_COLS = 2 * N_KV_HEADS * HEAD_DIM
LANES = 128
VMEM_LIMIT = 56 * 2 ** 20


def _params(sem, vmem=VMEM_LIMIT):
    return pltpu.CompilerParams(dimension_semantics=sem, vmem_limit_bytes=vmem)


def _ln(x, g, b):
    mu = jnp.mean(x, axis=-1, keepdims=True)
    xc = x - mu
    var = jnp.mean(xc * xc, axis=-1, keepdims=True)
    return xc * lax.rsqrt(var + LN_EPS) * g + b


def _nt(a, b):
    return lax.dot_general(a, b, (((1,), (1,)), ((), ())), preferred_element_type=F32)


def _router(h, wr_ref, br_ref):
    return jnp.dot(h, wr_ref[...], precision=HIGHEST, preferred_element_type=F32) + br_ref[...]


def _full(shape):
    return pl.BlockSpec(shape, lambda *_: (0,) * len(shape))


def _conv_body(x_ref, p_ref, win_ref, wc_ref, wout_ref, lng_ref, lnb_ref, wr_ref, br_ref,
               h_ref, st_ref, lg_ref, carry_ref, *, seq, tm, rows_prev):
    i = pl.program_id(0)

    @pl.when(i == 0)
    def _():
        carry_ref[...] = jnp.zeros_like(carry_ref)

    x = x_ref[...]
    d = x.shape[1]
    z = jnp.dot(x.astype(BF16), win_ref[...], preferred_element_type=F32)
    bg, c, xh = z[:, :d], z[:, d:2 * d], z[:, 2 * d:]
    u = c * xh
    row = lax.broadcasted_iota(jnp.int32, (tm, 1), 0)
    t = (i * tm + row) & (seq - 1)
    um1 = pltpu.roll(u, 1, 0)
    um2 = pltpu.roll(u, 2, 0)
    c0 = carry_ref[0:1, :]
    c1 = carry_ref[1:2, :]
    um1 = jnp.where(row == 0, c1, um1)
    um2 = jnp.where(row == 0, c0, jnp.where(row == 1, c1, um2))
    if rows_prev:
        p0, p1 = p_ref[0], p_ref[1]
    else:
        p0, p1 = p_ref[0, 0:1, :], p_ref[0, 1:2, :]
    um1 = jnp.where(t >= 1, um1, p1)
    um2 = jnp.where(t >= 2, um2, jnp.where(t == 1, p1, p0))
    conv = wc_ref[0:1, :] * um2 + wc_ref[1:2, :] * um1 + wc_ref[2:3, :] * u
    y = jnp.dot((bg * conv).astype(BF16), wout_ref[...], preferred_element_type=F32)
    h = _ln(ALPHA * x + y, lng_ref[...], lnb_ref[...])
    h_ref[...] = h
    lg_ref[...] = _router(h, wr_ref, br_ref)
    carry_ref[0:2, :] = u[tm - 2:tm, :]
    if rows_prev:
        st_ref[...] = u
    else:
        st_ref[0] = u[tm - 2:tm, :]


def _conv_layer(x, prev, w_in, w_conv, w_out, lng, lnb, wr, br, *, seq, tm, rows_prev):
    n, d = x.shape
    assert n % tm == 0 and seq >= 2 and seq & (seq - 1) == 0
    assert (seq % tm == 0) if not rows_prev else (tm % seq == 0 and n == tm)
    if rows_prev:
        p_spec = pl.BlockSpec((2, tm, d), lambda i: (0, i, 0))
        st_shape = jax.ShapeDtypeStruct((n, d), F32)
        st_spec = pl.BlockSpec((tm, d), lambda i: (i, 0))
    else:
        per = seq // tm
        p_spec = pl.BlockSpec((1, 2, d), lambda i: (i // per, 0, 0))
        st_shape = jax.ShapeDtypeStruct((n // seq, 2, d), F32)
        st_spec = pl.BlockSpec((1, 2, d), lambda i: (i // per, 0, 0))
    return pl.pallas_call(
        functools.partial(_conv_body, seq=seq, tm=tm, rows_prev=rows_prev),
        grid=(n // tm,),
        in_specs=[pl.BlockSpec((tm, d), lambda i: (i, 0)), p_spec,
                  _full((d, 3 * d)), _full((3, d)), _full((d, d)), _full((1, d)), _full((1, d)),
                  _full((d, N_EXPERTS)), _full((1, N_EXPERTS))],
        out_specs=[pl.BlockSpec((tm, d), lambda i: (i, 0)), st_spec,
                   pl.BlockSpec((tm, N_EXPERTS), lambda i: (i, 0))],
        out_shape=[jax.ShapeDtypeStruct((n, d), F32), st_shape,
                   jax.ShapeDtypeStruct((n, N_EXPERTS), F32)],
        scratch_shapes=[pltpu.VMEM((8, d), F32)],
        compiler_params=_params(("arbitrary",)),
        name="conv_mixer",
    )(x, prev, w_in.astype(BF16), w_conv, w_out.astype(BF16), lng, lnb, wr, br)


def _moe_body(te_ref, nv_ref, x_ref, w1_ref, b1_ref, w2_ref, b2_ref, o_ref, w1b, w2b):
    i = pl.program_id(0)
    e = te_ref[i]
    prev = te_ref[jnp.maximum(i - 1, 0)]
    valid = i < nv_ref[0]

    @pl.when(valid & ((i == 0) | (e != prev)))
    def _():
        w1b[...] = w1_ref[0, 0].astype(BF16)
        w2b[...] = w2_ref[0, 0].astype(BF16)

    @pl.when(valid)
    def _():
        hgu = jnp.dot(x_ref[...].astype(BF16), w1b[...], preferred_element_type=F32) + b1_ref[0, 0]
        g = jnp.minimum(hgu[:, :D_FF], SWIGLU_LIMIT)
        u = jnp.clip(hgu[:, D_FF:], -SWIGLU_LIMIT, SWIGLU_LIMIT)
        a = (u + 1.0) * (g * jax.nn.sigmoid(SWIGLU_ALPHA * g))
        o_ref[...] = jnp.dot(a.astype(BF16), w2b[...], preferred_element_type=F32) + b2_ref[0, 0]

    @pl.when(jnp.logical_not(valid))
    def _():
        o_ref[...] = jnp.zeros_like(o_ref)


def _moe_experts(xb, tile_e, n_valid, w1, b1, w2, b2, *, layer, tm):
    n_rows, d = xb.shape
    n_tiles = n_rows // tm
    grid_spec = pltpu.PrefetchScalarGridSpec(
        num_scalar_prefetch=2,
        grid=(n_tiles,),
        in_specs=[pl.BlockSpec((tm, d), lambda i, te, nv: (i, 0)),
                  pl.BlockSpec((1, 1, d, 2 * D_FF), lambda i, te, nv: (layer, te[i], 0, 0)),
                  pl.BlockSpec((1, 1, 1, 2 * D_FF), lambda i, te, nv: (layer, te[i], 0, 0)),
                  pl.BlockSpec((1, 1, D_FF, d), lambda i, te, nv: (layer, te[i], 0, 0)),
                  pl.BlockSpec((1, 1, 1, d), lambda i, te, nv: (layer, te[i], 0, 0))],
        out_specs=pl.BlockSpec((tm, d), lambda i, te, nv: (i, 0)),
        scratch_shapes=[pltpu.VMEM((d, 2 * D_FF), BF16), pltpu.VMEM((D_FF, d), BF16)],
    )
    return pl.pallas_call(
        _moe_body, grid_spec=grid_spec,
        out_shape=jax.ShapeDtypeStruct((n_rows, d), F32),
        compiler_params=_params(("arbitrary",)),
        name="moe_experts",
    )(tile_e, n_valid, xb, w1, b1[:, :, None, :], w2, b2[:, :, None, :])


def _combine_body(h_ref, yg_ref, gate_ref, lng_ref, lnb_ref, o_ref):
    gate = gate_ref[...]
    f = gate[:, 0:1] * yg_ref[0]
    for k in range(1, TOP_K):
        f = f + gate[:, k:k + 1] * yg_ref[k]
    o_ref[...] = _ln(ALPHA * h_ref[...] + f, lng_ref[...], lnb_ref[...])


def _moe_combine(h, yg, gate, lng, lnb, *, tm):
    n, d = h.shape
    return pl.pallas_call(
        _combine_body, grid=(n // tm,),
        in_specs=[pl.BlockSpec((tm, d), lambda i: (i, 0)),
                  pl.BlockSpec((TOP_K, tm, d), lambda i: (0, i, 0)),
                  pl.BlockSpec((tm, TOP_K), lambda i: (i, 0)),
                  _full((1, d)), _full((1, d))],
        out_specs=pl.BlockSpec((tm, d), lambda i: (i, 0)),
        out_shape=jax.ShapeDtypeStruct((n, d), F32),
        compiler_params=_params(("arbitrary",)),
        name="moe_combine",
    )(h, yg, gate, lng, lnb)


def _moe_layer(h, logits, w1, b1, w2, b2, lng, lnb, *, layer, tm_rows, tm_tok):
    n, d = h.shape
    top_v, top_e = lax.top_k(logits, TOP_K)
    gate = jax.nn.softmax(top_v, axis=-1)
    flat_e = top_e.reshape(-1)
    nk = n * TOP_K
    onehot = (flat_e[:, None] == jnp.arange(N_EXPERTS, dtype=jnp.int32)[None, :]).astype(jnp.int32)
    csum = jnp.cumsum(onehot, axis=0)
    rank = jnp.sum((csum - onehot) * onehot, axis=-1)
    counts = csum[-1]
    padded = (counts + tm_rows - 1) // tm_rows * tm_rows
    pad_end = jnp.cumsum(padded)
    pad_start = pad_end - padded
    dest = (pad_start[flat_e] + rank).astype(jnp.int32)
    n_tiles = (nk + N_EXPERTS * (tm_rows - 1) + tm_rows - 1) // tm_rows
    n_rows = n_tiles * tm_rows
    tile_start = jnp.arange(n_tiles, dtype=jnp.int32) * tm_rows
    tile_e = jnp.minimum(jnp.sum((pad_end[None, :] <= tile_start[:, None]).astype(jnp.int32), axis=1), N_EXPERTS - 1)
    n_valid = (pad_end[-1] // tm_rows).astype(jnp.int32).reshape(1)
    row_tok = jnp.zeros((n_rows,), jnp.int32).at[dest].set(jnp.arange(nk, dtype=jnp.int32) // TOP_K)
    xb = h[row_tok]
    yb = _moe_experts(xb, tile_e, n_valid, w1, b1, w2, b2, layer=layer, tm=tm_rows)
    yg = yb[dest.reshape(n, TOP_K).T]
    return _moe_combine(h, yg, gate, lng, lnb, tm=tm_tok)


def _proj_body(h_ref, w_ref, rc_ref, rs1_ref, rs2_ref, *out_refs, heads_out, seq, q_scale):
    z = jnp.dot(h_ref[...].astype(BF16), w_ref[...], preferred_element_type=F32)
    cc, s1, s2 = rc_ref[...], rs1_ref[...], rs2_ref[...]

    def rope(x):
        return x * cc + pltpu.roll(x, LANES - ROT_DIM // 2, 1) * s1 + pltpu.roll(x, ROT_DIM // 2, 1) * s2

    def rope_cols(lo, hi):
        return [rope(z[:, c:c + LANES]) for c in range(lo, hi, LANES)]

    if heads_out:
        rows_c, rows_s, rows_w, g_ref, q_ref, ks_ref, vs_ref, kw_ref, vw_ref = out_refs
    else:
        rows_c, rows_s, rows_w, g_ref, q_ref = out_refs
    half = KV_COLS // 2
    for br, rows in enumerate((rows_c, rows_s, rows_w)):
        base = br * KV_COLS
        k = jnp.concatenate(rope_cols(base, base + half), axis=1)
        v = z[:, base + half:base + KV_COLS]
        kv = jnp.concatenate([k, v], axis=1)
        if heads_out:
            rows[0] = kv.T
        else:
            rows[...] = kv
        if heads_out and br >= 1:
            tm = k.shape[0]
            t = (pl.program_id(0) % (seq // tm)) * tm + lax.broadcasted_iota(jnp.int32, (tm, HEAD_DIM), 0)
            lane = lax.broadcasted_iota(jnp.int32, (tm, HEAD_DIM), 1)
            blk_onehot = (lane == lax.shift_right_logical(t, 6)).astype(BF16)
            one_lane = (lane == 0).astype(BF16)
            k_ref, v_ref = (ks_ref, vs_ref) if br == 1 else (kw_ref, vw_ref)
            for g in range(N_KV_HEADS):
                kg = k[:, g * HEAD_DIM:(g + 1) * HEAD_DIM].astype(BF16)
                vg = v[:, g * HEAD_DIM:(g + 1) * HEAD_DIM].astype(BF16)
                k_ref[0, g] = jnp.concatenate([kg, blk_onehot], axis=1) if br == 1 else kg
                v_ref[0, g] = jnp.concatenate([vg, one_lane], axis=1)
    qb = 3 * KV_COLS
    q = jnp.concatenate(rope_cols(qb, qb + D_MODEL), axis=1) * q_scale
    if heads_out:
        for hd in range(N_HEADS):
            q_ref[0, hd] = q[:, hd * HEAD_DIM:(hd + 1) * HEAD_DIM].astype(BF16)
    else:
        q_ref[...] = q
    g_ref[...] = jax.nn.sigmoid(z[:, qb + D_MODEL:])


def _rope_tables(pos):
    half = ROT_DIM // 2
    inv = ROPE_THETA ** (-jnp.arange(half, dtype=F32) * 2.0 / ROT_DIM)
    ang = pos.astype(F32)[:, None] * inv[None, :]
    cos, sin = jnp.cos(ang), jnp.sin(ang)
    zeros = jnp.zeros((pos.shape[0], HEAD_DIM - ROT_DIM), F32)
    ones = jnp.ones_like(zeros)
    z8 = jnp.zeros_like(sin)
    c = jnp.concatenate([cos, cos, ones], axis=1)
    s1 = jnp.concatenate([-sin, z8, zeros], axis=1)
    s2 = jnp.concatenate([z8, sin, zeros], axis=1)
    rep = LANES // HEAD_DIM
    return tuple(jnp.tile(a, (1, rep)) for a in (c, s1, s2))


def _proj_weights(kv_w, wq, wg):
    wg4 = wg.reshape(D_MODEL, N_KV_HEADS, HEADS_PER_GROUP * 3)
    wg4 = jnp.pad(wg4, ((0, 0), (0, 0), (0, LANES - HEADS_PER_GROUP * 3))).reshape(D_MODEL, N_KV_HEADS * LANES)
    return jnp.concatenate([kv_w, wq, wg4], axis=1).astype(BF16)


def _nsa_proj(h, w_cat, tables, *, tm, seq, heads_out):
    n, d = h.shape
    ncol = w_cat.shape[1]
    per = tables[0].shape[0] // tm
    tab_spec = pl.BlockSpec((tm, LANES), lambda i: (i % per, 0))
    row_spec = lambda w: pl.BlockSpec((tm, w), lambda i: (i, 0))
    g_spec, g_shape = row_spec(N_KV_HEADS * LANES), jax.ShapeDtypeStruct((n, N_KV_HEADS * LANES), F32)
    if heads_out:
        bsz = n // seq
        tps = seq // tm
        out_specs = [pl.BlockSpec((1, KV_COLS, tm), lambda i: (i // tps, 0, i % tps))] * 3 + [g_spec]
        out_shape = [jax.ShapeDtypeStruct((bsz, KV_COLS, seq), F32)] * 3 + [g_shape]
        assert seq // SEL_BLK <= HEAD_DIM
        hspec = lambda nh, w: pl.BlockSpec((1, nh, tm, w), lambda i: (i // tps, 0, i % tps, 0))
        hshape = lambda nh, w: jax.ShapeDtypeStruct((bsz, nh, seq, w), BF16)
        widths = (2 * HEAD_DIM, 2 * HEAD_DIM, HEAD_DIM, 2 * HEAD_DIM)
        out_specs += [hspec(N_HEADS, HEAD_DIM)] + [hspec(N_KV_HEADS, w) for w in widths]
        out_shape += [hshape(N_HEADS, HEAD_DIM)] + [hshape(N_KV_HEADS, w) for w in widths]
    else:
        out_specs = [row_spec(KV_COLS)] * 3 + [g_spec, row_spec(d)]
        out_shape = [jax.ShapeDtypeStruct((n, KV_COLS), F32)] * 3 + [g_shape, jax.ShapeDtypeStruct((n, d), F32)]
    q_scale = HEAD_DIM ** -0.5 * (float(np.log2(np.e)) if heads_out else 1.0)
    return pl.pallas_call(
        functools.partial(_proj_body, heads_out=heads_out, seq=seq, q_scale=q_scale), grid=(n // tm,),
        in_specs=[row_spec(d), _full((d, ncol)), tab_spec, tab_spec, tab_spec],
        out_specs=out_specs, out_shape=out_shape,
        compiler_params=_params(("arbitrary",)),
        name="nsa_proj",
    )(h, w_cat, *tables)


def _cmp_math(xt, w1_ref, pe_ref, w2_ref, o_ref, nch):
    k = pl.program_id(1) // 2
    x = jnp.concatenate(
        [xt[pl.ds(c, nch, stride=CMP_STRIDE), :].astype(BF16) for c in range(CMP_STRIDE)],
        axis=1)
    acc = jnp.dot(x, w1_ref[0], preferred_element_type=F32)
    pe = pe_ref[pl.ds(k, 1), :]
    hid = []
    for gs in range(2):
        pa = acc[:, gs * 2 * CMP_HID:gs * 2 * CMP_HID + CMP_HID]
        pb = acc[:, gs * 2 * CMP_HID + CMP_HID:(gs + 1) * 2 * CMP_HID]
        pb_next = pltpu.roll(pb, nch - 1, 0)
        hid.append(jax.nn.gelu(pa + pb_next + pe))
    hid = jnp.concatenate(hid, axis=1).astype(BF16)
    o_ref[0] = jnp.dot(hid, w2_ref[0], preferred_element_type=F32)


def _cmp_t_body(r_ref, w1_ref, pe_ref, w2_ref, o_ref, xt, *, nch):
    for p in range(r_ref.shape[2] // LANES):
        xt[p * LANES:(p + 1) * LANES, :] = r_ref[0, :, p * LANES:(p + 1) * LANES].T
    _cmp_math(xt, w1_ref, pe_ref, w2_ref, o_ref, nch)


def _cmp_paged_body(pt_ref, cache_ref, w1_ref, pe_ref, w2_ref, o_ref, pbuf, xt, sem, *, nch, n_pages):
    ncol = pl.num_programs(1)
    step = pl.program_id(0) * ncol + pl.program_id(1)
    nsteps = pl.num_programs(0) * ncol
    slot = step & 1

    def page_copy(st, p, sl):
        bb = st // ncol
        col = pl.multiple_of((st - bb * ncol) * LANES, LANES)
        return pltpu.make_async_copy(cache_ref.at[pt_ref[bb * n_pages + p], pl.ds(col, LANES), :],
                                     pbuf.at[sl, p], sem.at[sl])

    @pl.when(step == 0)
    def _():
        for p in range(n_pages):
            page_copy(step, p, slot).start()

    @pl.when(step + 1 < nsteps)
    def _():
        for p in range(n_pages):
            page_copy(step + 1, p, 1 - slot).start()

    for p in range(n_pages):
        page_copy(step, p, slot).wait()
    for p in range(n_pages):
        xt[p * PAGE_SIZE:(p + 1) * PAGE_SIZE, :] = pbuf[slot, p].T
    _cmp_math(xt, w1_ref, pe_ref, w2_ref, o_ref, nch)


def _cmp_weights(cmp_pe, cmp_w1, cmp_b1, cmp_w2):
    eye = jnp.eye(2, dtype=F32)
    w1r = cmp_w1.reshape(2, 2, CMP_STRIDE, HEAD_DIM, CMP_HID)
    w1bd = jnp.einsum('ab,kncdh->kcadbnh', eye, w1r).reshape(2, CMP_STRIDE * 2 * HEAD_DIM, 4 * CMP_HID)
    w2bd = jnp.einsum('ab,khd->kahbd', eye, cmp_w2).reshape(2, 2 * CMP_HID, 2 * HEAD_DIM)
    pe_term = jnp.einsum('kcd,kcdh->kh', cmp_pe, cmp_w1, precision=HIGHEST) + cmp_b1
    return w1bd.astype(BF16), pe_term, w2bd.astype(BF16)


def _cmp_specs(cw, nch, imap):
    w1bd, pe_term, w2bd = cw
    return ([pl.BlockSpec((1,) + w1bd.shape[1:], imap(lambda b, j: (j // 2, 0, 0))),
             pl.BlockSpec(pe_term.shape, imap(lambda b, j: (0, 0))),
             pl.BlockSpec((1,) + w2bd.shape[1:], imap(lambda b, j: (j // 2, 0, 0)))],
            pl.BlockSpec((1, nch, LANES), imap(lambda b, j: (b, 0, j))))


def _compress_t(rows_t, cw):
    bsz, _, t = rows_t.shape
    nch = t // CMP_STRIDE
    w_specs, o_spec = _cmp_specs(cw, nch, lambda f: f)
    return pl.pallas_call(
        functools.partial(_cmp_t_body, nch=nch), grid=(bsz, KV_COLS // LANES),
        in_specs=[pl.BlockSpec((1, LANES, t), lambda b, j: (b, j, 0))] + w_specs,
        out_specs=o_spec,
        out_shape=jax.ShapeDtypeStruct((bsz, nch, KV_COLS), F32),
        scratch_shapes=[pltpu.VMEM((t, LANES), F32)],
        compiler_params=_params(("arbitrary", "arbitrary")),
        name="compress_kv",
    )(rows_t, *cw)


def _compress_paged(cache_t, page_table, cw):
    bsz, n_pages = page_table.shape
    nch = n_pages * PAGE_SIZE // CMP_STRIDE
    w_specs, o_spec = _cmp_specs(cw, nch, lambda f: (lambda b, j, pt: f(b, j)))
    grid_spec = pltpu.PrefetchScalarGridSpec(
        num_scalar_prefetch=1, grid=(bsz, KV_COLS // LANES),
        in_specs=[pl.BlockSpec(memory_space=pl.ANY)] + w_specs,
        out_specs=o_spec,
        scratch_shapes=[pltpu.VMEM((2, n_pages, LANES, PAGE_SIZE), F32),
                        pltpu.VMEM((n_pages * PAGE_SIZE, LANES), F32),
                        pltpu.SemaphoreType.DMA((2,))])
    return pl.pallas_call(
        functools.partial(_cmp_paged_body, nch=nch, n_pages=n_pages), grid_spec=grid_spec,
        out_shape=jax.ShapeDtypeStruct((bsz, nch, KV_COLS), F32),
        compiler_params=_params(("arbitrary", "arbitrary")),
        name="compress_kv_paged",
    )(page_table.reshape(-1), cache_t, *cw)


def _softmax_probs(s, m, exp_fn=jnp.exp):
    s = jnp.where(m, s, NEG)
    mx = jnp.max(s, axis=-1, keepdims=True)
    e = jnp.where(m, exp_fn(s - mx), 0.0)
    den = jnp.sum(e, axis=-1, keepdims=True)
    return e / jnp.maximum(den, 1e-30)


def _select_blocks(imp_t, qpos_row, n_rows):
    lanes = imp_t.shape[1]
    j = lax.broadcasted_iota(jnp.int32, (n_rows, lanes), 0)
    cur = lax.shift_right_logical(qpos_row, 6)
    valid = j <= cur
    forced = (j == 0) | (j == cur) | (j == cur - 1)
    sc = jnp.where(valid, jnp.where(forced, FORCE, imp_t[:n_rows]), NEG)
    return sc, j, valid


def _overlap_t(n_sel_pad, n_cmp_pad):
    c0 = np.arange(n_cmp_pad)[None, :] * CMP_STRIDE
    s0 = np.arange(n_sel_pad)[:, None] * SEL_BLK
    ov = np.clip(np.minimum(c0 + CMP_BLK, s0 + SEL_BLK) - np.maximum(c0, s0), 0, None).astype(np.float32) / CMP_BLK
    return ov


def _attn_p_body(q_ref, kck_ref, kcv_ref, ks_ref, vs_ref, kw_ref, vw_ref, g_ref, h_ref,
                 ovt_ref, wo_ref, lng_ref, lnb_ref, wr_ref, br_ref,
                 o_ref, lg_ref, o_scr, *, tq, seq, kc):
    i = pl.program_id(1)
    g = pl.program_id(2)
    t0 = i * tq
    rows = HEADS_PER_GROUP * tq
    n_sel = seq // SEL_BLK
    q = q_ref[0].reshape(rows, HEAD_DIM)
    tpos = t0 + (lax.broadcasted_iota(jnp.int32, (rows, 1), 0) & (tq - 1))

    n_cmp = kck_ref.shape[2]
    s = _nt(q, kck_ref[0, 0])
    cend = lax.broadcasted_iota(jnp.int32, (1, n_cmp), 1) * CMP_STRIDE + (CMP_BLK - 1)
    p_c = _softmax_probs(s, cend <= tpos, jnp.exp2)
    o_c = jnp.dot(p_c.astype(BF16), kcv_ref[0, 0], preferred_element_type=F32)
    psum = p_c[0:tq]
    for qh in range(1, HEADS_PER_GROUP):
        psum = psum + p_c[qh * tq:(qh + 1) * tq]
    imp_t = lax.dot_general(ovt_ref[...], psum, (((1,), (1,)), ((), ())),
                            precision=HIGHEST, preferred_element_type=F32)

    qrow = t0 + lax.broadcasted_iota(jnp.int32, (1, tq), 1)
    sc, j, valid = _select_blocks(imp_t, qrow, n_sel)
    rank = jnp.zeros((n_sel, tq), jnp.int32)
    for jp in range(n_sel):
        r = sc[jp:jp + 1, :]
        beats = (r > sc) | ((r == sc) & (j > jp))
        rank = rank + beats.astype(jnp.int32)
    sel_t = (valid & (rank < SEL_TOPN)).astype(F32)
    sel_t = jnp.concatenate([sel_t, jnp.zeros((LANES - n_sel, tq), F32)], axis=0)
    sel = sel_t.T[:, 0:HEAD_DIM]
    blk_lane = lax.broadcasted_iota(jnp.int32, (1, HEAD_DIM), 1)
    r_i = lax.broadcasted_iota(jnp.int32, (tq, 1), 0)
    tq_pos = t0 + r_i
    t0a = pl.multiple_of(t0, tq)
    causal = jnp.where(lax.broadcasted_iota(jnp.int32, (1, tq), 1) <= r_i, 0.0, NEG)
    causal = jnp.concatenate([causal] * HEADS_PER_GROUP, axis=0)

    s = _nt(q, ks_ref[0, 0, pl.ds(t0a, tq), 0:HEAD_DIM]) + causal
    m0 = jnp.max(s, axis=-1, keepdims=True)
    acc0 = jnp.dot(jnp.exp2(s - m0).astype(BF16), vs_ref[0, 0, pl.ds(t0a, tq), :], preferred_element_type=F32)
    sweep = jnp.where((sel > 0.5) & (blk_lane < lax.shift_right_logical(t0, 6)), 0.0, NEG).astype(BF16)
    q_aug = jnp.concatenate([q, jnp.concatenate([sweep] * HEADS_PER_GROUP, axis=0)], axis=1)

    def sweep_chunk(c, carry):
        m_run, acc = carry
        k0 = pl.multiple_of(c * kc, kc)
        s = _nt(q_aug, ks_ref[0, 0, pl.ds(k0, kc), :])
        m_new = jnp.maximum(m_run, jnp.max(s, axis=-1, keepdims=True))
        pv = jnp.dot(jnp.exp2(s - m_new).astype(BF16), vs_ref[0, 0, pl.ds(k0, kc), :],
                     preferred_element_type=F32)
        return m_new, jnp.exp2(m_run - m_new) * acc + pv

    _, acc = lax.fori_loop(0, (t0 + kc - 1) // kc, sweep_chunk, (m0, acc0))
    o_s = acc[:, 0:HEAD_DIM] / acc[:, HEAD_DIM:HEAD_DIM + 1]

    ws = pl.multiple_of(jnp.maximum(t0 - WINDOW, 0), tq)
    wi = ws + lax.broadcasted_iota(jnp.int32, (1, WINDOW), 1)
    wbias = jnp.where((wi > tq_pos - WINDOW) & (wi < t0), 0.0, NEG)
    s_d = _nt(q, kw_ref[0, 0, pl.ds(t0a, tq), :]) + causal
    s_w = _nt(q, kw_ref[0, 0, pl.ds(ws, WINDOW), :]) + jnp.concatenate([wbias] * HEADS_PER_GROUP, axis=0)
    m = jnp.maximum(jnp.max(s_d, axis=-1, keepdims=True), jnp.max(s_w, axis=-1, keepdims=True))
    acc = (jnp.dot(jnp.exp2(s_d - m).astype(BF16), vw_ref[0, 0, pl.ds(t0a, tq), :], preferred_element_type=F32)
           + jnp.dot(jnp.exp2(s_w - m).astype(BF16), vw_ref[0, 0, pl.ds(ws, WINDOW), :],
                     preferred_element_type=F32))
    o_w = acc[:, 0:HEAD_DIM] / acc[:, HEAD_DIM:HEAD_DIM + 1]

    gates = g_ref[...]
    for qh in range(HEADS_PER_GROUP):
        sl = slice(qh * tq, (qh + 1) * tq)
        o_h = (gates[:, 3 * qh:3 * qh + 1] * o_c[sl] + gates[:, 3 * qh + 1:3 * qh + 2] * o_s[sl]
               + gates[:, 3 * qh + 2:3 * qh + 3] * o_w[sl])
        o_scr[g, :, qh * HEAD_DIM:(qh + 1) * HEAD_DIM] = o_h

    @pl.when(g == N_KV_HEADS - 1)
    def _():
        o = jnp.concatenate([o_scr[gg] for gg in range(N_KV_HEADS)], axis=1).astype(BF16)
        y = jnp.dot(o, wo_ref[...], preferred_element_type=F32)
        h = _ln(ALPHA * h_ref[...] + y, lng_ref[...], lnb_ref[...])
        o_ref[...] = h
        lg_ref[...] = _router(h, wr_ref, br_ref)


def _nsa_prompt(h, qh, kck, kcv, ks, vs, kw, vw, gates, wo, lng, lnb, wr, br, *, tq, kc):
    n, d = h.shape
    bsz, _, seq, _ = ks.shape
    nt = seq // tq
    n_cmp = kck.shape[2]
    ovt = jnp.asarray(_overlap_t(LANES, n_cmp))
    assert kc % tq == 0 and seq % kc == 0 and SEL_BLK % 64 == 0 and tq % SEL_BLK == 0
    grp = lambda a: pl.BlockSpec((1, 1) + a.shape[2:], lambda b, i, g: (b, g, 0, 0))
    row = lambda w: pl.BlockSpec((tq, w), lambda b, i, g: (b * nt + i, 0))
    return pl.pallas_call(
        functools.partial(_attn_p_body, tq=tq, seq=seq, kc=kc),
        grid=(bsz, nt, N_KV_HEADS),
        in_specs=[pl.BlockSpec((1, HEADS_PER_GROUP, tq, HEAD_DIM), lambda b, i, g: (b, g, i, 0)),
                  grp(kck), grp(kcv), grp(ks), grp(vs), grp(kw), grp(vw),
                  pl.BlockSpec((tq, LANES), lambda b, i, g: (b * nt + i, g)), row(d),
                  _full(ovt.shape), _full((d, d)), _full((1, d)), _full((1, d)),
                  _full((d, N_EXPERTS)), _full((1, N_EXPERTS))],
        out_specs=[row(d), row(N_EXPERTS)],
        out_shape=[jax.ShapeDtypeStruct((n, d), F32), jax.ShapeDtypeStruct((n, N_EXPERTS), F32)],
        scratch_shapes=[pltpu.VMEM((N_KV_HEADS, tq, HEADS_PER_GROUP * HEAD_DIM), F32)],
        compiler_params=_params(("arbitrary", "arbitrary", "arbitrary")),
        name="nsa_prompt",
    )(qh, kck, kcv, ks, vs, kw, vw, gates, h, ovt, wo.astype(BF16), lng, lnb, wr, br)


def _attn_s_body(pt_ref, q_ref, gt_ref, kvc_ref, cache_ref, ns_ref, cw_ref, nw_ref, ovt_ref, exp_ref,
                 o_ref, sc_ref, kvbuf, sem, msk_scr, m_scr, l_scr, acc_scr, ocw_scr, *, dec, past, ppc):
    nrow = q_ref.shape[1]
    half = KV_COLS // 2
    c = pl.program_id(1)
    nc = pl.num_programs(1)
    step = pl.program_id(0) * nc + c
    nsteps = pl.num_programs(0) * nc
    slot = step & 1

    def chunk_copies(st, sl):
        return [pltpu.make_async_copy(cache_ref.at[pt_ref[st * ppc + p]],
                                      kvbuf.at[sl, :, pl.ds(p * PAGE_SIZE, PAGE_SIZE)], sem.at[sl])
                for p in range(ppc)]

    @pl.when(step == 0)
    def _():
        for cp in chunk_copies(step, slot):
            cp.start()

    @pl.when(step + 1 < nsteps)
    def _():
        for cp in chunk_copies(step + 1, 1 - slot):
            cp.start()

    q = q_ref[0]
    row = lax.broadcasted_iota(jnp.int32, (nrow, 1), 0)
    tok = row & (dec - 1)
    gsel = lax.shift_right_logical(row, 2) & (N_KV_HEADS - 1)

    def pick(o):
        out = jnp.zeros((nrow, HEAD_DIM), F32)
        for gg in range(N_KV_HEADS):
            out = out + jnp.where(gsel == gg, o[:, gg * HEAD_DIM:(gg + 1) * HEAD_DIM], 0.0)
        return out

    def new_rows(new_ref):
        k_new = new_ref[0, :, 0:half].astype(BF16)
        m_new = lax.broadcasted_iota(jnp.int32, (1, new_ref.shape[1]), 1) <= tok
        return jnp.where(m_new, _nt(q, k_new), NEG), m_new, new_ref[0, :, half:KV_COLS].astype(BF16)

    @pl.when(c == 0)
    def _():
        qpos = past + tok
        n_cmp = kvc_ref.shape[1]
        kc = kvc_ref[0, :, 0:half].astype(BF16)
        vc = kvc_ref[0, :, half:KV_COLS].astype(BF16)
        cend = lax.broadcasted_iota(jnp.int32, (1, n_cmp), 1) * CMP_STRIDE + (CMP_BLK - 1)
        p_c = _softmax_probs(_nt(q, kc), cend <= qpos)
        ocw_scr[0] = pick(jnp.dot(p_c.astype(BF16), vc, preferred_element_type=F32))
        ngt = N_KV_HEADS * dec
        psum = p_c[0:ngt]
        for qh in range(1, HEADS_PER_GROUP):
            psum = psum + p_c[qh * ngt:(qh + 1) * ngt]
        psum = jnp.concatenate([psum, jnp.zeros((LANES - ngt, n_cmp), F32)], axis=0)
        imp_t = lax.dot_general(ovt_ref[...], psum, (((1,), (1,)), ((), ())),
                                precision=HIGHEST, preferred_element_type=F32)

        n_sel = past // SEL_BLK + 1
        n_sel8 = sc_ref.shape[0]
        lane_tok = lax.broadcasted_iota(jnp.int32, (1, LANES), 1) & (dec - 1)
        sc, j, valid = _select_blocks(imp_t, past + lane_tok, n_sel8)
        sc_ref[...] = sc

        def rank_step(jp, rank):
            r = sc_ref[pl.ds(jp, 1), :]
            beats = (r > sc) | ((r == sc) & (j > jp))
            return rank + beats.astype(jnp.int32)

        rank = lax.fori_loop(0, n_sel, rank_step, jnp.zeros((n_sel8, LANES), jnp.int32))
        sel_t = (valid & (rank < SEL_TOPN)).astype(F32)
        sel_t = jnp.concatenate([sel_t, jnp.zeros((ovt_ref.shape[0] - n_sel8, LANES), F32)], axis=0)
        sel = sel_t.T[0:ngt, 0:LANES]
        sel = jnp.concatenate([sel] * HEADS_PER_GROUP, axis=0)
        msk_scr[...] = jnp.where(sel > 0.5, 0.0, NEG).astype(BF16)

        s_n, m_n, v_n = new_rows(ns_ref)
        m0 = jnp.max(s_n, axis=-1, keepdims=True)
        e_n = jnp.where(m_n, jnp.exp(s_n - m0), 0.0)
        m_scr[...] = m0
        l_scr[...] = jnp.sum(e_n, axis=-1, keepdims=True)
        acc_scr[...] = jnp.dot(e_n.astype(BF16), v_n, preferred_element_type=F32)

        n_win = cw_ref.shape[2]
        m_w = lax.broadcasted_iota(jnp.int32, (1, n_win), 1) > tok + (n_win - WINDOW)
        s_o = jnp.where(m_w, jnp.dot(q, cw_ref[0, 0:half, :].astype(BF16), preferred_element_type=F32), NEG)
        s_n, m_n, v_n = new_rows(nw_ref)
        mx = jnp.maximum(jnp.max(s_o, axis=-1, keepdims=True), jnp.max(s_n, axis=-1, keepdims=True))
        e_o = jnp.where(m_w, jnp.exp(s_o - mx), 0.0)
        e_n = jnp.where(m_n, jnp.exp(s_n - mx), 0.0)
        den = jnp.sum(e_o, axis=-1, keepdims=True) + jnp.sum(e_n, axis=-1, keepdims=True)
        o_w = (_nt(e_o.astype(BF16), cw_ref[0, half:KV_COLS, :].astype(BF16))
               + jnp.dot(e_n.astype(BF16), v_n, preferred_element_type=F32))
        ocw_scr[1] = pick(o_w / den)

    for cp in chunk_copies(step, slot):
        cp.wait()
    kt = kvbuf[slot, 0:half, :].astype(BF16)
    vt = kvbuf[slot, half:KV_COLS, :].astype(BF16)
    s = (jnp.dot(q, kt, preferred_element_type=F32)
         + jnp.dot(msk_scr[...], exp_ref[c], preferred_element_type=F32))
    m_old = m_scr[...]
    m_new = jnp.maximum(m_old, jnp.max(s, axis=-1, keepdims=True))
    alpha = jnp.exp(m_old - m_new)
    e = jnp.exp(s - m_new)
    m_scr[...] = m_new
    l_scr[...] = alpha * l_scr[...] + jnp.sum(e, axis=-1, keepdims=True)
    acc_scr[...] = alpha * acc_scr[...] + _nt(e.astype(BF16), vt)

    @pl.when(c == nc - 1)
    def _():
        gt = gt_ref[0]
        o_s = pick(acc_scr[...] / l_scr[...])
        o_ref[0] = gt[:, 0:1] * ocw_scr[0] + gt[:, 1:2] * o_s + gt[:, 2:3] * ocw_scr[1]


def _nsa_sample(qbd, gt, kvc, cache_t, page_table, new_s, cache_w_t, new_w, *, dec, past, ppc):
    bsz, nrow, _ = qbd.shape
    n_pages = past // PAGE_SIZE
    assert n_pages % ppc == 0 and past // SEL_BLK <= LANES
    nc = n_pages // ppc
    ck = ppc * PAGE_SIZE
    ovt = jnp.asarray(_overlap_t(2 * LANES, kvc.shape[1]))
    kblk = (np.arange(past) // SEL_BLK).reshape(nc, 1, ck)
    expand = jnp.asarray(np.arange(LANES)[None, :, None] == kblk, BF16)
    n_sel8 = (past // SEL_BLK + 1 + 7) // 8 * 8
    b3 = lambda a: pl.BlockSpec((1,) + a.shape[1:], lambda b, c, pt: (b, 0, 0))
    full = lambda a: pl.BlockSpec(a.shape, lambda b, c, pt: (0,) * a.ndim)
    grid_spec = pltpu.PrefetchScalarGridSpec(
        num_scalar_prefetch=1, grid=(bsz, nc),
        in_specs=[b3(qbd), b3(gt), b3(kvc), pl.BlockSpec(memory_space=pl.ANY), b3(new_s), b3(cache_w_t),
                  b3(new_w), full(ovt), full(expand)],
        out_specs=pl.BlockSpec((1, nrow, HEAD_DIM), lambda b, c, pt: (b, 0, 0)),
        scratch_shapes=[pltpu.VMEM((n_sel8, LANES), F32),
                        pltpu.VMEM((2, KV_COLS, ck), F32),
                        pltpu.SemaphoreType.DMA((2,)),
                        pltpu.VMEM((nrow, LANES), BF16),
                        pltpu.VMEM((nrow, 1), F32), pltpu.VMEM((nrow, 1), F32),
                        pltpu.VMEM((nrow, KV_COLS // 2), F32),
                        pltpu.VMEM((2, nrow, HEAD_DIM), F32)])
    return pl.pallas_call(
        functools.partial(_attn_s_body, dec=dec, past=past, ppc=ppc), grid_spec=grid_spec,
        out_shape=jax.ShapeDtypeStruct((bsz, nrow, HEAD_DIM), F32),
        compiler_params=_params(("arbitrary", "arbitrary")),
        name="nsa_sample",
    )(page_table.reshape(-1), qbd, gt, kvc, cache_t, new_s, cache_w_t, new_w, ovt, expand)


def _out_body(o_ref, h_ref, wo_ref, lng_ref, lnb_ref, wr_ref, br_ref, y_ref, lg_ref):
    y = jnp.dot(o_ref[...].astype(BF16), wo_ref[...], preferred_element_type=F32)
    h = _ln(ALPHA * h_ref[...] + y, lng_ref[...], lnb_ref[...])
    y_ref[...] = h
    lg_ref[...] = _router(h, wr_ref, br_ref)


def _out_proj(o, h, wo, lng, lnb, wr, br):
    n, d = h.shape
    return pl.pallas_call(
        _out_body, grid=(1,),
        in_specs=[_full((n, d)), _full((n, d)), _full((d, d)), _full((1, d)), _full((1, d)),
                  _full((d, N_EXPERTS)), _full((1, N_EXPERTS))],
        out_specs=[_full((n, d)), _full((n, N_EXPERTS))],
        out_shape=[jax.ShapeDtypeStruct((n, d), F32), jax.ShapeDtypeStruct((n, N_EXPERTS), F32)],
        compiler_params=_params(("arbitrary",)),
        name="nsa_out_proj",
    )(o, h, wo.astype(BF16), lng, lnb, wr, br)


def kernel(x_prompt, x_sample, state_conv, cache_kv_cmp, cache_kv_sel, cache_kv_win, page_table,
           conv_w_in, conv_w, conv_w_out, kv_w, cmp_pe, cmp_w1, cmp_b1, cmp_w2,
           nsa_wq, nsa_wg, nsa_wo, moe_wr, moe_br, moe_w1, moe_b1, moe_w2, moe_b2, ln_g, ln_b):
    bp, sp, d = x_prompt.shape
    bd, sd, _ = x_sample.shape
    kv_shape = (2, N_KV_HEADS, HEAD_DIM)
    lng = lambda l, s: ln_g[l, s][None, :]
    lnb = lambda l, s: ln_b[l, s][None, :]
    moe = lambda l, h, lg, **kw: _moe_layer(h, lg, moe_w1, moe_b1, moe_w2, moe_b2,
                                            lng(l, 1), lnb(l, 1), layer=l, **kw)
    w_cat = _proj_weights(kv_w, nsa_wq[0], nsa_wg[0])
    cw = _cmp_weights(cmp_pe, cmp_w1, cmp_b1, cmp_w2)
    br = lambda l: moe_br[l][None, :]

    n_p = bp * sp
    xp = x_prompt.reshape(n_p, d)
    h, conv_p, lg = _conv_layer(xp, jnp.zeros((bp, 2, d), F32), conv_w_in[0], conv_w[0], conv_w_out[0],
                                lng(0, 0), lnb(0, 0), moe_wr[0], br(0), seq=sp, tm=256, rows_prev=False)
    h = moe(0, h, lg, tm_rows=256, tm_tok=256)
    tabs = _rope_tables(jnp.arange(sp, dtype=jnp.int32))
    rc, rs, rw, gates, qh, ks, vs, kw, vw = _nsa_proj(h, w_cat, tabs, tm=256, seq=sp, heads_out=True)
    kvc = _compress_t(rc, cw)
    kvc_h = kvc.reshape(bp, -1, 2, N_KV_HEADS, HEAD_DIM).transpose(2, 0, 3, 1, 4).astype(BF16)
    h, lg = _nsa_prompt(h, qh, kvc_h[0], kvc_h[1], ks, vs, kw, vw, gates, nsa_wo[0],
                        lng(1, 0), lnb(1, 0), moe_wr[1], br(1), tq=256, kc=512)
    y_prompt = moe(1, h, lg, tm_rows=256, tm_tok=256).reshape(bp, sp, d)
    rows_out = lambda r_t: r_t.reshape((bp,) + kv_shape + (r_t.shape[-1],)).transpose(0, 4, 1, 2, 3)
    kv_cmp_p = rows_out(rc)
    kv_sel_p = rows_out(rs)
    kv_win_p = rows_out(rw[:, :, sp - min(WINDOW, sp):])

    n_s = bd * sd
    xs = x_sample.reshape(n_s, d)
    prev_rows = jnp.repeat(state_conv[0].transpose(1, 0, 2), sd, axis=1)
    h, u_s, lg = _conv_layer(xs, prev_rows, conv_w_in[0], conv_w[0], conv_w_out[0],
                             lng(0, 0), lnb(0, 0), moe_wr[0], br(0), seq=sd, tm=n_s, rows_prev=True)
    conv_s = u_s.reshape(bd, sd, d)[:, sd - 2:]
    h = moe(0, h, lg, tm_rows=64, tm_tok=n_s)
    pos_s = PAST_LEN + (jnp.arange(n_s, dtype=jnp.int32) % sd)
    rc_s, rs_s, rw_s, gates_s, q_s = _nsa_proj(h, w_cat, _rope_tables(pos_s), tm=n_s, seq=sd, heads_out=False)

    cols_major = lambda a: a.transpose(0, 2, 3, 4, 1).reshape(a.shape[0], KV_COLS, a.shape[1])
    kvc_s = _compress_paged(cols_major(cache_kv_cmp), page_table, cw)
    q5 = q_s.reshape(bd, sd, N_KV_HEADS, HEADS_PER_GROUP, HEAD_DIM).transpose(0, 3, 2, 1, 4)
    eye = jnp.eye(N_KV_HEADS, dtype=F32)
    qbd = jnp.einsum('bqgtd,gh->bqgthd', q5, eye).reshape(bd, N_HEADS * sd, N_KV_HEADS * HEAD_DIM).astype(BF16)
    g5 = gates_s.reshape(bd, sd, N_KV_HEADS, LANES)[..., :HEADS_PER_GROUP * 3]
    g5 = g5.reshape(bd, sd, N_KV_HEADS, HEADS_PER_GROUP, 3).transpose(0, 3, 2, 1, 4).reshape(bd, N_HEADS * sd, 3)
    gt = jnp.pad(g5, ((0, 0), (0, 0), (0, 5)))
    pad_new = lambda r: jnp.pad(r.reshape(bd, sd, KV_COLS), ((0, 0), (0, 16 - sd), (0, 0)))
    w_buf = cache_kv_win.shape[1]
    o_s = _nsa_sample(qbd, gt, kvc_s, cols_major(cache_kv_sel), page_table, pad_new(rs_s),
                      cols_major(cache_kv_win), pad_new(rw_s), dec=sd, past=PAST_LEN, ppc=16)
    o_s = o_s.reshape(bd, HEADS_PER_GROUP, N_KV_HEADS, sd, HEAD_DIM).transpose(0, 3, 2, 1, 4).reshape(n_s, d)
    h, lg = _out_proj(o_s, h, nsa_wo[0], lng(1, 0), lnb(1, 0), moe_wr[1], br(1))
    y_sample = moe(1, h, lg, tm_rows=64, tm_tok=n_s).reshape(bd, sd, d)
    kv_cmp_s = rc_s.reshape((bd, sd) + kv_shape)
    kv_sel_s = rs_s.reshape((bd, sd) + kv_shape)
    kv_win_s = jnp.concatenate([cache_kv_win, rw_s.reshape((bd, sd) + kv_shape)], axis=1)[:, -w_buf:]

    return (y_prompt, y_sample, conv_p[None], kv_cmp_p, kv_sel_p, kv_win_p,
            conv_s[None], kv_cmp_s, kv_sel_s, kv_win_s)
```

```python
import functools

import numpy as np
import jax
import jax.numpy as jnp
from jax import lax
from jax.experimental import pallas as pl
from jax.experimental.pallas import tpu as pltpu

F32 = jnp.float32
BF16 = jnp.bfloat16
HIGHEST = lax.Precision.HIGHEST

D_MODEL = 1024
DEPTH = 2
PAST_LEN = 8192
PAGE_SIZE = 128
N_HEADS = 16
N_KV_HEADS = 4
HEADS_PER_GROUP = N_HEADS // N_KV_HEADS
HEAD_DIM = D_MODEL // N_HEADS
ROT_DIM = HEAD_DIM // 4
ROPE_THETA = 500000.0
CMP_BLK = 32
CMP_STRIDE = 16
CMP_HID = 2 * HEAD_DIM
SEL_BLK = 64
SEL_TOPN = 16
WINDOW = 512
N_EXPERTS = 32
TOP_K = 4
D_FF = D_MODEL
SWIGLU_LIMIT = 7.0
SWIGLU_ALPHA = 1.702
ALPHA = (2 * DEPTH) ** 0.25
LN_EPS = 1e-5
NEG = -1e30
FORCE = 1e4

KV_COLS = 2 * N_KV_HEADS * HEAD_DIM
LANES = 128
VMEM_LIMIT = 56 * 2 ** 20


def _params(sem, vmem=VMEM_LIMIT):
    return pltpu.CompilerParams(dimension_semantics=sem, vmem_limit_bytes=vmem)


def _ln(x, g, b):
    mu = jnp.mean(x, axis=-1, keepdims=True)
    xc = x - mu
    var = jnp.mean(xc * xc, axis=-1, keepdims=True)
    return xc * lax.rsqrt(var + LN_EPS) * g + b


def _nt(a, b):
    return lax.dot_general(a, b, (((1,), (1,)), ((), ())), preferred_element_type=F32)


def _router(h, wr_ref, br_ref):
    return jnp.dot(h, wr_ref[...], precision=HIGHEST, preferred_element_type=F32) + br_ref[...]


def _full(shape):
    return pl.BlockSpec(shape, lambda *_: (0,) * len(shape))


def _conv_body(x_ref, p_ref, win_ref, wc_ref, wout_ref, lng_ref, lnb_ref, wr_ref, br_ref,
               h_ref, st_ref, lg_ref, carry_ref, *, seq, tm, rows_prev):
    i = pl.program_id(0)

    @pl.when(i == 0)
    def _():
        carry_ref[...] = jnp.zeros_like(carry_ref)

    x = x_ref[...]
    d = x.shape[1]
    z = jnp.dot(x.astype(BF16), win_ref[...], preferred_element_type=F32)
    bg, c, xh = z[:, :d], z[:, d:2 * d], z[:, 2 * d:]
    u = c * xh
    row = lax.broadcasted_iota(jnp.int32, (tm, 1), 0)
    t = (i * tm + row) & (seq - 1)
    um1 = pltpu.roll(u, 1, 0)
    um2 = pltpu.roll(u, 2, 0)
    c0 = carry_ref[0:1, :]
    c1 = carry_ref[1:2, :]
    um1 = jnp.where(row == 0, c1, um1)
    um2 = jnp.where(row == 0, c0, jnp.where(row == 1, c1, um2))
    if rows_prev:
        p0, p1 = p_ref[0], p_ref[1]
    else:
        p0, p1 = p_ref[0, 0:1, :], p_ref[0, 1:2, :]
    um1 = jnp.where(t >= 1, um1, p1)
    um2 = jnp.where(t >= 2, um2, jnp.where(t == 1, p1, p0))
    conv = wc_ref[0:1, :] * um2 + wc_ref[1:2, :] * um1 + wc_ref[2:3, :] * u
    y = jnp.dot((bg * conv).astype(BF16), wout_ref[...], preferred_element_type=F32)
    h = _ln(ALPHA * x + y, lng_ref[...], lnb_ref[...])
    h_ref[...] = h
    lg_ref[...] = _router(h, wr_ref, br_ref)
    carry_ref[0:2, :] = u[tm - 2:tm, :]
    if rows_prev:
        st_ref[...] = u
    else:
        st_ref[0] = u[tm - 2:tm, :]


def _conv_layer(x, prev, w_in, w_conv, w_out, lng, lnb, wr, br, *, seq, tm, rows_prev):
    n, d = x.shape
    assert n % tm == 0 and seq >= 2 and seq & (seq - 1) == 0
    assert (seq % tm == 0) if not rows_prev else (tm % seq == 0 and n == tm)
    if rows_prev:
        p_spec = pl.BlockSpec((2, tm, d), lambda i: (0, i, 0))
        st_shape = jax.ShapeDtypeStruct((n, d), F32)
        st_spec = pl.BlockSpec((tm, d), lambda i: (i, 0))
    else:
        per = seq // tm
        p_spec = pl.BlockSpec((1, 2, d), lambda i: (i // per, 0, 0))
        st_shape = jax.ShapeDtypeStruct((n // seq, 2, d), F32)
        st_spec = pl.BlockSpec((1, 2, d), lambda i: (i // per, 0, 0))
    return pl.pallas_call(
        functools.partial(_conv_body, seq=seq, tm=tm, rows_prev=rows_prev),
        grid=(n // tm,),
        in_specs=[pl.BlockSpec((tm, d), lambda i: (i, 0)), p_spec,
                  _full((d, 3 * d)), _full((3, d)), _full((d, d)), _full((1, d)), _full((1, d)),
                  _full((d, N_EXPERTS)), _full((1, N_EXPERTS))],
        out_specs=[pl.BlockSpec((tm, d), lambda i: (i, 0)), st_spec,
                   pl.BlockSpec((tm, N_EXPERTS), lambda i: (i, 0))],
        out_shape=[jax.ShapeDtypeStruct((n, d), F32), st_shape,
                   jax.ShapeDtypeStruct((n, N_EXPERTS), F32)],
        scratch_shapes=[pltpu.VMEM((8, d), F32)],
        compiler_params=_params(("arbitrary",)),
        name="conv_mixer",
    )(x, prev, w_in.astype(BF16), w_conv, w_out.astype(BF16), lng, lnb, wr, br)


def _moe_body(te_ref, nv_ref, nx_ref, x_ref, w1_hbm, b1_ref, w2_hbm, b2_ref, o_ref, w1b, w2b, w1s, w2s, sem,
              *, layer):
    i = pl.program_id(0)
    e = te_ref[i]
    prev = te_ref[jnp.maximum(i - 1, 0)]
    valid = i < nv_ref[0]

    def weight_copies(expert):
        return (pltpu.make_async_copy(w1_hbm.at[layer, expert], w1s, sem.at[0]),
                pltpu.make_async_copy(w2_hbm.at[layer, expert], w2s, sem.at[1]))

    @pl.when(i == 0)
    def _():
        for cp in weight_copies(e):
            cp.start()

    @pl.when(valid & ((i == 0) | (e != prev)))
    def _():
        for cp in weight_copies(e):
            cp.wait()
        w1b[...] = w1s[...].astype(BF16)
        w2b[...] = w2s[...].astype(BF16)
        nxt = nx_ref[i]

        @pl.when(nxt >= 0)
        def _():
            for cp in weight_copies(nxt):
                cp.start()

    @pl.when(valid)
    def _():
        hgu = jnp.dot(x_ref[...].astype(BF16), w1b[...], preferred_element_type=F32) + b1_ref[0, 0]
        g = jnp.minimum(hgu[:, :D_FF], SWIGLU_LIMIT)
        u = jnp.clip(hgu[:, D_FF:], -SWIGLU_LIMIT, SWIGLU_LIMIT)
        a = (u + 1.0) * (g * jax.nn.sigmoid(SWIGLU_ALPHA * g))
        o_ref[...] = jnp.dot(a.astype(BF16), w2b[...], preferred_element_type=F32) + b2_ref[0, 0]

    @pl.when(jnp.logical_not(valid))
    def _():
        o_ref[...] = jnp.zeros_like(o_ref)


def _moe_experts(xb, tile_e, n_valid, next_e, w1, b1, w2, b2, *, layer, tm):
    n_rows, d = xb.shape
    n_tiles = n_rows // tm
    bias = lambda w: pl.BlockSpec((1, 1, 1, w), lambda i, te, nv, nx: (layer, te[i], 0, 0))
    grid_spec = pltpu.PrefetchScalarGridSpec(
        num_scalar_prefetch=3,
        grid=(n_tiles,),
        in_specs=[pl.BlockSpec((tm, d), lambda i, te, nv, nx: (i, 0)),
                  pl.BlockSpec(memory_space=pl.ANY), bias(2 * D_FF),
                  pl.BlockSpec(memory_space=pl.ANY), bias(d)],
        out_specs=pl.BlockSpec((tm, d), lambda i, te, nv, nx: (i, 0)),
        scratch_shapes=[pltpu.VMEM((d, 2 * D_FF), BF16), pltpu.VMEM((D_FF, d), BF16),
                        pltpu.VMEM((d, 2 * D_FF), F32), pltpu.VMEM((D_FF, d), F32),
                        pltpu.SemaphoreType.DMA((2,))],
    )
    return pl.pallas_call(
        functools.partial(_moe_body, layer=layer), grid_spec=grid_spec,
        out_shape=jax.ShapeDtypeStruct((n_rows, d), F32),
        compiler_params=_params(("arbitrary",)),
        name="moe_experts",
    )(tile_e, n_valid, next_e, xb, w1, b1[:, :, None, :], w2, b2[:, :, None, :])


def _combine_body(h_ref, yg_ref, gate_ref, lng_ref, lnb_ref, o_ref):
    gate = gate_ref[...]
    f = gate[:, 0:1] * yg_ref[0]
    for k in range(1, TOP_K):
        f = f + gate[:, k:k + 1] * yg_ref[k]
    o_ref[...] = _ln(ALPHA * h_ref[...] + f, lng_ref[...], lnb_ref[...])


def _moe_combine(h, yg, gate, lng, lnb, *, tm):
    n, d = h.shape
    return pl.pallas_call(
        _combine_body, grid=(n // tm,),
        in_specs=[pl.BlockSpec((tm, d), lambda i: (i, 0)),
                  pl.BlockSpec((TOP_K, tm, d), lambda i: (0, i, 0)),
                  pl.BlockSpec((tm, TOP_K), lambda i: (i, 0)),
                  _full((1, d)), _full((1, d))],
        out_specs=pl.BlockSpec((tm, d), lambda i: (i, 0)),
        out_shape=jax.ShapeDtypeStruct((n, d), F32),
        compiler_params=_params(("arbitrary",)),
        name="moe_combine",
    )(h, yg, gate, lng, lnb)


def _moe_route(h, logits, *, tm_rows):
    n, d = h.shape
    top_v, top_e = lax.top_k(logits, TOP_K)
    gate = jax.nn.softmax(top_v, axis=-1)
    flat_e = top_e.reshape(-1)
    nk = n * TOP_K
    onehot = (flat_e[:, None] == jnp.arange(N_EXPERTS, dtype=jnp.int32)[None, :]).astype(jnp.int32)
    csum = jnp.cumsum(onehot, axis=0)
    rank = jnp.sum((csum - onehot) * onehot, axis=-1)
    counts = csum[-1]
    padded = (counts + tm_rows - 1) // tm_rows * tm_rows
    pad_end = jnp.cumsum(padded)
    pad_start = pad_end - padded
    dest = (pad_start[flat_e] + rank).astype(jnp.int32)
    n_tiles = (nk + N_EXPERTS * (tm_rows - 1) + tm_rows - 1) // tm_rows
    n_rows = n_tiles * tm_rows
    tile_start = jnp.arange(n_tiles, dtype=jnp.int32) * tm_rows
    tile_e = jnp.minimum(jnp.sum((pad_end[None, :] <= tile_start[:, None]).astype(jnp.int32), axis=1), N_EXPERTS - 1)
    n_valid = (pad_end[-1] // tm_rows).astype(jnp.int32).reshape(1)
    row_tok = jnp.zeros((n_rows,), jnp.int32).at[dest].set(
        jnp.arange(nk, dtype=jnp.int32) // TOP_K, unique_indices=True, mode='promise_in_bounds')
    xb = h[row_tok]
    eidx = jnp.arange(N_EXPERTS, dtype=jnp.int32)
    later = jnp.where((eidx[None, :] > eidx[:, None]) & (counts[None, :] > 0), eidx[None, :], N_EXPERTS)
    nxt_tab = jnp.min(later, axis=1)
    next_e = jnp.where(nxt_tab == N_EXPERTS, -1, nxt_tab).astype(jnp.int32)[tile_e]
    return xb, (tile_e, n_valid, next_e), dest.reshape(n, TOP_K).T, gate


def _moe_apply(h, routed, w1, b1, w2, b2, lng, lnb, *, layer, tm_rows, tm_tok):
    xb, tiles, dest_t, gate = routed
    yb = _moe_experts(xb, *tiles, w1, b1, w2, b2, layer=layer, tm=tm_rows)
    yg = yb[dest_t]
    return _moe_combine(h, yg, gate, lng, lnb, tm=tm_tok)


def _proj_body(h_ref, w_ref, rc_ref, rs1_ref, rs2_ref, *out_refs, heads_out, seq, q_scale):
    z = jnp.dot(h_ref[...].astype(BF16), w_ref[...], preferred_element_type=F32)
    cc, s1, s2 = rc_ref[...], rs1_ref[...], rs2_ref[...]

    def rope(x):
        return x * cc + pltpu.roll(x, LANES - ROT_DIM // 2, 1) * s1 + pltpu.roll(x, ROT_DIM // 2, 1) * s2

    def rope_cols(lo, hi):
        return [rope(z[:, c:c + LANES]) for c in range(lo, hi, LANES)]

    if heads_out:
        rows_c, rows_s, rows_w, g_ref, q_ref, ks_ref, vs_ref, kw_ref, vw_ref = out_refs
    else:
        rows_c, rows_s, rows_w, g_ref, q_ref = out_refs
    half = KV_COLS // 2
    for br, rows in enumerate((rows_c, rows_s, rows_w)):
        base = br * KV_COLS
        k = jnp.concatenate(rope_cols(base, base + half), axis=1)
        v = z[:, base + half:base + KV_COLS]
        kv = jnp.concatenate([k, v], axis=1)
        if heads_out:
            rows[0] = kv.T
        else:
            rows[...] = kv
        if heads_out and br >= 1:
            tm = k.shape[0]
            t = (pl.program_id(0) % (seq // tm)) * tm + lax.broadcasted_iota(jnp.int32, (tm, HEAD_DIM), 0)
            lane = lax.broadcasted_iota(jnp.int32, (tm, HEAD_DIM), 1)
            blk_onehot = (lane == lax.shift_right_logical(t, 6)).astype(BF16)
            one_lane = (lane == 0).astype(BF16)
            k_ref, v_ref = (ks_ref, vs_ref) if br == 1 else (kw_ref, vw_ref)
            for g in range(N_KV_HEADS):
                kg = k[:, g * HEAD_DIM:(g + 1) * HEAD_DIM].astype(BF16)
                vg = v[:, g * HEAD_DIM:(g + 1) * HEAD_DIM].astype(BF16)
                k_ref[0, g] = jnp.concatenate([kg, blk_onehot], axis=1) if br == 1 else kg
                v_ref[0, g] = jnp.concatenate([vg, one_lane], axis=1)
    qb = 3 * KV_COLS
    q = jnp.concatenate(rope_cols(qb, qb + D_MODEL), axis=1) * q_scale
    if heads_out:
        for hd in range(N_HEADS):
            q_ref[0, hd] = q[:, hd * HEAD_DIM:(hd + 1) * HEAD_DIM].astype(BF16)
    else:
        q_ref[...] = q
    g_ref[...] = jax.nn.sigmoid(z[:, qb + D_MODEL:])


def _rope_tables(pos):
    half = ROT_DIM // 2
    inv = ROPE_THETA ** (-jnp.arange(half, dtype=F32) * 2.0 / ROT_DIM)
    ang = pos.astype(F32)[:, None] * inv[None, :]
    cos, sin = jnp.cos(ang), jnp.sin(ang)
    zeros = jnp.zeros((pos.shape[0], HEAD_DIM - ROT_DIM), F32)
    ones = jnp.ones_like(zeros)
    z8 = jnp.zeros_like(sin)
    c = jnp.concatenate([cos, cos, ones], axis=1)
    s1 = jnp.concatenate([-sin, z8, zeros], axis=1)
    s2 = jnp.concatenate([z8, sin, zeros], axis=1)
    rep = LANES // HEAD_DIM
    return tuple(jnp.tile(a, (1, rep)) for a in (c, s1, s2))


def _proj_weights(kv_w, wq, wg):
    wg4 = wg.reshape(D_MODEL, N_KV_HEADS, HEADS_PER_GROUP * 3)
    wg4 = jnp.pad(wg4, ((0, 0), (0, 0), (0, LANES - HEADS_PER_GROUP * 3))).reshape(D_MODEL, N_KV_HEADS * LANES)
    return jnp.concatenate([kv_w, wq, wg4], axis=1).astype(BF16)


def _nsa_proj(h, w_cat, tables, *, tm, seq, heads_out):
    n, d = h.shape
    ncol = w_cat.shape[1]
    per = tables[0].shape[0] // tm
    tab_spec = pl.BlockSpec((tm, LANES), lambda i: (i % per, 0))
    row_spec = lambda w: pl.BlockSpec((tm, w), lambda i: (i, 0))
    g_spec, g_shape = row_spec(N_KV_HEADS * LANES), jax.ShapeDtypeStruct((n, N_KV_HEADS * LANES), F32)
    if heads_out:
        bsz = n // seq
        tps = seq // tm
        out_specs = [pl.BlockSpec((1, KV_COLS, tm), lambda i: (i // tps, 0, i % tps))] * 3 + [g_spec]
        out_shape = [jax.ShapeDtypeStruct((bsz, KV_COLS, seq), F32)] * 3 + [g_shape]
        assert seq // SEL_BLK <= HEAD_DIM
        hspec = lambda nh, w: pl.BlockSpec((1, nh, tm, w), lambda i: (i // tps, 0, i % tps, 0))
        hshape = lambda nh, w: jax.ShapeDtypeStruct((bsz, nh, seq, w), BF16)
        widths = (2 * HEAD_DIM, 2 * HEAD_DIM, HEAD_DIM, 2 * HEAD_DIM)
        out_specs += [hspec(N_HEADS, HEAD_DIM)] + [hspec(N_KV_HEADS, w) for w in widths]
        out_shape += [hshape(N_HEADS, HEAD_DIM)] + [hshape(N_KV_HEADS, w) for w in widths]
    else:
        out_specs = [row_spec(KV_COLS)] * 3 + [g_spec, row_spec(d)]
        out_shape = [jax.ShapeDtypeStruct((n, KV_COLS), F32)] * 3 + [g_shape, jax.ShapeDtypeStruct((n, d), F32)]
    q_scale = HEAD_DIM ** -0.5 * (float(np.log2(np.e)) if heads_out else 1.0)
    return pl.pallas_call(
        functools.partial(_proj_body, heads_out=heads_out, seq=seq, q_scale=q_scale), grid=(n // tm,),
        in_specs=[row_spec(d), _full((d, ncol)), tab_spec, tab_spec, tab_spec],
        out_specs=out_specs, out_shape=out_shape,
        compiler_params=_params(("arbitrary",)),
        name="nsa_proj",
    )(h, w_cat, *tables)


def _cmp_math(xt, w1_ref, pe_ref, w2_ref, o_ref, nch):
    k = pl.program_id(1) // 2
    x = jnp.concatenate(
        [xt[pl.ds(c, nch, stride=CMP_STRIDE), :].astype(BF16) for c in range(CMP_STRIDE)],
        axis=1)
    acc = jnp.dot(x, w1_ref[0], preferred_element_type=F32)
    pe = pe_ref[pl.ds(k, 1), :]
    hid = []
    for gs in range(2):
        pa = acc[:, gs * 2 * CMP_HID:gs * 2 * CMP_HID + CMP_HID]
        pb = acc[:, gs * 2 * CMP_HID + CMP_HID:(gs + 1) * 2 * CMP_HID]
        pb_next = pltpu.roll(pb, nch - 1, 0)
        hid.append(jax.nn.gelu(pa + pb_next + pe))
    hid = jnp.concatenate(hid, axis=1).astype(BF16)
    o_ref[0] = jnp.dot(hid, w2_ref[0], preferred_element_type=F32)


def _cmp_t_body(r_ref, w1_ref, pe_ref, w2_ref, o_ref, xt, *, nch):
    for p in range(r_ref.shape[2] // LANES):
        xt[p * LANES:(p + 1) * LANES, :] = r_ref[0, :, p * LANES:(p + 1) * LANES].T
    _cmp_math(xt, w1_ref, pe_ref, w2_ref, o_ref, nch)


def _cmp_paged_body(pt_ref, cache_ref, w1_ref, pe_ref, w2_ref, o_ref, pbuf, xt, sem, *, nch, n_pages):
    ncol = pl.num_programs(1)
    step = pl.program_id(0) * ncol + pl.program_id(1)
    nsteps = pl.num_programs(0) * ncol
    slot = step & 1

    def page_copy(st, p, sl):
        bb = st // ncol
        col = pl.multiple_of((st - bb * ncol) * LANES, LANES)
        return pltpu.make_async_copy(cache_ref.at[pt_ref[bb * n_pages + p], pl.ds(col, LANES), :],
                                     pbuf.at[sl, p], sem.at[sl])

    @pl.when(step == 0)
    def _():
        for p in range(n_pages):
            page_copy(step, p, slot).start()

    @pl.when(step + 1 < nsteps)
    def _():
        for p in range(n_pages):
            page_copy(step + 1, p, 1 - slot).start()

    for p in range(n_pages):
        page_copy(step, p, slot).wait()
    for p in range(n_pages):
        xt[p * PAGE_SIZE:(p + 1) * PAGE_SIZE, :] = pbuf[slot, p].T
    _cmp_math(xt, w1_ref, pe_ref, w2_ref, o_ref, nch)


def _cmp_weights(cmp_pe, cmp_w1, cmp_b1, cmp_w2):
    eye = jnp.eye(2, dtype=F32)
    w1r = cmp_w1.reshape(2, 2, CMP_STRIDE, HEAD_DIM, CMP_HID)
    w1bd = jnp.einsum('ab,kncdh->kcadbnh', eye, w1r).reshape(2, CMP_STRIDE * 2 * HEAD_DIM, 4 * CMP_HID)
    w2bd = jnp.einsum('ab,khd->kahbd', eye, cmp_w2).reshape(2, 2 * CMP_HID, 2 * HEAD_DIM)
    pe_term = jnp.einsum('kcd,kcdh->kh', cmp_pe, cmp_w1, precision=HIGHEST) + cmp_b1
    return w1bd.astype(BF16), pe_term, w2bd.astype(BF16)


def _cmp_specs(cw, nch, imap):
    w1bd, pe_term, w2bd = cw
    return ([pl.BlockSpec((1,) + w1bd.shape[1:], imap(lambda b, j: (j // 2, 0, 0))),
             pl.BlockSpec(pe_term.shape, imap(lambda b, j: (0, 0))),
             pl.BlockSpec((1,) + w2bd.shape[1:], imap(lambda b, j: (j // 2, 0, 0)))],
            pl.BlockSpec((1, nch, LANES), imap(lambda b, j: (b, 0, j))))


def _compress_t(rows_t, cw):
    bsz, _, t = rows_t.shape
    nch = t // CMP_STRIDE
    w_specs, o_spec = _cmp_specs(cw, nch, lambda f: f)
    return pl.pallas_call(
        functools.partial(_cmp_t_body, nch=nch), grid=(bsz, KV_COLS // LANES),
        in_specs=[pl.BlockSpec((1, LANES, t), lambda b, j: (b, j, 0))] + w_specs,
        out_specs=o_spec,
        out_shape=jax.ShapeDtypeStruct((bsz, nch, KV_COLS), F32),
        scratch_shapes=[pltpu.VMEM((t, LANES), F32)],
        compiler_params=_params(("arbitrary", "arbitrary")),
        name="compress_kv",
    )(rows_t, *cw)


def _compress_paged(cache_t, page_table, cw):
    bsz, n_pages = page_table.shape
    nch = n_pages * PAGE_SIZE // CMP_STRIDE
    w_specs, o_spec = _cmp_specs(cw, nch, lambda f: (lambda b, j, pt: f(b, j)))
    grid_spec = pltpu.PrefetchScalarGridSpec(
        num_scalar_prefetch=1, grid=(bsz, KV_COLS // LANES),
        in_specs=[pl.BlockSpec(memory_space=pl.ANY)] + w_specs,
        out_specs=o_spec,
        scratch_shapes=[pltpu.VMEM((2, n_pages, LANES, PAGE_SIZE), F32),
                        pltpu.VMEM((n_pages * PAGE_SIZE, LANES), F32),
                        pltpu.SemaphoreType.DMA((2,))])
    return pl.pallas_call(
        functools.partial(_cmp_paged_body, nch=nch, n_pages=n_pages), grid_spec=grid_spec,
        out_shape=jax.ShapeDtypeStruct((bsz, nch, KV_COLS), F32),
        compiler_params=_params(("arbitrary", "arbitrary")),
        name="compress_kv_paged",
    )(page_table.reshape(-1), cache_t, *cw)


def _softmax_probs(s, m, exp_fn=jnp.exp):
    s = jnp.where(m, s, NEG)
    mx = jnp.max(s, axis=-1, keepdims=True)
    e = jnp.where(m, exp_fn(s - mx), 0.0)
    den = jnp.sum(e, axis=-1, keepdims=True)
    return e / jnp.maximum(den, 1e-30)


def _select_blocks(imp_t, qpos_row, n_rows):
    lanes = imp_t.shape[1]
    j = lax.broadcasted_iota(jnp.int32, (n_rows, lanes), 0)
    cur = lax.shift_right_logical(qpos_row, 6)
    valid = j <= cur
    forced = (j == 0) | (j == cur) | (j == cur - 1)
    sc = jnp.where(valid, jnp.where(forced, FORCE, imp_t[:n_rows]), NEG)
    return sc, j, valid


def _overlap_t(n_sel_pad, n_cmp_pad):
    c0 = np.arange(n_cmp_pad)[None, :] * CMP_STRIDE
    s0 = np.arange(n_sel_pad)[:, None] * SEL_BLK
    ov = np.clip(np.minimum(c0 + CMP_BLK, s0 + SEL_BLK) - np.maximum(c0, s0), 0, None).astype(np.float32) / CMP_BLK
    return ov


def _attn_p_body(q_ref, kck_ref, kcv_ref, ks_ref, vs_ref, kw_ref, vw_ref, g_ref, h_ref,
                 ovt_ref, wo_ref, lng_ref, lnb_ref, wr_ref, br_ref,
                 o_ref, lg_ref, o_scr, *, tq, seq, kc):
    i = pl.program_id(1)
    g = pl.program_id(2)
    t0 = i * tq
    rows = HEADS_PER_GROUP * tq
    n_sel = seq // SEL_BLK
    q = q_ref[0].reshape(rows, HEAD_DIM)
    tpos = t0 + (lax.broadcasted_iota(jnp.int32, (rows, 1), 0) & (tq - 1))

    n_cmp = kck_ref.shape[2]
    s = _nt(q, kck_ref[0, 0])
    cend = lax.broadcasted_iota(jnp.int32, (1, n_cmp), 1) * CMP_STRIDE + (CMP_BLK - 1)
    p_c = _softmax_probs(s, cend <= tpos, jnp.exp2)
    o_c = jnp.dot(p_c.astype(BF16), kcv_ref[0, 0], preferred_element_type=F32)
    psum = p_c[0:tq]
    for qh in range(1, HEADS_PER_GROUP):
        psum = psum + p_c[qh * tq:(qh + 1) * tq]
    imp_t = lax.dot_general(ovt_ref[...], psum, (((1,), (1,)), ((), ())),
                            precision=HIGHEST, preferred_element_type=F32)

    qrow = t0 + lax.broadcasted_iota(jnp.int32, (1, tq), 1)
    sc, j, valid = _select_blocks(imp_t, qrow, n_sel)
    rank = jnp.zeros((n_sel, tq), jnp.int32)
    for jp in range(n_sel):
        r = sc[jp:jp + 1, :]
        beats = (r > sc) | ((r == sc) & (j > jp))
        rank = rank + beats.astype(jnp.int32)
    sel_t = (valid & (rank < SEL_TOPN)).astype(F32)
    sel_t = jnp.concatenate([sel_t, jnp.zeros((LANES - n_sel, tq), F32)], axis=0)
    sel = sel_t.T[:, 0:HEAD_DIM]
    blk_lane = lax.broadcasted_iota(jnp.int32, (1, HEAD_DIM), 1)
    r_i = lax.broadcasted_iota(jnp.int32, (tq, 1), 0)
    tq_pos = t0 + r_i
    t0a = pl.multiple_of(t0, tq)
    causal = jnp.where(lax.broadcasted_iota(jnp.int32, (1, tq), 1) <= r_i, 0.0, NEG)
    causal = jnp.concatenate([causal] * HEADS_PER_GROUP, axis=0)

    s = _nt(q, ks_ref[0, 0, pl.ds(t0a, tq), 0:HEAD_DIM]) + causal
    m0 = jnp.max(s, axis=-1, keepdims=True)
    acc0 = jnp.dot(jnp.exp2(s - m0).astype(BF16), vs_ref[0, 0, pl.ds(t0a, tq), :], preferred_element_type=F32)
    sweep = jnp.where((sel > 0.5) & (blk_lane < lax.shift_right_logical(t0, 6)), 0.0, NEG).astype(BF16)
    q_aug = jnp.concatenate([q, jnp.concatenate([sweep] * HEADS_PER_GROUP, axis=0)], axis=1)

    def sweep_chunk(c, carry):
        m_run, acc = carry
        k0 = pl.multiple_of(c * kc, kc)
        s = _nt(q_aug, ks_ref[0, 0, pl.ds(k0, kc), :])
        m_new = jnp.maximum(m_run, jnp.max(s, axis=-1, keepdims=True))
        pv = jnp.dot(jnp.exp2(s - m_new).astype(BF16), vs_ref[0, 0, pl.ds(k0, kc), :],
                     preferred_element_type=F32)
        return m_new, jnp.exp2(m_run - m_new) * acc + pv

    _, acc = lax.fori_loop(0, (t0 + kc - 1) // kc, sweep_chunk, (m0, acc0))
    o_s = acc[:, 0:HEAD_DIM] / acc[:, HEAD_DIM:HEAD_DIM + 1]

    ws = pl.multiple_of(jnp.maximum(t0 - WINDOW, 0), tq)
    wi = ws + lax.broadcasted_iota(jnp.int32, (1, WINDOW), 1)
    wbias = jnp.where((wi > tq_pos - WINDOW) & (wi < t0), 0.0, NEG)
    s_d = _nt(q, kw_ref[0, 0, pl.ds(t0a, tq), :]) + causal
    s_w = _nt(q, kw_ref[0, 0, pl.ds(ws, WINDOW), :]) + jnp.concatenate([wbias] * HEADS_PER_GROUP, axis=0)
    m = jnp.maximum(jnp.max(s_d, axis=-1, keepdims=True), jnp.max(s_w, axis=-1, keepdims=True))
    acc = (jnp.dot(jnp.exp2(s_d - m).astype(BF16), vw_ref[0, 0, pl.ds(t0a, tq), :], preferred_element_type=F32)
           + jnp.dot(jnp.exp2(s_w - m).astype(BF16), vw_ref[0, 0, pl.ds(ws, WINDOW), :],
                     preferred_element_type=F32))
    o_w = acc[:, 0:HEAD_DIM] / acc[:, HEAD_DIM:HEAD_DIM + 1]

    gates = g_ref[...]
    for qh in range(HEADS_PER_GROUP):
        sl = slice(qh * tq, (qh + 1) * tq)
        o_h = (gates[:, 3 * qh:3 * qh + 1] * o_c[sl] + gates[:, 3 * qh + 1:3 * qh + 2] * o_s[sl]
               + gates[:, 3 * qh + 2:3 * qh + 3] * o_w[sl])
        o_scr[g, :, qh * HEAD_DIM:(qh + 1) * HEAD_DIM] = o_h

    @pl.when(g == N_KV_HEADS - 1)
    def _():
        o = jnp.concatenate([o_scr[gg] for gg in range(N_KV_HEADS)], axis=1).astype(BF16)
        y = jnp.dot(o, wo_ref[...], preferred_element_type=F32)
        h = _ln(ALPHA * h_ref[...] + y, lng_ref[...], lnb_ref[...])
        o_ref[...] = h
        lg_ref[...] = _router(h, wr_ref, br_ref)


def _nsa_prompt(h, qh, kck, kcv, ks, vs, kw, vw, gates, wo, lng, lnb, wr, br, *, tq, kc):
    n, d = h.shape
    bsz, _, seq, _ = ks.shape
    nt = seq // tq
    n_cmp = kck.shape[2]
    ovt = jnp.asarray(_overlap_t(LANES, n_cmp))
    assert kc % tq == 0 and seq % kc == 0 and SEL_BLK % 64 == 0 and tq % SEL_BLK == 0
    grp = lambda a: pl.BlockSpec((1, 1) + a.shape[2:], lambda b, i, g: (b, g, 0, 0))
    row = lambda w: pl.BlockSpec((tq, w), lambda b, i, g: (b * nt + i, 0))
    return pl.pallas_call(
        functools.partial(_attn_p_body, tq=tq, seq=seq, kc=kc),
        grid=(bsz, nt, N_KV_HEADS),
        in_specs=[pl.BlockSpec((1, HEADS_PER_GROUP, tq, HEAD_DIM), lambda b, i, g: (b, g, i, 0)),
                  grp(kck), grp(kcv), grp(ks), grp(vs), grp(kw), grp(vw),
                  pl.BlockSpec((tq, LANES), lambda b, i, g: (b * nt + i, g)), row(d),
                  _full(ovt.shape), _full((d, d)), _full((1, d)), _full((1, d)),
                  _full((d, N_EXPERTS)), _full((1, N_EXPERTS))],
        out_specs=[row(d), row(N_EXPERTS)],
        out_shape=[jax.ShapeDtypeStruct((n, d), F32), jax.ShapeDtypeStruct((n, N_EXPERTS), F32)],
        scratch_shapes=[pltpu.VMEM((N_KV_HEADS, tq, HEADS_PER_GROUP * HEAD_DIM), F32)],
        compiler_params=_params(("arbitrary", "arbitrary", "arbitrary")),
        name="nsa_prompt",
    )(qh, kck, kcv, ks, vs, kw, vw, gates, h, ovt, wo.astype(BF16), lng, lnb, wr, br)


def _attn_s_body(pt_ref, q_ref, gt_ref, kvc_ref, cache_ref, ns_ref, cw_ref, nw_ref, ovt_ref, exp_ref,
                 o_ref, sc_ref, kvbuf, sem, msk_scr, m_scr, l_scr, acc_scr, ocw_scr, *, dec, past, ppc):
    nrow = q_ref.shape[1]
    half = KV_COLS // 2
    c = pl.program_id(1)
    nc = pl.num_programs(1)
    step = pl.program_id(0) * nc + c
    nsteps = pl.num_programs(0) * nc
    slot = step & 1

    def chunk_copies(st, sl):
        return [pltpu.make_async_copy(cache_ref.at[pt_ref[st * ppc + p]],
                                      kvbuf.at[sl, :, pl.ds(p * PAGE_SIZE, PAGE_SIZE)], sem.at[sl])
                for p in range(ppc)]

    @pl.when(step == 0)
    def _():
        for cp in chunk_copies(step, slot):
            cp.start()

    @pl.when(step + 1 < nsteps)
    def _():
        for cp in chunk_copies(step + 1, 1 - slot):
            cp.start()

    q = q_ref[0]
    row = lax.broadcasted_iota(jnp.int32, (nrow, 1), 0)
    tok = row & (dec - 1)
    gsel = lax.shift_right_logical(row, 2) & (N_KV_HEADS - 1)

    def pick(o):
        out = jnp.zeros((nrow, HEAD_DIM), F32)
        for gg in range(N_KV_HEADS):
            out = out + jnp.where(gsel == gg, o[:, gg * HEAD_DIM:(gg + 1) * HEAD_DIM], 0.0)
        return out

    def new_rows(new_ref):
        k_new = new_ref[0, :, 0:half].astype(BF16)
        m_new = lax.broadcasted_iota(jnp.int32, (1, new_ref.shape[1]), 1) <= tok
        return jnp.where(m_new, _nt(q, k_new), NEG), m_new, new_ref[0, :, half:KV_COLS].astype(BF16)

    @pl.when(c == 0)
    def _():
        qpos = past + tok
        n_cmp = kvc_ref.shape[1]
        kc = kvc_ref[0, :, 0:half].astype(BF16)
        vc = kvc_ref[0, :, half:KV_COLS].astype(BF16)
        cend = lax.broadcasted_iota(jnp.int32, (1, n_cmp), 1) * CMP_STRIDE + (CMP_BLK - 1)
        p_c = _softmax_probs(_nt(q, kc), cend <= qpos)
        ocw_scr[0] = pick(jnp.dot(p_c.astype(BF16), vc, preferred_element_type=F32))
        ngt = N_KV_HEADS * dec
        psum = p_c[0:ngt]
        for qh in range(1, HEADS_PER_GROUP):
            psum = psum + p_c[qh * ngt:(qh + 1) * ngt]
        psum = jnp.concatenate([psum, jnp.zeros((LANES - ngt, n_cmp), F32)], axis=0)
        imp_t = lax.dot_general(ovt_ref[...], psum, (((1,), (1,)), ((), ())),
                                precision=HIGHEST, preferred_element_type=F32)

        n_sel = past // SEL_BLK + 1
        n_sel8 = sc_ref.shape[0]
        lane_tok = lax.broadcasted_iota(jnp.int32, (1, LANES), 1) & (dec - 1)
        sc, j, valid = _select_blocks(imp_t, past + lane_tok, n_sel8)
        sc_ref[...] = sc

        def rank_step(jp, rank):
            r = sc_ref[pl.ds(jp, 1), :]
            beats = (r > sc) | ((r == sc) & (j > jp))
            return rank + beats.astype(jnp.int32)

        rank = lax.fori_loop(0, n_sel, rank_step, jnp.zeros((n_sel8, LANES), jnp.int32))
        sel_t = (valid & (rank < SEL_TOPN)).astype(F32)
        sel_t = jnp.concatenate([sel_t, jnp.zeros((ovt_ref.shape[0] - n_sel8, LANES), F32)], axis=0)
        sel = sel_t.T[0:ngt, 0:LANES]
        sel = jnp.concatenate([sel] * HEADS_PER_GROUP, axis=0)
        msk_scr[...] = jnp.where(sel > 0.5, 0.0, NEG).astype(BF16)

        s_n, m_n, v_n = new_rows(ns_ref)
        m0 = jnp.max(s_n, axis=-1, keepdims=True)
        e_n = jnp.where(m_n, jnp.exp(s_n - m0), 0.0)
        m_scr[...] = m0
        l_scr[...] = jnp.sum(e_n, axis=-1, keepdims=True)
        acc_scr[...] = jnp.dot(e_n.astype(BF16), v_n, preferred_element_type=F32)

        n_win = cw_ref.shape[2]
        m_w = lax.broadcasted_iota(jnp.int32, (1, n_win), 1) > tok + (n_win - WINDOW)
        s_o = jnp.where(m_w, jnp.dot(q, cw_ref[0, 0:half, :].astype(BF16), preferred_element_type=F32), NEG)
        s_n, m_n, v_n = new_rows(nw_ref)
        mx = jnp.maximum(jnp.max(s_o, axis=-1, keepdims=True), jnp.max(s_n, axis=-1, keepdims=True))
        e_o = jnp.where(m_w, jnp.exp(s_o - mx), 0.0)
        e_n = jnp.where(m_n, jnp.exp(s_n - mx), 0.0)
        den = jnp.sum(e_o, axis=-1, keepdims=True) + jnp.sum(e_n, axis=-1, keepdims=True)
        o_w = (_nt(e_o.astype(BF16), cw_ref[0, half:KV_COLS, :].astype(BF16))
               + jnp.dot(e_n.astype(BF16), v_n, preferred_element_type=F32))
        ocw_scr[1] = pick(o_w / den)

    for cp in chunk_copies(step, slot):
        cp.wait()
    kt = kvbuf[slot, 0:half, :].astype(BF16)
    vt = kvbuf[slot, half:KV_COLS, :].astype(BF16)
    s = (jnp.dot(q, kt, preferred_element_type=F32)
         + jnp.dot(msk_scr[...], exp_ref[c], preferred_element_type=F32))
    m_old = m_scr[...]
    m_new = jnp.maximum(m_old, jnp.max(s, axis=-1, keepdims=True))
    alpha = jnp.exp(m_old - m_new)
    e = jnp.exp(s - m_new)
    m_scr[...] = m_new
    l_scr[...] = alpha * l_scr[...] + jnp.sum(e, axis=-1, keepdims=True)
    acc_scr[...] = alpha * acc_scr[...] + _nt(e.astype(BF16), vt)

    @pl.when(c == nc - 1)
    def _():
        gt = gt_ref[0]
        o_s = pick(acc_scr[...] / l_scr[...])
        o_ref[0] = gt[:, 0:1] * ocw_scr[0] + gt[:, 1:2] * o_s + gt[:, 2:3] * ocw_scr[1]


def _nsa_sample(qbd, gt, kvc, cache_t, page_table, new_s, cache_w_t, new_w, *, dec, past, ppc):
    bsz, nrow, _ = qbd.shape
    n_pages = past // PAGE_SIZE
    assert n_pages % ppc == 0 and past // SEL_BLK <= LANES
    nc = n_pages // ppc
    ck = ppc * PAGE_SIZE
    ovt = jnp.asarray(_overlap_t(2 * LANES, kvc.shape[1]))
    kblk = (np.arange(past) // SEL_BLK).reshape(nc, 1, ck)
    expand = jnp.asarray(np.arange(LANES)[None, :, None] == kblk, BF16)
    n_sel8 = (past // SEL_BLK + 1 + 7) // 8 * 8
    b3 = lambda a: pl.BlockSpec((1,) + a.shape[1:], lambda b, c, pt: (b, 0, 0))
    full = lambda a: pl.BlockSpec(a.shape, lambda b, c, pt: (0,) * a.ndim)
    grid_spec = pltpu.PrefetchScalarGridSpec(
        num_scalar_prefetch=1, grid=(bsz, nc),
        in_specs=[b3(qbd), b3(gt), b3(kvc), pl.BlockSpec(memory_space=pl.ANY), b3(new_s), b3(cache_w_t),
                  b3(new_w), full(ovt), full(expand)],
        out_specs=pl.BlockSpec((1, nrow, HEAD_DIM), lambda b, c, pt: (b, 0, 0)),
        scratch_shapes=[pltpu.VMEM((n_sel8, LANES), F32),
                        pltpu.VMEM((2, KV_COLS, ck), F32),
                        pltpu.SemaphoreType.DMA((2,)),
                        pltpu.VMEM((nrow, LANES), BF16),
                        pltpu.VMEM((nrow, 1), F32), pltpu.VMEM((nrow, 1), F32),
                        pltpu.VMEM((nrow, KV_COLS // 2), F32),
                        pltpu.VMEM((2, nrow, HEAD_DIM), F32)])
    return pl.pallas_call(
        functools.partial(_attn_s_body, dec=dec, past=past, ppc=ppc), grid_spec=grid_spec,
        out_shape=jax.ShapeDtypeStruct((bsz, nrow, HEAD_DIM), F32),
        compiler_params=_params(("arbitrary", "arbitrary")),
        name="nsa_sample",
    )(page_table.reshape(-1), qbd, gt, kvc, cache_t, new_s, cache_w_t, new_w, ovt, expand)


def _out_body(o_ref, h_ref, wo_ref, lng_ref, lnb_ref, wr_ref, br_ref, y_ref, lg_ref):
    y = jnp.dot(o_ref[...].astype(BF16), wo_ref[...], preferred_element_type=F32)
    h = _ln(ALPHA * h_ref[...] + y, lng_ref[...], lnb_ref[...])
    y_ref[...] = h
    lg_ref[...] = _router(h, wr_ref, br_ref)


def _out_proj(o, h, wo, lng, lnb, wr, br):
    n, d = h.shape
    return pl.pallas_call(
        _out_body, grid=(1,),
        in_specs=[_full((n, d)), _full((n, d)), _full((d, d)), _full((1, d)), _full((1, d)),
                  _full((d, N_EXPERTS)), _full((1, N_EXPERTS))],
        out_specs=[_full((n, d)), _full((n, N_EXPERTS))],
        out_shape=[jax.ShapeDtypeStruct((n, d), F32), jax.ShapeDtypeStruct((n, N_EXPERTS), F32)],
        compiler_params=_params(("arbitrary",)),
        name="nsa_out_proj",
    )(o, h, wo.astype(BF16), lng, lnb, wr, br)


def kernel(x_prompt, x_sample, state_conv, cache_kv_cmp, cache_kv_sel, cache_kv_win, page_table,
           conv_w_in, conv_w, conv_w_out, kv_w, cmp_pe, cmp_w1, cmp_b1, cmp_w2,
           nsa_wq, nsa_wg, nsa_wo, moe_wr, moe_br, moe_w1, moe_b1, moe_w2, moe_b2, ln_g, ln_b):
    bp, sp, d = x_prompt.shape
    bd, sd, _ = x_sample.shape
    kv_shape = (2, N_KV_HEADS, HEAD_DIM)
    lng = lambda l, s: ln_g[l, s][None, :]
    lnb = lambda l, s: ln_b[l, s][None, :]
    moe = lambda l, h, routed, **kw: _moe_apply(h, routed, moe_w1, moe_b1, moe_w2, moe_b2,
                                                lng(l, 1), lnb(l, 1), layer=l, **kw)
    w_cat = _proj_weights(kv_w, nsa_wq[0], nsa_wg[0])
    cw = _cmp_weights(cmp_pe, cmp_w1, cmp_b1, cmp_w2)
    br = lambda l: moe_br[l][None, :]

    n_p = bp * sp
    xp = x_prompt.reshape(n_p, d)
    h, conv_p, lg = _conv_layer(xp, jnp.zeros((bp, 2, d), F32), conv_w_in[0], conv_w[0], conv_w_out[0],
                                lng(0, 0), lnb(0, 0), moe_wr[0], br(0), seq=sp, tm=256, rows_prev=False)
    routed_p = _moe_route(h, lg, tm_rows=256)
    n_s = bd * sd
    xs = x_sample.reshape(n_s, d)
    prev_rows = jnp.repeat(state_conv[0].transpose(1, 0, 2), sd, axis=1)
    hs, u_s, lgs = _conv_layer(xs, prev_rows, conv_w_in[0], conv_w[0], conv_w_out[0],
                               lng(0, 0), lnb(0, 0), moe_wr[0], br(0), seq=sd, tm=n_s, rows_prev=True)
    conv_s = u_s.reshape(bd, sd, d)[:, sd - 2:]
    routed_s = _moe_route(hs, lgs, tm_rows=64)
    h = moe(0, h, routed_p, tm_rows=256, tm_tok=256)
    hs = moe(0, hs, routed_s, tm_rows=64, tm_tok=n_s)

    pos_s = PAST_LEN + (jnp.arange(n_s, dtype=jnp.int32) % sd)
    rc_s, rs_s, rw_s, gates_s, q_s = _nsa_proj(hs, w_cat, _rope_tables(pos_s), tm=n_s, seq=sd, heads_out=False)
    cols_major = lambda a: a.transpose(0, 2, 3, 4, 1).reshape(a.shape[0], KV_COLS, a.shape[1])
    kvc_s = _compress_paged(cols_major(cache_kv_cmp), page_table, cw)
    q5 = q_s.reshape(bd, sd, N_KV_HEADS, HEADS_PER_GROUP, HEAD_DIM).transpose(0, 3, 2, 1, 4)
    eye = jnp.eye(N_KV_HEADS, dtype=F32)
    qbd = jnp.einsum('bqgtd,gh->bqgthd', q5, eye).reshape(bd, N_HEADS * sd, N_KV_HEADS * HEAD_DIM).astype(BF16)
    g5 = gates_s.reshape(bd, sd, N_KV_HEADS, LANES)[..., :HEADS_PER_GROUP * 3]
    g5 = g5.reshape(bd, sd, N_KV_HEADS, HEADS_PER_GROUP, 3).transpose(0, 3, 2, 1, 4).reshape(bd, N_HEADS * sd, 3)
    gt = jnp.pad(g5, ((0, 0), (0, 0), (0, 5)))
    pad_new = lambda r: jnp.pad(r.reshape(bd, sd, KV_COLS), ((0, 0), (0, 16 - sd), (0, 0)))
    w_buf = cache_kv_win.shape[1]
    o_s = _nsa_sample(qbd, gt, kvc_s, cols_major(cache_kv_sel), page_table, pad_new(rs_s),
                      cols_major(cache_kv_win), pad_new(rw_s), dec=sd, past=PAST_LEN, ppc=16)
    o_s = o_s.reshape(bd, HEADS_PER_GROUP, N_KV_HEADS, sd, HEAD_DIM).transpose(0, 3, 2, 1, 4).reshape(n_s, d)
    hs, lgs = _out_proj(o_s, hs, nsa_wo[0], lng(1, 0), lnb(1, 0), moe_wr[1], br(1))
    routed_s = _moe_route(hs, lgs, tm_rows=64)

    tabs = _rope_tables(jnp.arange(sp, dtype=jnp.int32))
    rc, rs, rw, gates, qh, ks, vs, kw, vw = _nsa_proj(h, w_cat, tabs, tm=256, seq=sp, heads_out=True)
    kvc = _compress_t(rc, cw)
    kvc_h = kvc.reshape(bp, -1, 2, N_KV_HEADS, HEAD_DIM).transpose(2, 0, 3, 1, 4).astype(BF16)
    h, lg = _nsa_prompt(h, qh, kvc_h[0], kvc_h[1], ks, vs, kw, vw, gates, nsa_wo[0],
                        lng(1, 0), lnb(1, 0), moe_wr[1], br(1), tq=256, kc=512)
    routed_p = _moe_route(h, lg, tm_rows=256)
    y_sample = moe(1, hs, routed_s, tm_rows=64, tm_tok=n_s).reshape(bd, sd, d)
    y_prompt = moe(1, h, routed_p, tm_rows=256, tm_tok=256).reshape(bp, sp, d)
    rows_out = lambda r_t: r_t.reshape((bp,) + kv_shape + (r_t.shape[-1],)).transpose(0, 4, 1, 2, 3)
    kv_cmp_p = rows_out(rc)
    kv_sel_p = rows_out(rs)
    kv_win_p = rows_out(rw[:, :, sp - min(WINDOW, sp):])

    kv_cmp_s = rc_s.reshape((bd, sd) + kv_shape)
    kv_sel_s = rs_s.reshape((bd, sd) + kv_shape)
    kv_win_s = jnp.concatenate([cache_kv_win, rw_s.reshape((bd, sd) + kv_shape)], axis=1)[:, -w_buf:]

    return (y_prompt, y_sample, conv_p[None], kv_cmp_p, kv_sel_p, kv_win_p,
            conv_s[None], kv_cmp_s, kv_sel_s, kv_win_s)
```

```python
import functools

import numpy as np
import jax
import jax.numpy as jnp
from jax import lax
from jax.experimental import pallas as pl
from jax.experimental.pallas import tpu as pltpu

F32 = jnp.float32
BF16 = jnp.bfloat16
HIGHEST = lax.Precision.HIGHEST

D_MODEL = 1024
DEPTH = 2
PAST_LEN = 8192
PAGE_SIZE = 128
N_HEADS = 16
N_KV_HEADS = 4
HEADS_PER_GROUP = N_HEADS // N_KV_HEADS
HEAD_DIM = D_MODEL // N_HEADS
ROT_DIM = HEAD_DIM // 4
ROPE_THETA = 500000.0
CMP_BLK = 32
CMP_STRIDE = 16
CMP_HID = 2 * HEAD_DIM
SEL_BLK = 64
SEL_TOPN = 16
WINDOW = 512
N_EXPERTS = 32
TOP_K = 4
D_FF = D_MODEL
SWIGLU_LIMIT = 7.0
SWIGLU_ALPHA = 1.702
ALPHA = (2 * DEPTH) ** 0.25
LN_EPS = 1e-5
NEG = -1e30
FORCE = 1e4

KV_COLS = 2 * N_KV_HEADS * HEAD_DIM
LANES = 128
VMEM_LIMIT = 56 * 2 ** 20


def _params(sem, vmem=VMEM_LIMIT):
    return pltpu.CompilerParams(dimension_semantics=sem, vmem_limit_bytes=vmem)


def _ln(x, g, b):
    mu = jnp.mean(x, axis=-1, keepdims=True)
    xc = x - mu
    var = jnp.mean(xc * xc, axis=-1, keepdims=True)
    return xc * lax.rsqrt(var + LN_EPS) * g + b


def _nt(a, b):
    return lax.dot_general(a, b, (((1,), (1,)), ((), ())), preferred_element_type=F32)


def _router(h, wr_ref, br_ref):
    return jnp.dot(h, wr_ref[...], precision=HIGHEST, preferred_element_type=F32) + br_ref[...]


def _full(shape):
    return pl.BlockSpec(shape, lambda *_: (0,) * len(shape))


def _conv_body(x_ref, p_ref, win_ref, wc_ref, wout_ref, lng_ref, lnb_ref, wr_ref, br_ref,
               h_ref, st_ref, lg_ref, carry_ref, *, seq, tm, rows_prev):
    i = pl.program_id(0)

    @pl.when(i == 0)
    def _():
        carry_ref[...] = jnp.zeros_like(carry_ref)

    x = x_ref[...]
    d = x.shape[1]
    z = jnp.dot(x.astype(BF16), win_ref[...], preferred_element_type=F32)
    bg, c, xh = z[:, :d], z[:, d:2 * d], z[:, 2 * d:]
    u = c * xh
    row = lax.broadcasted_iota(jnp.int32, (tm, 1), 0)
    t = (i * tm + row) & (seq - 1)
    um1 = pltpu.roll(u, 1, 0)
    um2 = pltpu.roll(u, 2, 0)
    c0 = carry_ref[0:1, :]
    c1 = carry_ref[1:2, :]
    um1 = jnp.where(row == 0, c1, um1)
    um2 = jnp.where(row == 0, c0, jnp.where(row == 1, c1, um2))
    if rows_prev:
        p0, p1 = p_ref[0], p_ref[1]
    else:
        p0, p1 = p_ref[0, 0:1, :], p_ref[0, 1:2, :]
    um1 = jnp.where(t >= 1, um1, p1)
    um2 = jnp.where(t >= 2, um2, jnp.where(t == 1, p1, p0))
    conv = wc_ref[0:1, :] * um2 + wc_ref[1:2, :] * um1 + wc_ref[2:3, :] * u
    y = jnp.dot((bg * conv).astype(BF16), wout_ref[...], preferred_element_type=F32)
    h = _ln(ALPHA * x + y, lng_ref[...], lnb_ref[...])
    h_ref[...] = h
    lg_ref[...] = _router(h, wr_ref, br_ref)
    carry_ref[0:2, :] = u[tm - 2:tm, :]
    if rows_prev:
        st_ref[...] = u
    else:
        st_ref[0] = u[tm - 2:tm, :]


def _conv_layer(x, prev, w_in, w_conv, w_out, lng, lnb, wr, br, *, seq, tm, rows_prev):
    n, d = x.shape
    assert n % tm == 0 and seq >= 2 and seq & (seq - 1) == 0
    assert (seq % tm == 0) if not rows_prev else (tm % seq == 0 and n == tm)
    if rows_prev:
        p_spec = pl.BlockSpec((2, tm, d), lambda i: (0, i, 0))
        st_shape = jax.ShapeDtypeStruct((n, d), F32)
        st_spec = pl.BlockSpec((tm, d), lambda i: (i, 0))
    else:
        per = seq // tm
        p_spec = pl.BlockSpec((1, 2, d), lambda i: (i // per, 0, 0))
        st_shape = jax.ShapeDtypeStruct((n // seq, 2, d), F32)
        st_spec = pl.BlockSpec((1, 2, d), lambda i: (i // per, 0, 0))
    return pl.pallas_call(
        functools.partial(_conv_body, seq=seq, tm=tm, rows_prev=rows_prev),
        grid=(n // tm,),
        in_specs=[pl.BlockSpec((tm, d), lambda i: (i, 0)), p_spec,
                  _full((d, 3 * d)), _full((3, d)), _full((d, d)), _full((1, d)), _full((1, d)),
                  _full((d, N_EXPERTS)), _full((1, N_EXPERTS))],
        out_specs=[pl.BlockSpec((tm, d), lambda i: (i, 0)), st_spec,
                   pl.BlockSpec((tm, N_EXPERTS), lambda i: (i, 0))],
        out_shape=[jax.ShapeDtypeStruct((n, d), F32), st_shape,
                   jax.ShapeDtypeStruct((n, N_EXPERTS), F32)],
        scratch_shapes=[pltpu.VMEM((8, d), F32)],
        compiler_params=_params(("arbitrary",)),
        name="conv_mixer",
    )(x, prev, w_in.astype(BF16), w_conv, w_out.astype(BF16), lng, lnb, wr, br)


def _moe_body(te_ref, nv_ref, nx_ref, x_ref, w1_hbm, b1_ref, w2_hbm, b2_ref, o_ref, w1b, w2b, w1s, w2s, sem,
              *, layer):
    i = pl.program_id(0)
    e = te_ref[i]
    prev = te_ref[jnp.maximum(i - 1, 0)]
    valid = i < nv_ref[0]

    def weight_copies(expert):
        return (pltpu.make_async_copy(w1_hbm.at[layer, expert], w1s, sem.at[0]),
                pltpu.make_async_copy(w2_hbm.at[layer, expert], w2s, sem.at[1]))

    @pl.when(i == 0)
    def _():
        for cp in weight_copies(e):
            cp.start()

    @pl.when(valid & ((i == 0) | (e != prev)))
    def _():
        for cp in weight_copies(e):
            cp.wait()
        w1b[...] = w1s[...].astype(BF16)
        w2b[...] = w2s[...].astype(BF16)
        nxt = nx_ref[i]

        @pl.when(nxt >= 0)
        def _():
            for cp in weight_copies(nxt):
                cp.start()

    @pl.when(valid)
    def _():
        hgu = jnp.dot(x_ref[...].astype(BF16), w1b[...], preferred_element_type=F32) + b1_ref[0, 0]
        g = jnp.minimum(hgu[:, :D_FF], SWIGLU_LIMIT)
        u = jnp.clip(hgu[:, D_FF:], -SWIGLU_LIMIT, SWIGLU_LIMIT)
        a = (u + 1.0) * (g * jax.nn.sigmoid(SWIGLU_ALPHA * g))
        o_ref[...] = jnp.dot(a.astype(BF16), w2b[...], preferred_element_type=F32) + b2_ref[0, 0]

    @pl.when(jnp.logical_not(valid))
    def _():
        o_ref[...] = jnp.zeros_like(o_ref)


def _moe_experts(xb, tile_e, n_valid, next_e, w1, b1, w2, b2, *, layer, tm):
    n_rows, d = xb.shape
    n_tiles = n_rows // tm
    bias = lambda w: pl.BlockSpec((1, 1, 1, w), lambda i, te, nv, nx: (layer, te[i], 0, 0))
    grid_spec = pltpu.PrefetchScalarGridSpec(
        num_scalar_prefetch=3,
        grid=(n_tiles,),
        in_specs=[pl.BlockSpec((tm, d), lambda i, te, nv, nx: (i, 0)),
                  pl.BlockSpec(memory_space=pl.ANY), bias(2 * D_FF),
                  pl.BlockSpec(memory_space=pl.ANY), bias(d)],
        out_specs=pl.BlockSpec((tm, d), lambda i, te, nv, nx: (i, 0)),
        scratch_shapes=[pltpu.VMEM((d, 2 * D_FF), BF16), pltpu.VMEM((D_FF, d), BF16),
                        pltpu.VMEM((d, 2 * D_FF), F32), pltpu.VMEM((D_FF, d), F32),
                        pltpu.SemaphoreType.DMA((2,))],
    )
    return pl.pallas_call(
        functools.partial(_moe_body, layer=layer), grid_spec=grid_spec,
        out_shape=jax.ShapeDtypeStruct((n_rows, d), F32),
        compiler_params=_params(("arbitrary",)),
        name="moe_experts",
    )(tile_e, n_valid, next_e, xb, w1, b1[:, :, None, :], w2, b2[:, :, None, :])


def _combine_body(h_ref, yg_ref, gate_ref, lng_ref, lnb_ref, o_ref):
    gate = gate_ref[...]
    f = gate[:, 0:1] * yg_ref[0]
    for k in range(1, TOP_K):
        f = f + gate[:, k:k + 1] * yg_ref[k]
    o_ref[...] = _ln(ALPHA * h_ref[...] + f, lng_ref[...], lnb_ref[...])


def _moe_combine(h, yg, gate, lng, lnb, *, tm):
    n, d = h.shape
    return pl.pallas_call(
        _combine_body, grid=(n // tm,),
        in_specs=[pl.BlockSpec((tm, d), lambda i: (i, 0)),
                  pl.BlockSpec((TOP_K, tm, d), lambda i: (0, i, 0)),
                  pl.BlockSpec((tm, TOP_K), lambda i: (i, 0)),
                  _full((1, d)), _full((1, d))],
        out_specs=pl.BlockSpec((tm, d), lambda i: (i, 0)),
        out_shape=jax.ShapeDtypeStruct((n, d), F32),
        compiler_params=_params(("arbitrary",)),
        name="moe_combine",
    )(h, yg, gate, lng, lnb)


def _route_body(lg_ref, e_ref, g_ref, r_ref, cnt_ref, carry, *, tm):
    i = pl.program_id(0)

    @pl.when(i == 0)
    def _():
        carry[...] = jnp.zeros_like(carry)

    lg = lg_ref[...]
    lane = lax.broadcasted_iota(jnp.int32, lg.shape, 1).astype(F32)
    picks, vals = [], []
    for _ in range(TOP_K):
        m = jnp.max(lg, axis=-1, keepdims=True)
        idx = jnp.min(jnp.where(lg == m, lane, float(N_EXPERTS)), axis=-1, keepdims=True)
        hit = lane == idx
        picks.append((idx.astype(jnp.int32), hit))
        vals.append(m)
        lg = jnp.where(hit, -jnp.inf, lg)
    ex = [jnp.exp(v - vals[0]) for v in vals]
    den = ex[0] + ex[1] + ex[2] + ex[3]
    chosen = picks[0][1] | picks[1][1] | picks[2][1] | picks[3][1]
    onehot = chosen.astype(BF16)
    before = (lax.broadcasted_iota(jnp.int32, (tm, tm), 1)
              < lax.broadcasted_iota(jnp.int32, (tm, tm), 0)).astype(BF16)
    prior = jnp.dot(before, onehot, preferred_element_type=F32) + carry[...]
    for k, (idx, hit) in enumerate(picks):
        e_ref[:, k:k + 1] = idx
        g_ref[:, k:k + 1] = ex[k] / den
        r_ref[:, k:k + 1] = jnp.sum(jnp.where(hit, prior, 0.0), axis=-1, keepdims=True).astype(jnp.int32)
    carry[...] = carry[...] + jnp.sum(chosen.astype(F32), axis=0, keepdims=True)
    cnt_ref[...] = carry[...]


def _route_topk(logits, *, tm):
    n, ne = logits.shape
    col = lambda dt: (pl.BlockSpec((tm, TOP_K), lambda i: (i, 0)), jax.ShapeDtypeStruct((n, TOP_K), dt))
    specs, shapes = zip(col(jnp.int32), col(F32), col(jnp.int32),
                        (_full((1, ne)), jax.ShapeDtypeStruct((1, ne), F32)))
    return pl.pallas_call(
        functools.partial(_route_body, tm=tm), grid=(n // tm,),
        in_specs=[pl.BlockSpec((tm, ne), lambda i: (i, 0))],
        out_specs=list(specs), out_shape=list(shapes),
        scratch_shapes=[pltpu.VMEM((1, ne), F32)],
        compiler_params=_params(("arbitrary",)),
        name="moe_route",
    )(logits)


def _moe_route(h, logits, *, tm_rows):
    n, d = h.shape
    top_e, gate, rank, counts = _route_topk(logits, tm=min(n, 512))
    flat_e = top_e.reshape(-1)
    rank = rank.reshape(-1)
    counts = counts.reshape(-1).astype(jnp.int32)
    nk = n * TOP_K
    padded = (counts + tm_rows - 1) // tm_rows * tm_rows
    pad_end = jnp.cumsum(padded)
    pad_start = pad_end - padded
    dest = (pad_start[flat_e] + rank).astype(jnp.int32)
    n_tiles = (nk + N_EXPERTS * (tm_rows - 1) + tm_rows - 1) // tm_rows
    n_rows = n_tiles * tm_rows
    tile_start = jnp.arange(n_tiles, dtype=jnp.int32) * tm_rows
    tile_e = jnp.minimum(jnp.sum((pad_end[None, :] <= tile_start[:, None]).astype(jnp.int32), axis=1), N_EXPERTS - 1)
    n_valid = (pad_end[-1] // tm_rows).astype(jnp.int32).reshape(1)
    row_tok = jnp.zeros((n_rows,), jnp.int32).at[dest].set(
        jnp.arange(nk, dtype=jnp.int32) // TOP_K, unique_indices=True, mode='promise_in_bounds')
    xb = h[row_tok]
    eidx = jnp.arange(N_EXPERTS, dtype=jnp.int32)
    later = jnp.where((eidx[None, :] > eidx[:, None]) & (counts[None, :] > 0), eidx[None, :], N_EXPERTS)
    nxt_tab = jnp.min(later, axis=1)
    next_e = jnp.where(nxt_tab == N_EXPERTS, -1, nxt_tab).astype(jnp.int32)[tile_e]
    return xb, (tile_e, n_valid, next_e), dest.reshape(n, TOP_K).T, gate


def _moe_apply(h, routed, w1, b1, w2, b2, lng, lnb, *, layer, tm_rows, tm_tok):
    xb, tiles, dest_t, gate = routed
    yb = _moe_experts(xb, *tiles, w1, b1, w2, b2, layer=layer, tm=tm_rows)
    yg = yb[dest_t]
    return _moe_combine(h, yg, gate, lng, lnb, tm=tm_tok)


def _proj_body(h_ref, w_ref, rc_ref, rs1_ref, rs2_ref, *out_refs, heads_out, seq, q_scale):
    z = jnp.dot(h_ref[...].astype(BF16), w_ref[...], preferred_element_type=F32)
    cc, s1, s2 = rc_ref[...], rs1_ref[...], rs2_ref[...]

    def rope(x):
        return x * cc + pltpu.roll(x, LANES - ROT_DIM // 2, 1) * s1 + pltpu.roll(x, ROT_DIM // 2, 1) * s2

    def rope_cols(lo, hi):
        return [rope(z[:, c:c + LANES]) for c in range(lo, hi, LANES)]

    if heads_out:
        rows_c, rows_s, rows_w, g_ref, q_ref, ks_ref, vs_ref, kw_ref, vw_ref = out_refs
    else:
        rows_c, rows_s, rows_w, g_ref, q_ref = out_refs
    half = KV_COLS // 2
    for br, rows in enumerate((rows_c, rows_s, rows_w)):
        base = br * KV_COLS
        k = jnp.concatenate(rope_cols(base, base + half), axis=1)
        v = z[:, base + half:base + KV_COLS]
        kv = jnp.concatenate([k, v], axis=1)
        if heads_out:
            rows[0] = kv.T
        else:
            rows[...] = kv
        if heads_out and br >= 1:
            tm = k.shape[0]
            t = (pl.program_id(0) % (seq // tm)) * tm + lax.broadcasted_iota(jnp.int32, (tm, HEAD_DIM), 0)
            lane = lax.broadcasted_iota(jnp.int32, (tm, HEAD_DIM), 1)
            blk_onehot = (lane == lax.shift_right_logical(t, 6)).astype(BF16)
            one_lane = (lane == 0).astype(BF16)
            k_ref, v_ref = (ks_ref, vs_ref) if br == 1 else (kw_ref, vw_ref)
            for g in range(N_KV_HEADS):
                kg = k[:, g * HEAD_DIM:(g + 1) * HEAD_DIM].astype(BF16)
                vg = v[:, g * HEAD_DIM:(g + 1) * HEAD_DIM].astype(BF16)
                k_ref[0, g] = jnp.concatenate([kg, blk_onehot], axis=1) if br == 1 else kg
                v_ref[0, g] = jnp.concatenate([vg, one_lane], axis=1)
    qb = 3 * KV_COLS
    q = jnp.concatenate(rope_cols(qb, qb + D_MODEL), axis=1) * q_scale
    if heads_out:
        for hd in range(N_HEADS):
            q_ref[0, hd] = q[:, hd * HEAD_DIM:(hd + 1) * HEAD_DIM].astype(BF16)
    else:
        q_ref[...] = q
    g_ref[...] = jax.nn.sigmoid(z[:, qb + D_MODEL:])


def _rope_tables(pos):
    half = ROT_DIM // 2
    inv = ROPE_THETA ** (-jnp.arange(half, dtype=F32) * 2.0 / ROT_DIM)
    ang = pos.astype(F32)[:, None] * inv[None, :]
    cos, sin = jnp.cos(ang), jnp.sin(ang)
    zeros = jnp.zeros((pos.shape[0], HEAD_DIM - ROT_DIM), F32)
    ones = jnp.ones_like(zeros)
    z8 = jnp.zeros_like(sin)
    c = jnp.concatenate([cos, cos, ones], axis=1)
    s1 = jnp.concatenate([-sin, z8, zeros], axis=1)
    s2 = jnp.concatenate([z8, sin, zeros], axis=1)
    rep = LANES // HEAD_DIM
    return tuple(jnp.tile(a, (1, rep)) for a in (c, s1, s2))


def _proj_weights(kv_w, wq, wg):
    wg4 = wg.reshape(D_MODEL, N_KV_HEADS, HEADS_PER_GROUP * 3)
    wg4 = jnp.pad(wg4, ((0, 0), (0, 0), (0, LANES - HEADS_PER_GROUP * 3))).reshape(D_MODEL, N_KV_HEADS * LANES)
    return jnp.concatenate([kv_w, wq, wg4], axis=1).astype(BF16)


def _nsa_proj(h, w_cat, tables, *, tm, seq, heads_out):
    n, d = h.shape
    ncol = w_cat.shape[1]
    per = tables[0].shape[0] // tm
    tab_spec = pl.BlockSpec((tm, LANES), lambda i: (i % per, 0))
    row_spec = lambda w: pl.BlockSpec((tm, w), lambda i: (i, 0))
    g_spec, g_shape = row_spec(N_KV_HEADS * LANES), jax.ShapeDtypeStruct((n, N_KV_HEADS * LANES), F32)
    if heads_out:
        bsz = n // seq
        tps = seq // tm
        out_specs = [pl.BlockSpec((1, KV_COLS, tm), lambda i: (i // tps, 0, i % tps))] * 3 + [g_spec]
        out_shape = [jax.ShapeDtypeStruct((bsz, KV_COLS, seq), F32)] * 3 + [g_shape]
        assert seq // SEL_BLK <= HEAD_DIM
        hspec = lambda nh, w: pl.BlockSpec((1, nh, tm, w), lambda i: (i // tps, 0, i % tps, 0))
        hshape = lambda nh, w: jax.ShapeDtypeStruct((bsz, nh, seq, w), BF16)
        widths = (2 * HEAD_DIM, 2 * HEAD_DIM, HEAD_DIM, 2 * HEAD_DIM)
        out_specs += [hspec(N_HEADS, HEAD_DIM)] + [hspec(N_KV_HEADS, w) for w in widths]
        out_shape += [hshape(N_HEADS, HEAD_DIM)] + [hshape(N_KV_HEADS, w) for w in widths]
    else:
        out_specs = [row_spec(KV_COLS)] * 3 + [g_spec, row_spec(d)]
        out_shape = [jax.ShapeDtypeStruct((n, KV_COLS), F32)] * 3 + [g_shape, jax.ShapeDtypeStruct((n, d), F32)]
    q_scale = HEAD_DIM ** -0.5 * (float(np.log2(np.e)) if heads_out else 1.0)
    return pl.pallas_call(
        functools.partial(_proj_body, heads_out=heads_out, seq=seq, q_scale=q_scale), grid=(n // tm,),
        in_specs=[row_spec(d), _full((d, ncol)), tab_spec, tab_spec, tab_spec],
        out_specs=out_specs, out_shape=out_shape,
        compiler_params=_params(("arbitrary",)),
        name="nsa_proj",
    )(h, w_cat, *tables)


def _cmp_math(xt, w1_ref, pe_ref, w2_ref, o_ref, nch):
    k = pl.program_id(1) // 2
    x = jnp.concatenate(
        [xt[pl.ds(c, nch, stride=CMP_STRIDE), :].astype(BF16) for c in range(CMP_STRIDE)],
        axis=1)
    acc = jnp.dot(x, w1_ref[0], preferred_element_type=F32)
    pe = pe_ref[pl.ds(k, 1), :]
    hid = []
    for gs in range(2):
        pa = acc[:, gs * 2 * CMP_HID:gs * 2 * CMP_HID + CMP_HID]
        pb = acc[:, gs * 2 * CMP_HID + CMP_HID:(gs + 1) * 2 * CMP_HID]
        pb_next = pltpu.roll(pb, nch - 1, 0)
        hid.append(jax.nn.gelu(pa + pb_next + pe))
    hid = jnp.concatenate(hid, axis=1).astype(BF16)
    o_ref[0] = jnp.dot(hid, w2_ref[0], preferred_element_type=F32)


def _cmp_t_body(r_ref, w1_ref, pe_ref, w2_ref, o_ref, xt, *, nch):
    for p in range(r_ref.shape[2] // LANES):
        xt[p * LANES:(p + 1) * LANES, :] = r_ref[0, :, p * LANES:(p + 1) * LANES].T
    _cmp_math(xt, w1_ref, pe_ref, w2_ref, o_ref, nch)


def _cmp_paged_body(pt_ref, cache_ref, w1_ref, pe_ref, w2_ref, o_ref, pbuf, xt, sem, *, nch, n_pages):
    ncol = pl.num_programs(1)
    step = pl.program_id(0) * ncol + pl.program_id(1)
    nsteps = pl.num_programs(0) * ncol
    slot = step & 1

    def page_copy(st, p, sl):
        bb = st // ncol
        col = pl.multiple_of((st - bb * ncol) * LANES, LANES)
        return pltpu.make_async_copy(cache_ref.at[pt_ref[bb * n_pages + p], pl.ds(col, LANES), :],
                                     pbuf.at[sl, p], sem.at[sl])

    @pl.when(step == 0)
    def _():
        for p in range(n_pages):
            page_copy(step, p, slot).start()

    @pl.when(step + 1 < nsteps)
    def _():
        for p in range(n_pages):
            page_copy(step + 1, p, 1 - slot).start()

    for p in range(n_pages):
        page_copy(step, p, slot).wait()
    for p in range(n_pages):
        xt[p * PAGE_SIZE:(p + 1) * PAGE_SIZE, :] = pbuf[slot, p].T
    _cmp_math(xt, w1_ref, pe_ref, w2_ref, o_ref, nch)


def _cmp_weights(cmp_pe, cmp_w1, cmp_b1, cmp_w2):
    eye = jnp.eye(2, dtype=F32)
    w1r = cmp_w1.reshape(2, 2, CMP_STRIDE, HEAD_DIM, CMP_HID)
    w1bd = jnp.einsum('ab,kncdh->kcadbnh', eye, w1r).reshape(2, CMP_STRIDE * 2 * HEAD_DIM, 4 * CMP_HID)
    w2bd = jnp.einsum('ab,khd->kahbd', eye, cmp_w2).reshape(2, 2 * CMP_HID, 2 * HEAD_DIM)
    pe_term = jnp.einsum('kcd,kcdh->kh', cmp_pe, cmp_w1, precision=HIGHEST) + cmp_b1
    return w1bd.astype(BF16), pe_term, w2bd.astype(BF16)


def _cmp_specs(cw, nch, imap):
    w1bd, pe_term, w2bd = cw
    return ([pl.BlockSpec((1,) + w1bd.shape[1:], imap(lambda b, j: (j // 2, 0, 0))),
             pl.BlockSpec(pe_term.shape, imap(lambda b, j: (0, 0))),
             pl.BlockSpec((1,) + w2bd.shape[1:], imap(lambda b, j: (j // 2, 0, 0)))],
            pl.BlockSpec((1, nch, LANES), imap(lambda b, j: (b, 0, j))))


def _compress_t(rows_t, cw):
    bsz, _, t = rows_t.shape
    nch = t // CMP_STRIDE
    w_specs, o_spec = _cmp_specs(cw, nch, lambda f: f)
    return pl.pallas_call(
        functools.partial(_cmp_t_body, nch=nch), grid=(bsz, KV_COLS // LANES),
        in_specs=[pl.BlockSpec((1, LANES, t), lambda b, j: (b, j, 0))] + w_specs,
        out_specs=o_spec,
        out_shape=jax.ShapeDtypeStruct((bsz, nch, KV_COLS), F32),
        scratch_shapes=[pltpu.VMEM((t, LANES), F32)],
        compiler_params=_params(("arbitrary", "arbitrary")),
        name="compress_kv",
    )(rows_t, *cw)


def _compress_paged(cache_t, page_table, cw):
    bsz, n_pages = page_table.shape
    nch = n_pages * PAGE_SIZE // CMP_STRIDE
    w_specs, o_spec = _cmp_specs(cw, nch, lambda f: (lambda b, j, pt: f(b, j)))
    grid_spec = pltpu.PrefetchScalarGridSpec(
        num_scalar_prefetch=1, grid=(bsz, KV_COLS // LANES),
        in_specs=[pl.BlockSpec(memory_space=pl.ANY)] + w_specs,
        out_specs=o_spec,
        scratch_shapes=[pltpu.VMEM((2, n_pages, LANES, PAGE_SIZE), F32),
                        pltpu.VMEM((n_pages * PAGE_SIZE, LANES), F32),
                        pltpu.SemaphoreType.DMA((2,))])
    return pl.pallas_call(
        functools.partial(_cmp_paged_body, nch=nch, n_pages=n_pages), grid_spec=grid_spec,
        out_shape=jax.ShapeDtypeStruct((bsz, nch, KV_COLS), F32),
        compiler_params=_params(("arbitrary", "arbitrary")),
        name="compress_kv_paged",
    )(page_table.reshape(-1), cache_t, *cw)


def _softmax_probs(s, m, exp_fn=jnp.exp):
    s = jnp.where(m, s, NEG)
    mx = jnp.max(s, axis=-1, keepdims=True)
    e = jnp.where(m, exp_fn(s - mx), 0.0)
    den = jnp.sum(e, axis=-1, keepdims=True)
    return e / jnp.maximum(den, 1e-30)


def _select_blocks(imp_t, qpos_row, n_rows):
    lanes = imp_t.shape[1]
    j = lax.broadcasted_iota(jnp.int32, (n_rows, lanes), 0)
    cur = lax.shift_right_logical(qpos_row, 6)
    valid = j <= cur
    forced = (j == 0) | (j == cur) | (j == cur - 1)
    sc = jnp.where(valid, jnp.where(forced, FORCE, imp_t[:n_rows]), NEG)
    return sc, j, valid


def _overlap_t(n_sel_pad, n_cmp_pad):
    c0 = np.arange(n_cmp_pad)[None, :] * CMP_STRIDE
    s0 = np.arange(n_sel_pad)[:, None] * SEL_BLK
    ov = np.clip(np.minimum(c0 + CMP_BLK, s0 + SEL_BLK) - np.maximum(c0, s0), 0, None).astype(np.float32) / CMP_BLK
    return ov


def _attn_p_body(q_ref, kck_ref, kcv_ref, ks_ref, vs_ref, kw_ref, vw_ref, g_ref, h_ref,
                 ovt_ref, wo_ref, lng_ref, lnb_ref, wr_ref, br_ref,
                 o_ref, lg_ref, o_scr, *, tq, seq, kc):
    i = pl.program_id(1)
    g = pl.program_id(2)
    t0 = i * tq
    rows = HEADS_PER_GROUP * tq
    n_sel = seq // SEL_BLK
    q = q_ref[0].reshape(rows, HEAD_DIM)
    tpos = t0 + (lax.broadcasted_iota(jnp.int32, (rows, 1), 0) & (tq - 1))

    n_cmp = kck_ref.shape[2]
    s = _nt(q, kck_ref[0, 0])
    cend = lax.broadcasted_iota(jnp.int32, (1, n_cmp), 1) * CMP_STRIDE + (CMP_BLK - 1)
    p_c = _softmax_probs(s, cend <= tpos, jnp.exp2)
    o_c = jnp.dot(p_c.astype(BF16), kcv_ref[0, 0], preferred_element_type=F32)
    psum = p_c[0:tq]
    for qh in range(1, HEADS_PER_GROUP):
        psum = psum + p_c[qh * tq:(qh + 1) * tq]
    imp_t = lax.dot_general(ovt_ref[...], psum, (((1,), (1,)), ((), ())),
                            precision=HIGHEST, preferred_element_type=F32)

    qrow = t0 + lax.broadcasted_iota(jnp.int32, (1, tq), 1)
    sc, j, valid = _select_blocks(imp_t, qrow, n_sel)
    rank = jnp.zeros((n_sel, tq), jnp.int32)
    for jp in range(n_sel):
        r = sc[jp:jp + 1, :]
        beats = (r > sc) | ((r == sc) & (j > jp))
        rank = rank + beats.astype(jnp.int32)
    sel_t = (valid & (rank < SEL_TOPN)).astype(F32)
    sel_t = jnp.concatenate([sel_t, jnp.zeros((LANES - n_sel, tq), F32)], axis=0)
    sel = sel_t.T[:, 0:HEAD_DIM]
    blk_lane = lax.broadcasted_iota(jnp.int32, (1, HEAD_DIM), 1)
    r_i = lax.broadcasted_iota(jnp.int32, (tq, 1), 0)
    tq_pos = t0 + r_i
    t0a = pl.multiple_of(t0, tq)
    causal = jnp.where(lax.broadcasted_iota(jnp.int32, (1, tq), 1) <= r_i, 0.0, NEG)
    causal = jnp.concatenate([causal] * HEADS_PER_GROUP, axis=0)

    s = _nt(q, ks_ref[0, 0, pl.ds(t0a, tq), 0:HEAD_DIM]) + causal
    m0 = jnp.max(s, axis=-1, keepdims=True)
    acc0 = jnp.dot(jnp.exp2(s - m0).astype(BF16), vs_ref[0, 0, pl.ds(t0a, tq), :], preferred_element_type=F32)
    sweep = jnp.where((sel > 0.5) & (blk_lane < lax.shift_right_logical(t0, 6)), 0.0, NEG).astype(BF16)
    q_aug = jnp.concatenate([q, jnp.concatenate([sweep] * HEADS_PER_GROUP, axis=0)], axis=1)

    def sweep_chunk(c, carry):
        m_run, acc = carry
        k0 = pl.multiple_of(c * kc, kc)
        s = _nt(q_aug, ks_ref[0, 0, pl.ds(k0, kc), :])
        m_new = jnp.maximum(m_run, jnp.max(s, axis=-1, keepdims=True))
        pv = jnp.dot(jnp.exp2(s - m_new).astype(BF16), vs_ref[0, 0, pl.ds(k0, kc), :],
                     preferred_element_type=F32)
        return m_new, jnp.exp2(m_run - m_new) * acc + pv

    _, acc = lax.fori_loop(0, (t0 + kc - 1) // kc, sweep_chunk, (m0, acc0))
    o_s = acc[:, 0:HEAD_DIM] / acc[:, HEAD_DIM:HEAD_DIM + 1]

    ws = pl.multiple_of(jnp.maximum(t0 - WINDOW, 0), tq)
    wi = ws + lax.broadcasted_iota(jnp.int32, (1, WINDOW), 1)
    wbias = jnp.where((wi > tq_pos - WINDOW) & (wi < t0), 0.0, NEG)
    s_d = _nt(q, kw_ref[0, 0, pl.ds(t0a, tq), :]) + causal
    s_w = _nt(q, kw_ref[0, 0, pl.ds(ws, WINDOW), :]) + jnp.concatenate([wbias] * HEADS_PER_GROUP, axis=0)
    m = jnp.maximum(jnp.max(s_d, axis=-1, keepdims=True), jnp.max(s_w, axis=-1, keepdims=True))
    acc = (jnp.dot(jnp.exp2(s_d - m).astype(BF16), vw_ref[0, 0, pl.ds(t0a, tq), :], preferred_element_type=F32)
           + jnp.dot(jnp.exp2(s_w - m).astype(BF16), vw_ref[0, 0, pl.ds(ws, WINDOW), :],
                     preferred_element_type=F32))
    o_w = acc[:, 0:HEAD_DIM] / acc[:, HEAD_DIM:HEAD_DIM + 1]

    gates = g_ref[...]
    for qh in range(HEADS_PER_GROUP):
        sl = slice(qh * tq, (qh + 1) * tq)
        o_h = (gates[:, 3 * qh:3 * qh + 1] * o_c[sl] + gates[:, 3 * qh + 1:3 * qh + 2] * o_s[sl]
               + gates[:, 3 * qh + 2:3 * qh + 3] * o_w[sl])
        o_scr[g, :, qh * HEAD_DIM:(qh + 1) * HEAD_DIM] = o_h

    @pl.when(g == N_KV_HEADS - 1)
    def _():
        o = jnp.concatenate([o_scr[gg] for gg in range(N_KV_HEADS)], axis=1).astype(BF16)
        y = jnp.dot(o, wo_ref[...], preferred_element_type=F32)
        h = _ln(ALPHA * h_ref[...] + y, lng_ref[...], lnb_ref[...])
        o_ref[...] = h
        lg_ref[...] = _router(h, wr_ref, br_ref)


def _nsa_prompt(h, qh, kck, kcv, ks, vs, kw, vw, gates, wo, lng, lnb, wr, br, *, tq, kc):
    n, d = h.shape
    bsz, _, seq, _ = ks.shape
    nt = seq // tq
    n_cmp = kck.shape[2]
    ovt = jnp.asarray(_overlap_t(LANES, n_cmp))
    assert kc % tq == 0 and seq % kc == 0 and SEL_BLK % 64 == 0 and tq % SEL_BLK == 0
    grp = lambda a: pl.BlockSpec((1, 1) + a.shape[2:], lambda b, i, g: (b, g, 0, 0))
    row = lambda w: pl.BlockSpec((tq, w), lambda b, i, g: (b * nt + i, 0))
    return pl.pallas_call(
        functools.partial(_attn_p_body, tq=tq, seq=seq, kc=kc),
        grid=(bsz, nt, N_KV_HEADS),
        in_specs=[pl.BlockSpec((1, HEADS_PER_GROUP, tq, HEAD_DIM), lambda b, i, g: (b, g, i, 0)),
                  grp(kck), grp(kcv), grp(ks), grp(vs), grp(kw), grp(vw),
                  pl.BlockSpec((tq, LANES), lambda b, i, g: (b * nt + i, g)), row(d),
                  _full(ovt.shape), _full((d, d)), _full((1, d)), _full((1, d)),
                  _full((d, N_EXPERTS)), _full((1, N_EXPERTS))],
        out_specs=[row(d), row(N_EXPERTS)],
        out_shape=[jax.ShapeDtypeStruct((n, d), F32), jax.ShapeDtypeStruct((n, N_EXPERTS), F32)],
        scratch_shapes=[pltpu.VMEM((N_KV_HEADS, tq, HEADS_PER_GROUP * HEAD_DIM), F32)],
        compiler_params=_params(("arbitrary", "arbitrary", "arbitrary")),
        name="nsa_prompt",
    )(qh, kck, kcv, ks, vs, kw, vw, gates, h, ovt, wo.astype(BF16), lng, lnb, wr, br)


def _attn_s_body(pt_ref, q_ref, gt_ref, kvc_ref, cache_ref, ns_ref, cw_ref, nw_ref, ovt_ref, exp_ref,
                 o_ref, sc_ref, kvbuf, sem, msk_scr, m_scr, l_scr, acc_scr, ocw_scr, *, dec, past, ppc):
    nrow = q_ref.shape[1]
    half = KV_COLS // 2
    c = pl.program_id(1)
    nc = pl.num_programs(1)
    step = pl.program_id(0) * nc + c
    nsteps = pl.num_programs(0) * nc
    slot = step & 1

    def chunk_copies(st, sl):
        return [pltpu.make_async_copy(cache_ref.at[pt_ref[st * ppc + p]],
                                      kvbuf.at[sl, :, pl.ds(p * PAGE_SIZE, PAGE_SIZE)], sem.at[sl])
                for p in range(ppc)]

    @pl.when(step == 0)
    def _():
        for cp in chunk_copies(step, slot):
            cp.start()

    @pl.when(step + 1 < nsteps)
    def _():
        for cp in chunk_copies(step + 1, 1 - slot):
            cp.start()

    q = q_ref[0]
    row = lax.broadcasted_iota(jnp.int32, (nrow, 1), 0)
    tok = row & (dec - 1)
    gsel = lax.shift_right_logical(row, 2) & (N_KV_HEADS - 1)

    def pick(o):
        out = jnp.zeros((nrow, HEAD_DIM), F32)
        for gg in range(N_KV_HEADS):
            out = out + jnp.where(gsel == gg, o[:, gg * HEAD_DIM:(gg + 1) * HEAD_DIM], 0.0)
        return out

    def new_rows(new_ref):
        k_new = new_ref[0, :, 0:half].astype(BF16)
        m_new = lax.broadcasted_iota(jnp.int32, (1, new_ref.shape[1]), 1) <= tok
        return jnp.where(m_new, _nt(q, k_new), NEG), m_new, new_ref[0, :, half:KV_COLS].astype(BF16)

    @pl.when(c == 0)
    def _():
        qpos = past + tok
        n_cmp = kvc_ref.shape[1]
        kc = kvc_ref[0, :, 0:half].astype(BF16)
        vc = kvc_ref[0, :, half:KV_COLS].astype(BF16)
        cend = lax.broadcasted_iota(jnp.int32, (1, n_cmp), 1) * CMP_STRIDE + (CMP_BLK - 1)
        p_c = _softmax_probs(_nt(q, kc), cend <= qpos)
        ocw_scr[0] = pick(jnp.dot(p_c.astype(BF16), vc, preferred_element_type=F32))
        ngt = N_KV_HEADS * dec
        psum = p_c[0:ngt]
        for qh in range(1, HEADS_PER_GROUP):
            psum = psum + p_c[qh * ngt:(qh + 1) * ngt]
        psum = jnp.concatenate([psum, jnp.zeros((LANES - ngt, n_cmp), F32)], axis=0)
        imp_t = lax.dot_general(ovt_ref[...], psum, (((1,), (1,)), ((), ())),
                                precision=HIGHEST, preferred_element_type=F32)

        n_sel = past // SEL_BLK + 1
        n_sel8 = sc_ref.shape[0]
        lane_tok = lax.broadcasted_iota(jnp.int32, (1, LANES), 1) & (dec - 1)
        sc, j, valid = _select_blocks(imp_t, past + lane_tok, n_sel8)
        sc_ref[...] = sc

        def rank_step(jp, rank):
            r = sc_ref[pl.ds(jp, 1), :]
            beats = (r > sc) | ((r == sc) & (j > jp))
            return rank + beats.astype(jnp.int32)

        rank = lax.fori_loop(0, n_sel, rank_step, jnp.zeros((n_sel8, LANES), jnp.int32))
        sel_t = (valid & (rank < SEL_TOPN)).astype(F32)
        sel_t = jnp.concatenate([sel_t, jnp.zeros((ovt_ref.shape[0] - n_sel8, LANES), F32)], axis=0)
        sel = sel_t.T[0:ngt, 0:LANES]
        sel = jnp.concatenate([sel] * HEADS_PER_GROUP, axis=0)
        msk_scr[...] = jnp.where(sel > 0.5, 0.0, NEG).astype(BF16)

        s_n, m_n, v_n = new_rows(ns_ref)
        m0 = jnp.max(s_n, axis=-1, keepdims=True)
        e_n = jnp.where(m_n, jnp.exp(s_n - m0), 0.0)
        m_scr[...] = m0
        l_scr[...] = jnp.sum(e_n, axis=-1, keepdims=True)
        acc_scr[...] = jnp.dot(e_n.astype(BF16), v_n, preferred_element_type=F32)

        n_win = cw_ref.shape[2]
        m_w = lax.broadcasted_iota(jnp.int32, (1, n_win), 1) > tok + (n_win - WINDOW)
        s_o = jnp.where(m_w, jnp.dot(q, cw_ref[0, 0:half, :].astype(BF16), preferred_element_type=F32), NEG)
        s_n, m_n, v_n = new_rows(nw_ref)
        mx = jnp.maximum(jnp.max(s_o, axis=-1, keepdims=True), jnp.max(s_n, axis=-1, keepdims=True))
        e_o = jnp.where(m_w, jnp.exp(s_o - mx), 0.0)
        e_n = jnp.where(m_n, jnp.exp(s_n - mx), 0.0)
        den = jnp.sum(e_o, axis=-1, keepdims=True) + jnp.sum(e_n, axis=-1, keepdims=True)
        o_w = (_nt(e_o.astype(BF16), cw_ref[0, half:KV_COLS, :].astype(BF16))
               + jnp.dot(e_n.astype(BF16), v_n, preferred_element_type=F32))
        ocw_scr[1] = pick(o_w / den)

    for cp in chunk_copies(step, slot):
        cp.wait()
    kt = kvbuf[slot, 0:half, :].astype(BF16)
    vt = kvbuf[slot, half:KV_COLS, :].astype(BF16)
    s = (jnp.dot(q, kt, preferred_element_type=F32)
         + jnp.dot(msk_scr[...], exp_ref[c], preferred_element_type=F32))
    m_old = m_scr[...]
    m_new = jnp.maximum(m_old, jnp.max(s, axis=-1, keepdims=True))
    alpha = jnp.exp(m_old - m_new)
    e = jnp.exp(s - m_new)
    m_scr[...] = m_new
    l_scr[...] = alpha * l_scr[...] + jnp.sum(e, axis=-1, keepdims=True)
    acc_scr[...] = alpha * acc_scr[...] + _nt(e.astype(BF16), vt)

    @pl.when(c == nc - 1)
    def _():
        gt = gt_ref[0]
        o_s = pick(acc_scr[...] / l_scr[...])
        o_ref[0] = gt[:, 0:1] * ocw_scr[0] + gt[:, 1:2] * o_s + gt[:, 2:3] * ocw_scr[1]


def _nsa_sample(qbd, gt, kvc, cache_t, page_table, new_s, cache_w_t, new_w, *, dec, past, ppc):
    bsz, nrow, _ = qbd.shape
    n_pages = past // PAGE_SIZE
    assert n_pages % ppc == 0 and past // SEL_BLK <= LANES
    nc = n_pages // ppc
    ck = ppc * PAGE_SIZE
    ovt = jnp.asarray(_overlap_t(2 * LANES, kvc.shape[1]))
    kblk = (np.arange(past) // SEL_BLK).reshape(nc, 1, ck)
    expand = jnp.asarray(np.arange(LANES)[None, :, None] == kblk, BF16)
    n_sel8 = (past // SEL_BLK + 1 + 7) // 8 * 8
    b3 = lambda a: pl.BlockSpec((1,) + a.shape[1:], lambda b, c, pt: (b, 0, 0))
    full = lambda a: pl.BlockSpec(a.shape, lambda b, c, pt: (0,) * a.ndim)
    grid_spec = pltpu.PrefetchScalarGridSpec(
        num_scalar_prefetch=1, grid=(bsz, nc),
        in_specs=[b3(qbd), b3(gt), b3(kvc), pl.BlockSpec(memory_space=pl.ANY), b3(new_s), b3(cache_w_t),
                  b3(new_w), full(ovt), full(expand)],
        out_specs=pl.BlockSpec((1, nrow, HEAD_DIM), lambda b, c, pt: (b, 0, 0)),
        scratch_shapes=[pltpu.VMEM((n_sel8, LANES), F32),
                        pltpu.VMEM((2, KV_COLS, ck), F32),
                        pltpu.SemaphoreType.DMA((2,)),
                        pltpu.VMEM((nrow, LANES), BF16),
                        pltpu.VMEM((nrow, 1), F32), pltpu.VMEM((nrow, 1), F32),
                        pltpu.VMEM((nrow, KV_COLS // 2), F32),
                        pltpu.VMEM((2, nrow, HEAD_DIM), F32)])
    return pl.pallas_call(
        functools.partial(_attn_s_body, dec=dec, past=past, ppc=ppc), grid_spec=grid_spec,
        out_shape=jax.ShapeDtypeStruct((bsz, nrow, HEAD_DIM), F32),
        compiler_params=_params(("arbitrary", "arbitrary")),
        name="nsa_sample",
    )(page_table.reshape(-1), qbd, gt, kvc, cache_t, new_s, cache_w_t, new_w, ovt, expand)


def _out_body(o_ref, h_ref, wo_ref, lng_ref, lnb_ref, wr_ref, br_ref, y_ref, lg_ref):
    y = jnp.dot(o_ref[...].astype(BF16), wo_ref[...], preferred_element_type=F32)
    h = _ln(ALPHA * h_ref[...] + y, lng_ref[...], lnb_ref[...])
    y_ref[...] = h
    lg_ref[...] = _router(h, wr_ref, br_ref)


def _out_proj(o, h, wo, lng, lnb, wr, br):
    n, d = h.shape
    return pl.pallas_call(
        _out_body, grid=(1,),
        in_specs=[_full((n, d)), _full((n, d)), _full((d, d)), _full((1, d)), _full((1, d)),
                  _full((d, N_EXPERTS)), _full((1, N_EXPERTS))],
        out_specs=[_full((n, d)), _full((n, N_EXPERTS))],
        out_shape=[jax.ShapeDtypeStruct((n, d), F32), jax.ShapeDtypeStruct((n, N_EXPERTS), F32)],
        compiler_params=_params(("arbitrary",)),
        name="nsa_out_proj",
    )(o, h, wo.astype(BF16), lng, lnb, wr, br)


def kernel(x_prompt, x_sample, state_conv, cache_kv_cmp, cache_kv_sel, cache_kv_win, page_table,
           conv_w_in, conv_w, conv_w_out, kv_w, cmp_pe, cmp_w1, cmp_b1, cmp_w2,
           nsa_wq, nsa_wg, nsa_wo, moe_wr, moe_br, moe_w1, moe_b1, moe_w2, moe_b2, ln_g, ln_b):
    bp, sp, d = x_prompt.shape
    bd, sd, _ = x_sample.shape
    kv_shape = (2, N_KV_HEADS, HEAD_DIM)
    lng = lambda l, s: ln_g[l, s][None, :]
    lnb = lambda l, s: ln_b[l, s][None, :]
    moe = lambda l, h, routed, **kw: _moe_apply(h, routed, moe_w1, moe_b1, moe_w2, moe_b2,
                                                lng(l, 1), lnb(l, 1), layer=l, **kw)
    w_cat = _proj_weights(kv_w, nsa_wq[0], nsa_wg[0])
    cw = _cmp_weights(cmp_pe, cmp_w1, cmp_b1, cmp_w2)
    br = lambda l: moe_br[l][None, :]

    n_p = bp * sp
    xp = x_prompt.reshape(n_p, d)
    h, conv_p, lg = _conv_layer(xp, jnp.zeros((bp, 2, d), F32), conv_w_in[0], conv_w[0], conv_w_out[0],
                                lng(0, 0), lnb(0, 0), moe_wr[0], br(0), seq=sp, tm=256, rows_prev=False)
    routed_p = _moe_route(h, lg, tm_rows=256)
    n_s = bd * sd
    xs = x_sample.reshape(n_s, d)
    prev_rows = jnp.repeat(state_conv[0].transpose(1, 0, 2), sd, axis=1)
    hs, u_s, lgs = _conv_layer(xs, prev_rows, conv_w_in[0], conv_w[0], conv_w_out[0],
                               lng(0, 0), lnb(0, 0), moe_wr[0], br(0), seq=sd, tm=n_s, rows_prev=True)
    conv_s = u_s.reshape(bd, sd, d)[:, sd - 2:]
    routed_s = _moe_route(hs, lgs, tm_rows=64)
    h = moe(0, h, routed_p, tm_rows=256, tm_tok=256)
    hs = moe(0, hs, routed_s, tm_rows=64, tm_tok=n_s)

    pos_s = PAST_LEN + (jnp.arange(n_s, dtype=jnp.int32) % sd)
    rc_s, rs_s, rw_s, gates_s, q_s = _nsa_proj(hs, w_cat, _rope_tables(pos_s), tm=n_s, seq=sd, heads_out=False)
    cols_major = lambda a: a.transpose(0, 2, 3, 4, 1).reshape(a.shape[0], KV_COLS, a.shape[1])
    kvc_s = _compress_paged(cols_major(cache_kv_cmp), page_table, cw)
    q5 = q_s.reshape(bd, sd, N_KV_HEADS, HEADS_PER_GROUP, HEAD_DIM).transpose(0, 3, 2, 1, 4)
    eye = jnp.eye(N_KV_HEADS, dtype=F32)
    qbd = jnp.einsum('bqgtd,gh->bqgthd', q5, eye).reshape(bd, N_HEADS * sd, N_KV_HEADS * HEAD_DIM).astype(BF16)
    g5 = gates_s.reshape(bd, sd, N_KV_HEADS, LANES)[..., :HEADS_PER_GROUP * 3]
    g5 = g5.reshape(bd, sd, N_KV_HEADS, HEADS_PER_GROUP, 3).transpose(0, 3, 2, 1, 4).reshape(bd, N_HEADS * sd, 3)
    gt = jnp.pad(g5, ((0, 0), (0, 0), (0, 5)))
    pad_new = lambda r: jnp.pad(r.reshape(bd, sd, KV_COLS), ((0, 0), (0, 16 - sd), (0, 0)))
    w_buf = cache_kv_win.shape[1]
    o_s = _nsa_sample(qbd, gt, kvc_s, cols_major(cache_kv_sel), page_table, pad_new(rs_s),
                      cols_major(cache_kv_win), pad_new(rw_s), dec=sd, past=PAST_LEN, ppc=16)
    o_s = o_s.reshape(bd, HEADS_PER_GROUP, N_KV_HEADS, sd, HEAD_DIM).transpose(0, 3, 2, 1, 4).reshape(n_s, d)
    hs, lgs = _out_proj(o_s, hs, nsa_wo[0], lng(1, 0), lnb(1, 0), moe_wr[1], br(1))
    routed_s = _moe_route(hs, lgs, tm_rows=64)

    tabs = _rope_tables(jnp.arange(sp, dtype=jnp.int32))
    rc, rs, rw, gates, qh, ks, vs, kw, vw = _nsa_proj(h, w_cat, tabs, tm=256, seq=sp, heads_out=True)
    kvc = _compress_t(rc, cw)
    kvc_h = kvc.reshape(bp, -1, 2, N_KV_HEADS, HEAD_DIM).transpose(2, 0, 3, 1, 4).astype(BF16)
    h, lg = _nsa_prompt(h, qh, kvc_h[0], kvc_h[1], ks, vs, kw, vw, gates, nsa_wo[0],
                        lng(1, 0), lnb(1, 0), moe_wr[1], br(1), tq=256, kc=512)
    routed_p = _moe_route(h, lg, tm_rows=256)
    y_sample = moe(1, hs, routed_s, tm_rows=64, tm_tok=n_s).reshape(bd, sd, d)
    y_prompt = moe(1, h, routed_p, tm_rows=256, tm_tok=256).reshape(bp, sp, d)
    rows_out = lambda r_t: r_t.reshape((bp,) + kv_shape + (r_t.shape[-1],)).transpose(0, 4, 1, 2, 3)
    kv_cmp_p = rows_out(rc)
    kv_sel_p = rows_out(rs)
    kv_win_p = rows_out(rw[:, :, sp - min(WINDOW, sp):])

    kv_cmp_s = rc_s.reshape((bd, sd) + kv_shape)
    kv_sel_s = rs_s.reshape((bd, sd) + kv_shape)
    kv_win_s = jnp.concatenate([cache_kv_win, rw_s.reshape((bd, sd) + kv_shape)], axis=1)[:, -w_buf:]

    return (y_prompt, y_sample, conv_p[None], kv_cmp_p, kv_sel_p, kv_win_p,
            conv_s[None], kv_cmp_s, kv_sel_s, kv_win_s)
```

```python
import functools

import numpy as np
import jax
import jax.numpy as jnp
from jax import lax
from jax.experimental import pallas as pl
from jax.experimental.pallas import tpu as pltpu

F32 = jnp.float32
BF16 = jnp.bfloat16
HIGHEST = lax.Precision.HIGHEST

D_MODEL = 1024
DEPTH = 2
PAST_LEN = 8192
PAGE_SIZE = 128
N_HEADS = 16
N_KV_HEADS = 4
HEADS_PER_GROUP = N_HEADS // N_KV_HEADS
HEAD_DIM = D_MODEL // N_HEADS
ROT_DIM = HEAD_DIM // 4
ROPE_THETA = 500000.0
CMP_BLK = 32
CMP_STRIDE = 16
CMP_HID = 2 * HEAD_DIM
SEL_BLK = 64
SEL_TOPN = 16
WINDOW = 512
N_EXPERTS = 32
TOP_K = 4
D_FF = D_MODEL
SWIGLU_LIMIT = 7.0
SWIGLU_ALPHA = 1.702
ALPHA = (2 * DEPTH) ** 0.25
LN_EPS = 1e-5
NEG = -1e30
FORCE = 1e4

KV_COLS = 2 * N_KV_HEADS * HEAD_DIM
LANES = 128
VMEM_LIMIT = 56 * 2 ** 20


def _params(sem, vmem=VMEM_LIMIT):
    return pltpu.CompilerParams(dimension_semantics=sem, vmem_limit_bytes=vmem)


def _ln(x, g, b):
    mu = jnp.mean(x, axis=-1, keepdims=True)
    xc = x - mu
    var = jnp.mean(xc * xc, axis=-1, keepdims=True)
    return xc * lax.rsqrt(var + LN_EPS) * g + b


def _nt(a, b):
    return lax.dot_general(a, b, (((1,), (1,)), ((), ())), preferred_element_type=F32)


def _router(h, wr_ref, br_ref):
    return jnp.dot(h, wr_ref[...], precision=HIGHEST, preferred_element_type=F32) + br_ref[...]


def _full(shape):
    return pl.BlockSpec(shape, lambda *_: (0,) * len(shape))


def _conv_body(x_ref, p_ref, win_ref, wc_ref, wout_ref, lng_ref, lnb_ref, wr_ref, br_ref,
               h_ref, st_ref, lg_ref, carry_ref, *, seq, tm, rows_prev):
    i = pl.program_id(0)

    @pl.when(i == 0)
    def _():
        carry_ref[...] = jnp.zeros_like(carry_ref)

    x = x_ref[...]
    d = x.shape[1]
    z = jnp.dot(x.astype(BF16), win_ref[...], preferred_element_type=F32)
    bg, c, xh = z[:, :d], z[:, d:2 * d], z[:, 2 * d:]
    u = c * xh
    row = lax.broadcasted_iota(jnp.int32, (tm, 1), 0)
    t = (i * tm + row) & (seq - 1)
    um1 = pltpu.roll(u, 1, 0)
    um2 = pltpu.roll(u, 2, 0)
    c0 = carry_ref[0:1, :]
    c1 = carry_ref[1:2, :]
    um1 = jnp.where(row == 0, c1, um1)
    um2 = jnp.where(row == 0, c0, jnp.where(row == 1, c1, um2))
    if rows_prev:
        p0, p1 = p_ref[0], p_ref[1]
    else:
        p0, p1 = p_ref[0, 0:1, :], p_ref[0, 1:2, :]
    um1 = jnp.where(t >= 1, um1, p1)
    um2 = jnp.where(t >= 2, um2, jnp.where(t == 1, p1, p0))
    conv = wc_ref[0:1, :] * um2 + wc_ref[1:2, :] * um1 + wc_ref[2:3, :] * u
    y = jnp.dot((bg * conv).astype(BF16), wout_ref[...], preferred_element_type=F32)
    h = _ln(ALPHA * x + y, lng_ref[...], lnb_ref[...])
    h_ref[...] = h
    lg_ref[...] = _router(h, wr_ref, br_ref)
    carry_ref[0:2, :] = u[tm - 2:tm, :]
    if rows_prev:
        st_ref[...] = u
    else:
        st_ref[0] = u[tm - 2:tm, :]


def _conv_layer(x, prev, w_in, w_conv, w_out, lng, lnb, wr, br, *, seq, tm, rows_prev):
    n, d = x.shape
    assert n % tm == 0 and seq >= 2 and seq & (seq - 1) == 0
    assert (seq % tm == 0) if not rows_prev else (tm % seq == 0 and n == tm)
    if rows_prev:
        p_spec = pl.BlockSpec((2, tm, d), lambda i: (0, i, 0))
        st_shape = jax.ShapeDtypeStruct((n, d), F32)
        st_spec = pl.BlockSpec((tm, d), lambda i: (i, 0))
    else:
        per = seq // tm
        p_spec = pl.BlockSpec((1, 2, d), lambda i: (i // per, 0, 0))
        st_shape = jax.ShapeDtypeStruct((n // seq, 2, d), F32)
        st_spec = pl.BlockSpec((1, 2, d), lambda i: (i // per, 0, 0))
    return pl.pallas_call(
        functools.partial(_conv_body, seq=seq, tm=tm, rows_prev=rows_prev),
        grid=(n // tm,),
        in_specs=[pl.BlockSpec((tm, d), lambda i: (i, 0)), p_spec,
                  _full((d, 3 * d)), _full((3, d)), _full((d, d)), _full((1, d)), _full((1, d)),
                  _full((d, N_EXPERTS)), _full((1, N_EXPERTS))],
        out_specs=[pl.BlockSpec((tm, d), lambda i: (i, 0)), st_spec,
                   pl.BlockSpec((tm, N_EXPERTS), lambda i: (i, 0))],
        out_shape=[jax.ShapeDtypeStruct((n, d), F32), st_shape,
                   jax.ShapeDtypeStruct((n, N_EXPERTS), F32)],
        scratch_shapes=[pltpu.VMEM((8, d), F32)],
        compiler_params=_params(("arbitrary",)),
        name="conv_mixer",
    )(x, prev, w_in.astype(BF16), w_conv, w_out.astype(BF16), lng, lnb, wr, br)


def _moe_body(te_ref, nv_ref, nx_ref, x_ref, w1_hbm, b1_ref, w2_hbm, b2_ref, o_ref, w1b, w2b, w1s, w2s, sem,
              *, layer):
    i = pl.program_id(0)
    e = te_ref[i]
    prev = te_ref[jnp.maximum(i - 1, 0)]
    valid = i < nv_ref[0]

    def weight_copies(expert):
        return (pltpu.make_async_copy(w1_hbm.at[layer, expert], w1s, sem.at[0]),
                pltpu.make_async_copy(w2_hbm.at[layer, expert], w2s, sem.at[1]))

    @pl.when(i == 0)
    def _():
        for cp in weight_copies(e):
            cp.start()

    @pl.when(valid & ((i == 0) | (e != prev)))
    def _():
        for cp in weight_copies(e):
            cp.wait()
        w1b[...] = w1s[...].astype(BF16)
        w2b[...] = w2s[...].astype(BF16)
        nxt = nx_ref[i]

        @pl.when(nxt >= 0)
        def _():
            for cp in weight_copies(nxt):
                cp.start()

    @pl.when(valid)
    def _():
        hgu = jnp.dot(x_ref[...].astype(BF16), w1b[...], preferred_element_type=F32) + b1_ref[0, 0]
        g = jnp.minimum(hgu[:, :D_FF], SWIGLU_LIMIT)
        u = jnp.clip(hgu[:, D_FF:], -SWIGLU_LIMIT, SWIGLU_LIMIT)
        a = (u + 1.0) * (g * jax.nn.sigmoid(SWIGLU_ALPHA * g))
        o_ref[...] = jnp.dot(a.astype(BF16), w2b[...], preferred_element_type=F32) + b2_ref[0, 0]

    @pl.when(jnp.logical_not(valid))
    def _():
        o_ref[...] = jnp.zeros_like(o_ref)


def _moe_experts(xb, tile_e, n_valid, next_e, w1, b1, w2, b2, *, layer, tm):
    n_rows, d = xb.shape
    n_tiles = n_rows // tm
    bias = lambda w: pl.BlockSpec((1, 1, 1, w), lambda i, te, nv, nx: (layer, te[i], 0, 0))
    grid_spec = pltpu.PrefetchScalarGridSpec(
        num_scalar_prefetch=3,
        grid=(n_tiles,),
        in_specs=[pl.BlockSpec((tm, d), lambda i, te, nv, nx: (i, 0)),
                  pl.BlockSpec(memory_space=pl.ANY), bias(2 * D_FF),
                  pl.BlockSpec(memory_space=pl.ANY), bias(d)],
        out_specs=pl.BlockSpec((tm, d), lambda i, te, nv, nx: (i, 0)),
        scratch_shapes=[pltpu.VMEM((d, 2 * D_FF), BF16), pltpu.VMEM((D_FF, d), BF16),
                        pltpu.VMEM((d, 2 * D_FF), F32), pltpu.VMEM((D_FF, d), F32),
                        pltpu.SemaphoreType.DMA((2,))],
    )
    return pl.pallas_call(
        functools.partial(_moe_body, layer=layer), grid_spec=grid_spec,
        out_shape=jax.ShapeDtypeStruct((n_rows, d), F32),
        compiler_params=_params(("arbitrary",)),
        name="moe_experts",
    )(tile_e, n_valid, next_e, xb, w1, b1[:, :, None, :], w2, b2[:, :, None, :])


def _combine_body(h_ref, yg_ref, gate_ref, lng_ref, lnb_ref, o_ref):
    gate = gate_ref[...]
    f = gate[:, 0:1] * yg_ref[0]
    for k in range(1, TOP_K):
        f = f + gate[:, k:k + 1] * yg_ref[k]
    o_ref[...] = _ln(ALPHA * h_ref[...] + f, lng_ref[...], lnb_ref[...])


def _moe_combine(h, yg, gate, lng, lnb, *, tm):
    n, d = h.shape
    return pl.pallas_call(
        _combine_body, grid=(n // tm,),
        in_specs=[pl.BlockSpec((tm, d), lambda i: (i, 0)),
                  pl.BlockSpec((TOP_K, tm, d), lambda i: (0, i, 0)),
                  pl.BlockSpec((tm, TOP_K), lambda i: (i, 0)),
                  _full((1, d)), _full((1, d))],
        out_specs=pl.BlockSpec((tm, d), lambda i: (i, 0)),
        out_shape=jax.ShapeDtypeStruct((n, d), F32),
        compiler_params=_params(("arbitrary",)),
        name="moe_combine",
    )(h, yg, gate, lng, lnb)


def _route_body(lg_ref, e_ref, g_ref, r_ref, cnt_ref, carry, *, tm):
    i = pl.program_id(0)

    @pl.when(i == 0)
    def _():
        carry[...] = jnp.zeros_like(carry)

    lg = lg_ref[...]
    lane = lax.broadcasted_iota(jnp.int32, lg.shape, 1).astype(F32)
    picks, vals = [], []
    for _ in range(TOP_K):
        m = jnp.max(lg, axis=-1, keepdims=True)
        idx = jnp.min(jnp.where(lg == m, lane, float(N_EXPERTS)), axis=-1, keepdims=True)
        hit = lane == idx
        picks.append((idx.astype(jnp.int32), hit))
        vals.append(m)
        lg = jnp.where(hit, -jnp.inf, lg)
    ex = [jnp.exp(v - vals[0]) for v in vals]
    den = ex[0] + ex[1] + ex[2] + ex[3]
    chosen = picks[0][1] | picks[1][1] | picks[2][1] | picks[3][1]
    onehot = chosen.astype(BF16)
    before = (lax.broadcasted_iota(jnp.int32, (tm, tm), 1)
              < lax.broadcasted_iota(jnp.int32, (tm, tm), 0)).astype(BF16)
    prior = jnp.dot(before, onehot, preferred_element_type=F32) + carry[...]
    for k, (idx, hit) in enumerate(picks):
        e_ref[:, k:k + 1] = idx
        g_ref[:, k:k + 1] = ex[k] / den
        r_ref[:, k:k + 1] = jnp.sum(jnp.where(hit, prior, 0.0), axis=-1, keepdims=True).astype(jnp.int32)
    carry[...] = carry[...] + jnp.sum(chosen.astype(F32), axis=0, keepdims=True)
    cnt_ref[...] = carry[...]


def _route_topk(logits, *, tm):
    n, ne = logits.shape
    col = lambda dt: (pl.BlockSpec((tm, TOP_K), lambda i: (i, 0)), jax.ShapeDtypeStruct((n, TOP_K), dt))
    specs, shapes = zip(col(jnp.int32), col(F32), col(jnp.int32),
                        (_full((1, ne)), jax.ShapeDtypeStruct((1, ne), F32)))
    return pl.pallas_call(
        functools.partial(_route_body, tm=tm), grid=(n // tm,),
        in_specs=[pl.BlockSpec((tm, ne), lambda i: (i, 0))],
        out_specs=list(specs), out_shape=list(shapes),
        scratch_shapes=[pltpu.VMEM((1, ne), F32)],
        compiler_params=_params(("arbitrary",)),
        name="moe_route",
    )(logits)


def _moe_route(h, logits, *, tm_rows):
    n, d = h.shape
    top_e, gate, rank, counts = _route_topk(logits, tm=min(n, 512))
    flat_e = top_e.reshape(-1)
    rank = rank.reshape(-1)
    counts = counts.reshape(-1).astype(jnp.int32)
    nk = n * TOP_K
    padded = (counts + tm_rows - 1) // tm_rows * tm_rows
    pad_end = jnp.cumsum(padded)
    pad_start = pad_end - padded
    dest = (pad_start[flat_e] + rank).astype(jnp.int32)
    n_tiles = (nk + N_EXPERTS * (tm_rows - 1) + tm_rows - 1) // tm_rows
    n_rows = n_tiles * tm_rows
    tile_start = jnp.arange(n_tiles, dtype=jnp.int32) * tm_rows
    tile_e = jnp.minimum(jnp.sum((pad_end[None, :] <= tile_start[:, None]).astype(jnp.int32), axis=1), N_EXPERTS - 1)
    n_valid = (pad_end[-1] // tm_rows).astype(jnp.int32).reshape(1)
    _, tok_sorted = lax.sort_key_val(dest, jnp.arange(nk, dtype=jnp.int32) // TOP_K)
    seg_end = jnp.cumsum(counts)
    shift = pad_start - (seg_end - counts)
    tile_ok = tile_start < pad_end[-1]
    last = jnp.where(tile_ok, seg_end[tile_e] - 1, nk - 1)
    src = jnp.minimum(jnp.arange(n_rows, dtype=jnp.int32) - jnp.repeat(shift[tile_e], tm_rows),
                      jnp.repeat(last, tm_rows))
    row_tok = tok_sorted.at[src].get(indices_are_sorted=True, mode='promise_in_bounds')
    xb = h[row_tok]
    eidx = jnp.arange(N_EXPERTS, dtype=jnp.int32)
    later = jnp.where((eidx[None, :] > eidx[:, None]) & (counts[None, :] > 0), eidx[None, :], N_EXPERTS)
    nxt_tab = jnp.min(later, axis=1)
    next_e = jnp.where(nxt_tab == N_EXPERTS, -1, nxt_tab).astype(jnp.int32)[tile_e]
    return xb, (tile_e, n_valid, next_e), dest.reshape(n, TOP_K).T, gate


def _moe_apply(h, routed, w1, b1, w2, b2, lng, lnb, *, layer, tm_rows, tm_tok):
    xb, tiles, dest_t, gate = routed
    yb = _moe_experts(xb, *tiles, w1, b1, w2, b2, layer=layer, tm=tm_rows)
    yg = yb[dest_t]
    return _moe_combine(h, yg, gate, lng, lnb, tm=tm_tok)


def _proj_body(h_ref, w_ref, rc_ref, rs1_ref, rs2_ref, *out_refs, heads_out, seq, q_scale):
    z = jnp.dot(h_ref[...].astype(BF16), w_ref[...], preferred_element_type=F32)
    cc, s1, s2 = rc_ref[...], rs1_ref[...], rs2_ref[...]

    def rope(x):
        return x * cc + pltpu.roll(x, LANES - ROT_DIM // 2, 1) * s1 + pltpu.roll(x, ROT_DIM // 2, 1) * s2

    def rope_cols(lo, hi):
        return [rope(z[:, c:c + LANES]) for c in range(lo, hi, LANES)]

    if heads_out:
        rows_c, rows_s, rows_w, g_ref, q_ref, ks_ref, vs_ref, kw_ref, vw_ref = out_refs
    else:
        rows_c, rows_s, rows_w, g_ref, q_ref = out_refs
    half = KV_COLS // 2
    for br, rows in enumerate((rows_c, rows_s, rows_w)):
        base = br * KV_COLS
        k = jnp.concatenate(rope_cols(base, base + half), axis=1)
        v = z[:, base + half:base + KV_COLS]
        kv = jnp.concatenate([k, v], axis=1)
        if heads_out:
            rows[0] = kv.T
        else:
            rows[...] = kv
        if heads_out and br >= 1:
            tm = k.shape[0]
            t = (pl.program_id(0) % (seq // tm)) * tm + lax.broadcasted_iota(jnp.int32, (tm, HEAD_DIM), 0)
            lane = lax.broadcasted_iota(jnp.int32, (tm, HEAD_DIM), 1)
            blk_onehot = (lane == lax.shift_right_logical(t, 6)).astype(BF16)
            one_lane = (lane == 0).astype(BF16)
            k_ref, v_ref = (ks_ref, vs_ref) if br == 1 else (kw_ref, vw_ref)
            for g in range(N_KV_HEADS):
                kg = k[:, g * HEAD_DIM:(g + 1) * HEAD_DIM].astype(BF16)
                vg = v[:, g * HEAD_DIM:(g + 1) * HEAD_DIM].astype(BF16)
                k_ref[0, g] = jnp.concatenate([kg, blk_onehot], axis=1) if br == 1 else kg
                v_ref[0, g] = jnp.concatenate([vg, one_lane], axis=1)
    qb = 3 * KV_COLS
    q = jnp.concatenate(rope_cols(qb, qb + D_MODEL), axis=1) * q_scale
    if heads_out:
        for hd in range(N_HEADS):
            q_ref[0, hd] = q[:, hd * HEAD_DIM:(hd + 1) * HEAD_DIM].astype(BF16)
    else:
        q_ref[...] = q
    g_ref[...] = jax.nn.sigmoid(z[:, qb + D_MODEL:])


def _rope_tables(pos):
    half = ROT_DIM // 2
    inv = ROPE_THETA ** (-jnp.arange(half, dtype=F32) * 2.0 / ROT_DIM)
    ang = pos.astype(F32)[:, None] * inv[None, :]
    cos, sin = jnp.cos(ang), jnp.sin(ang)
    zeros = jnp.zeros((pos.shape[0], HEAD_DIM - ROT_DIM), F32)
    ones = jnp.ones_like(zeros)
    z8 = jnp.zeros_like(sin)
    c = jnp.concatenate([cos, cos, ones], axis=1)
    s1 = jnp.concatenate([-sin, z8, zeros], axis=1)
    s2 = jnp.concatenate([z8, sin, zeros], axis=1)
    rep = LANES // HEAD_DIM
    return tuple(jnp.tile(a, (1, rep)) for a in (c, s1, s2))


def _proj_weights(kv_w, wq, wg):
    wg4 = wg.reshape(D_MODEL, N_KV_HEADS, HEADS_PER_GROUP * 3)
    wg4 = jnp.pad(wg4, ((0, 0), (0, 0), (0, LANES - HEADS_PER_GROUP * 3))).reshape(D_MODEL, N_KV_HEADS * LANES)
    return jnp.concatenate([kv_w, wq, wg4], axis=1).astype(BF16)


def _nsa_proj(h, w_cat, tables, *, tm, seq, heads_out):
    n, d = h.shape
    ncol = w_cat.shape[1]
    per = tables[0].shape[0] // tm
    tab_spec = pl.BlockSpec((tm, LANES), lambda i: (i % per, 0))
    row_spec = lambda w: pl.BlockSpec((tm, w), lambda i: (i, 0))
    g_spec, g_shape = row_spec(N_KV_HEADS * LANES), jax.ShapeDtypeStruct((n, N_KV_HEADS * LANES), F32)
    if heads_out:
        bsz = n // seq
        tps = seq // tm
        out_specs = [pl.BlockSpec((1, KV_COLS, tm), lambda i: (i // tps, 0, i % tps))] * 3 + [g_spec]
        out_shape = [jax.ShapeDtypeStruct((bsz, KV_COLS, seq), F32)] * 3 + [g_shape]
        assert seq // SEL_BLK <= HEAD_DIM
        hspec = lambda nh, w: pl.BlockSpec((1, nh, tm, w), lambda i: (i // tps, 0, i % tps, 0))
        hshape = lambda nh, w: jax.ShapeDtypeStruct((bsz, nh, seq, w), BF16)
        widths = (2 * HEAD_DIM, 2 * HEAD_DIM, HEAD_DIM, 2 * HEAD_DIM)
        out_specs += [hspec(N_HEADS, HEAD_DIM)] + [hspec(N_KV_HEADS, w) for w in widths]
        out_shape += [hshape(N_HEADS, HEAD_DIM)] + [hshape(N_KV_HEADS, w) for w in widths]
    else:
        out_specs = [row_spec(KV_COLS)] * 3 + [g_spec, row_spec(d)]
        out_shape = [jax.ShapeDtypeStruct((n, KV_COLS), F32)] * 3 + [g_shape, jax.ShapeDtypeStruct((n, d), F32)]
    q_scale = HEAD_DIM ** -0.5 * (float(np.log2(np.e)) if heads_out else 1.0)
    return pl.pallas_call(
        functools.partial(_proj_body, heads_out=heads_out, seq=seq, q_scale=q_scale), grid=(n // tm,),
        in_specs=[row_spec(d), _full((d, ncol)), tab_spec, tab_spec, tab_spec],
        out_specs=out_specs, out_shape=out_shape,
        compiler_params=_params(("arbitrary",)),
        name="nsa_proj",
    )(h, w_cat, *tables)


def _cmp_math(xt, w1_ref, pe_ref, w2_ref, o_ref, nch):
    k = pl.program_id(1) // 2
    x = jnp.concatenate(
        [xt[pl.ds(c, nch, stride=CMP_STRIDE), :].astype(BF16) for c in range(CMP_STRIDE)],
        axis=1)
    acc = jnp.dot(x, w1_ref[0], preferred_element_type=F32)
    pe = pe_ref[pl.ds(k, 1), :]
    hid = []
    for gs in range(2):
        pa = acc[:, gs * 2 * CMP_HID:gs * 2 * CMP_HID + CMP_HID]
        pb = acc[:, gs * 2 * CMP_HID + CMP_HID:(gs + 1) * 2 * CMP_HID]
        pb_next = pltpu.roll(pb, nch - 1, 0)
        hid.append(jax.nn.gelu(pa + pb_next + pe))
    hid = jnp.concatenate(hid, axis=1).astype(BF16)
    o_ref[0] = jnp.dot(hid, w2_ref[0], preferred_element_type=F32)


def _cmp_t_body(r_ref, w1_ref, pe_ref, w2_ref, o_ref, xt, *, nch):
    for p in range(r_ref.shape[2] // LANES):
        xt[p * LANES:(p + 1) * LANES, :] = r_ref[0, :, p * LANES:(p + 1) * LANES].T
    _cmp_math(xt, w1_ref, pe_ref, w2_ref, o_ref, nch)


def _cmp_paged_body(pt_ref, cache_ref, w1_ref, pe_ref, w2_ref, o_ref, pbuf, xt, sem, *, nch, n_pages):
    ncol = pl.num_programs(1)
    step = pl.program_id(0) * ncol + pl.program_id(1)
    nsteps = pl.num_programs(0) * ncol
    slot = step & 1

    def page_copy(st, p, sl):
        bb = st // ncol
        col = pl.multiple_of((st - bb * ncol) * LANES, LANES)
        return pltpu.make_async_copy(cache_ref.at[pt_ref[bb * n_pages + p], pl.ds(col, LANES), :],
                                     pbuf.at[sl, p], sem.at[sl])

    @pl.when(step == 0)
    def _():
        for p in range(n_pages):
            page_copy(step, p, slot).start()

    @pl.when(step + 1 < nsteps)
    def _():
        for p in range(n_pages):
            page_copy(step + 1, p, 1 - slot).start()

    for p in range(n_pages):
        page_copy(step, p, slot).wait()
    for p in range(n_pages):
        xt[p * PAGE_SIZE:(p + 1) * PAGE_SIZE, :] = pbuf[slot, p].T
    _cmp_math(xt, w1_ref, pe_ref, w2_ref, o_ref, nch)


def _cmp_weights(cmp_pe, cmp_w1, cmp_b1, cmp_w2):
    eye = jnp.eye(2, dtype=F32)
    w1r = cmp_w1.reshape(2, 2, CMP_STRIDE, HEAD_DIM, CMP_HID)
    w1bd = jnp.einsum('ab,kncdh->kcadbnh', eye, w1r).reshape(2, CMP_STRIDE * 2 * HEAD_DIM, 4 * CMP_HID)
    w2bd = jnp.einsum('ab,khd->kahbd', eye, cmp_w2).reshape(2, 2 * CMP_HID, 2 * HEAD_DIM)
    pe_term = jnp.einsum('kcd,kcdh->kh', cmp_pe, cmp_w1, precision=HIGHEST) + cmp_b1
    return w1bd.astype(BF16), pe_term, w2bd.astype(BF16)


def _cmp_specs(cw, nch, imap):
    w1bd, pe_term, w2bd = cw
    return ([pl.BlockSpec((1,) + w1bd.shape[1:], imap(lambda b, j: (j // 2, 0, 0))),
             pl.BlockSpec(pe_term.shape, imap(lambda b, j: (0, 0))),
             pl.BlockSpec((1,) + w2bd.shape[1:], imap(lambda b, j: (j // 2, 0, 0)))],
            pl.BlockSpec((1, nch, LANES), imap(lambda b, j: (b, 0, j))))


def _compress_t(rows_t, cw):
    bsz, _, t = rows_t.shape
    nch = t // CMP_STRIDE
    w_specs, o_spec = _cmp_specs(cw, nch, lambda f: f)
    return pl.pallas_call(
        functools.partial(_cmp_t_body, nch=nch), grid=(bsz, KV_COLS // LANES),
        in_specs=[pl.BlockSpec((1, LANES, t), lambda b, j: (b, j, 0))] + w_specs,
        out_specs=o_spec,
        out_shape=jax.ShapeDtypeStruct((bsz, nch, KV_COLS), F32),
        scratch_shapes=[pltpu.VMEM((t, LANES), F32)],
        compiler_params=_params(("arbitrary", "arbitrary")),
        name="compress_kv",
    )(rows_t, *cw)


def _compress_paged(cache_t, page_table, cw):
    bsz, n_pages = page_table.shape
    nch = n_pages * PAGE_SIZE // CMP_STRIDE
    w_specs, o_spec = _cmp_specs(cw, nch, lambda f: (lambda b, j, pt: f(b, j)))
    grid_spec = pltpu.PrefetchScalarGridSpec(
        num_scalar_prefetch=1, grid=(bsz, KV_COLS // LANES),
        in_specs=[pl.BlockSpec(memory_space=pl.ANY)] + w_specs,
        out_specs=o_spec,
        scratch_shapes=[pltpu.VMEM((2, n_pages, LANES, PAGE_SIZE), F32),
                        pltpu.VMEM((n_pages * PAGE_SIZE, LANES), F32),
                        pltpu.SemaphoreType.DMA((2,))])
    return pl.pallas_call(
        functools.partial(_cmp_paged_body, nch=nch, n_pages=n_pages), grid_spec=grid_spec,
        out_shape=jax.ShapeDtypeStruct((bsz, nch, KV_COLS), F32),
        compiler_params=_params(("arbitrary", "arbitrary")),
        name="compress_kv_paged",
    )(page_table.reshape(-1), cache_t, *cw)


def _softmax_probs(s, m, exp_fn=jnp.exp):
    s = jnp.where(m, s, NEG)
    mx = jnp.max(s, axis=-1, keepdims=True)
    e = jnp.where(m, exp_fn(s - mx), 0.0)
    den = jnp.sum(e, axis=-1, keepdims=True)
    return e / jnp.maximum(den, 1e-30)


def _select_blocks(imp_t, qpos_row, n_rows):
    lanes = imp_t.shape[1]
    j = lax.broadcasted_iota(jnp.int32, (n_rows, lanes), 0)
    cur = lax.shift_right_logical(qpos_row, 6)
    valid = j <= cur
    forced = (j == 0) | (j == cur) | (j == cur - 1)
    sc = jnp.where(valid, jnp.where(forced, FORCE, imp_t[:n_rows]), NEG)
    return sc, j, valid


def _overlap_t(n_sel_pad, n_cmp_pad):
    c0 = np.arange(n_cmp_pad)[None, :] * CMP_STRIDE
    s0 = np.arange(n_sel_pad)[:, None] * SEL_BLK
    ov = np.clip(np.minimum(c0 + CMP_BLK, s0 + SEL_BLK) - np.maximum(c0, s0), 0, None).astype(np.float32) / CMP_BLK
    return ov


def _attn_p_body(q_ref, kck_ref, kcv_ref, ks_ref, vs_ref, kw_ref, vw_ref, g_ref, h_ref,
                 ovt_ref, wo_ref, lng_ref, lnb_ref, wr_ref, br_ref,
                 o_ref, lg_ref, o_scr, *, tq, seq, kc):
    i = pl.program_id(1)
    g = pl.program_id(2)
    t0 = i * tq
    rows = HEADS_PER_GROUP * tq
    n_sel = seq // SEL_BLK
    q = q_ref[0].reshape(rows, HEAD_DIM)
    tpos = t0 + (lax.broadcasted_iota(jnp.int32, (rows, 1), 0) & (tq - 1))

    n_cmp = kck_ref.shape[2]
    s = _nt(q, kck_ref[0, 0])
    cend = lax.broadcasted_iota(jnp.int32, (1, n_cmp), 1) * CMP_STRIDE + (CMP_BLK - 1)
    p_c = _softmax_probs(s, cend <= tpos, jnp.exp2)
    o_c = jnp.dot(p_c.astype(BF16), kcv_ref[0, 0], preferred_element_type=F32)
    psum = p_c[0:tq]
    for qh in range(1, HEADS_PER_GROUP):
        psum = psum + p_c[qh * tq:(qh + 1) * tq]
    imp_t = lax.dot_general(ovt_ref[...], psum, (((1,), (1,)), ((), ())),
                            precision=HIGHEST, preferred_element_type=F32)

    qrow = t0 + lax.broadcasted_iota(jnp.int32, (1, tq), 1)
    sc, j, valid = _select_blocks(imp_t, qrow, n_sel)
    rank = jnp.zeros((n_sel, tq), jnp.int32)
    for jp in range(n_sel):
        r = sc[jp:jp + 1, :]
        beats = (r > sc) | ((r == sc) & (j > jp))
        rank = rank + beats.astype(jnp.int32)
    sel_t = (valid & (rank < SEL_TOPN)).astype(F32)
    sel_t = jnp.concatenate([sel_t, jnp.zeros((LANES - n_sel, tq), F32)], axis=0)
    sel = sel_t.T[:, 0:HEAD_DIM]
    blk_lane = lax.broadcasted_iota(jnp.int32, (1, HEAD_DIM), 1)
    r_i = lax.broadcasted_iota(jnp.int32, (tq, 1), 0)
    tq_pos = t0 + r_i
    t0a = pl.multiple_of(t0, tq)
    causal = jnp.where(lax.broadcasted_iota(jnp.int32, (1, tq), 1) <= r_i, 0.0, NEG)
    causal = jnp.concatenate([causal] * HEADS_PER_GROUP, axis=0)

    s = _nt(q, ks_ref[0, 0, pl.ds(t0a, tq), 0:HEAD_DIM]) + causal
    m0 = jnp.max(s, axis=-1, keepdims=True)
    acc0 = jnp.dot(jnp.exp2(s - m0).astype(BF16), vs_ref[0, 0, pl.ds(t0a, tq), :], preferred_element_type=F32)
    sweep = jnp.where((sel > 0.5) & (blk_lane < lax.shift_right_logical(t0, 6)), 0.0, NEG).astype(BF16)
    q_aug = jnp.concatenate([q, jnp.concatenate([sweep] * HEADS_PER_GROUP, axis=0)], axis=1)

    def sweep_chunk(c, carry):
        m_run, acc = carry
        k0 = pl.multiple_of(c * kc, kc)
        s = _nt(q_aug, ks_ref[0, 0, pl.ds(k0, kc), :])
        m_new = jnp.maximum(m_run, jnp.max(s, axis=-1, keepdims=True))
        pv = jnp.dot(jnp.exp2(s - m_new).astype(BF16), vs_ref[0, 0, pl.ds(k0, kc), :],
                     preferred_element_type=F32)
        return m_new, jnp.exp2(m_run - m_new) * acc + pv

    _, acc = lax.fori_loop(0, (t0 + kc - 1) // kc, sweep_chunk, (m0, acc0))
    o_s = acc[:, 0:HEAD_DIM] / acc[:, HEAD_DIM:HEAD_DIM + 1]

    ws = pl.multiple_of(jnp.maximum(t0 - WINDOW, 0), tq)
    wi = ws + lax.broadcasted_iota(jnp.int32, (1, WINDOW), 1)
    wbias = jnp.where((wi > tq_pos - WINDOW) & (wi < t0), 0.0, NEG)
    s_d = _nt(q, kw_ref[0, 0, pl.ds(t0a, tq), :]) + causal
    s_w = _nt(q, kw_ref[0, 0, pl.ds(ws, WINDOW), :]) + jnp.concatenate([wbias] * HEADS_PER_GROUP, axis=0)
    m = jnp.maximum(jnp.max(s_d, axis=-1, keepdims=True), jnp.max(s_w, axis=-1, keepdims=True))
    acc = (jnp.dot(jnp.exp2(s_d - m).astype(BF16), vw_ref[0, 0, pl.ds(t0a, tq), :], preferred_element_type=F32)
           + jnp.dot(jnp.exp2(s_w - m).astype(BF16), vw_ref[0, 0, pl.ds(ws, WINDOW), :],
                     preferred_element_type=F32))
    o_w = acc[:, 0:HEAD_DIM] / acc[:, HEAD_DIM:HEAD_DIM + 1]

    gates = g_ref[...]
    for qh in range(HEADS_PER_GROUP):
        sl = slice(qh * tq, (qh + 1) * tq)
        o_h = (gates[:, 3 * qh:3 * qh + 1] * o_c[sl] + gates[:, 3 * qh + 1:3 * qh + 2] * o_s[sl]
               + gates[:, 3 * qh + 2:3 * qh + 3] * o_w[sl])
        o_scr[g, :, qh * HEAD_DIM:(qh + 1) * HEAD_DIM] = o_h

    @pl.when(g == N_KV_HEADS - 1)
    def _():
        o = jnp.concatenate([o_scr[gg] for gg in range(N_KV_HEADS)], axis=1).astype(BF16)
        y = jnp.dot(o, wo_ref[...], preferred_element_type=F32)
        h = _ln(ALPHA * h_ref[...] + y, lng_ref[...], lnb_ref[...])
        o_ref[...] = h
        lg_ref[...] = _router(h, wr_ref, br_ref)


def _nsa_prompt(h, qh, kck, kcv, ks, vs, kw, vw, gates, wo, lng, lnb, wr, br, *, tq, kc):
    n, d = h.shape
    bsz, _, seq, _ = ks.shape
    nt = seq // tq
    n_cmp = kck.shape[2]
    ovt = jnp.asarray(_overlap_t(LANES, n_cmp))
    assert kc % tq == 0 and seq % kc == 0 and SEL_BLK % 64 == 0 and tq % SEL_BLK == 0
    grp = lambda a: pl.BlockSpec((1, 1) + a.shape[2:], lambda b, i, g: (b, g, 0, 0))
    row = lambda w: pl.BlockSpec((tq, w), lambda b, i, g: (b * nt + i, 0))
    return pl.pallas_call(
        functools.partial(_attn_p_body, tq=tq, seq=seq, kc=kc),
        grid=(bsz, nt, N_KV_HEADS),
        in_specs=[pl.BlockSpec((1, HEADS_PER_GROUP, tq, HEAD_DIM), lambda b, i, g: (b, g, i, 0)),
                  grp(kck), grp(kcv), grp(ks), grp(vs), grp(kw), grp(vw),
                  pl.BlockSpec((tq, LANES), lambda b, i, g: (b * nt + i, g)), row(d),
                  _full(ovt.shape), _full((d, d)), _full((1, d)), _full((1, d)),
                  _full((d, N_EXPERTS)), _full((1, N_EXPERTS))],
        out_specs=[row(d), row(N_EXPERTS)],
        out_shape=[jax.ShapeDtypeStruct((n, d), F32), jax.ShapeDtypeStruct((n, N_EXPERTS), F32)],
        scratch_shapes=[pltpu.VMEM((N_KV_HEADS, tq, HEADS_PER_GROUP * HEAD_DIM), F32)],
        compiler_params=_params(("arbitrary", "arbitrary", "arbitrary")),
        name="nsa_prompt",
    )(qh, kck, kcv, ks, vs, kw, vw, gates, h, ovt, wo.astype(BF16), lng, lnb, wr, br)


def _attn_s_body(pt_ref, q_ref, gt_ref, kvc_ref, cache_ref, ns_ref, cw_ref, nw_ref, ovt_ref, exp_ref,
                 o_ref, sc_ref, kvbuf, sem, msk_scr, m_scr, l_scr, acc_scr, ocw_scr, *, dec, past, ppc):
    nrow = q_ref.shape[1]
    half = KV_COLS // 2
    c = pl.program_id(1)
    nc = pl.num_programs(1)
    step = pl.program_id(0) * nc + c
    nsteps = pl.num_programs(0) * nc
    slot = step & 1

    def chunk_copies(st, sl):
        return [pltpu.make_async_copy(cache_ref.at[pt_ref[st * ppc + p]],
                                      kvbuf.at[sl, :, pl.ds(p * PAGE_SIZE, PAGE_SIZE)], sem.at[sl])
                for p in range(ppc)]

    @pl.when(step == 0)
    def _():
        for cp in chunk_copies(step, slot):
            cp.start()

    @pl.when(step + 1 < nsteps)
    def _():
        for cp in chunk_copies(step + 1, 1 - slot):
            cp.start()

    q = q_ref[0]
    row = lax.broadcasted_iota(jnp.int32, (nrow, 1), 0)
    tok = row & (dec - 1)
    gsel = lax.shift_right_logical(row, 2) & (N_KV_HEADS - 1)

    def pick(o):
        out = jnp.zeros((nrow, HEAD_DIM), F32)
        for gg in range(N_KV_HEADS):
            out = out + jnp.where(gsel == gg, o[:, gg * HEAD_DIM:(gg + 1) * HEAD_DIM], 0.0)
        return out

    def new_rows(new_ref):
        k_new = new_ref[0, :, 0:half].astype(BF16)
        m_new = lax.broadcasted_iota(jnp.int32, (1, new_ref.shape[1]), 1) <= tok
        return jnp.where(m_new, _nt(q, k_new), NEG), m_new, new_ref[0, :, half:KV_COLS].astype(BF16)

    @pl.when(c == 0)
    def _():
        qpos = past + tok
        n_cmp = kvc_ref.shape[1]
        kc = kvc_ref[0, :, 0:half].astype(BF16)
        vc = kvc_ref[0, :, half:KV_COLS].astype(BF16)
        cend = lax.broadcasted_iota(jnp.int32, (1, n_cmp), 1) * CMP_STRIDE + (CMP_BLK - 1)
        p_c = _softmax_probs(_nt(q, kc), cend <= qpos)
        ocw_scr[0] = pick(jnp.dot(p_c.astype(BF16), vc, preferred_element_type=F32))
        ngt = N_KV_HEADS * dec
        psum = p_c[0:ngt]
        for qh in range(1, HEADS_PER_GROUP):
            psum = psum + p_c[qh * ngt:(qh + 1) * ngt]
        psum = jnp.concatenate([psum, jnp.zeros((LANES - ngt, n_cmp), F32)], axis=0)
        imp_t = lax.dot_general(ovt_ref[...], psum, (((1,), (1,)), ((), ())),
                                precision=HIGHEST, preferred_element_type=F32)

        n_sel = past // SEL_BLK + 1
        n_sel8 = sc_ref.shape[0]
        lane_tok = lax.broadcasted_iota(jnp.int32, (1, LANES), 1) & (dec - 1)
        sc, j, valid = _select_blocks(imp_t, past + lane_tok, n_sel8)
        sc_ref[...] = sc

        def rank_step(jp, rank):
            r = sc_ref[pl.ds(jp, 1), :]
            beats = (r > sc) | ((r == sc) & (j > jp))
            return rank + beats.astype(jnp.int32)

        rank = lax.fori_loop(0, n_sel, rank_step, jnp.zeros((n_sel8, LANES), jnp.int32))
        sel_t = (valid & (rank < SEL_TOPN)).astype(F32)
        sel_t = jnp.concatenate([sel_t, jnp.zeros((ovt_ref.shape[0] - n_sel8, LANES), F32)], axis=0)
        sel = sel_t.T[0:ngt, 0:LANES]
        sel = jnp.concatenate([sel] * HEADS_PER_GROUP, axis=0)
        msk_scr[...] = jnp.where(sel > 0.5, 0.0, NEG).astype(BF16)

        s_n, m_n, v_n = new_rows(ns_ref)
        m0 = jnp.max(s_n, axis=-1, keepdims=True)
        e_n = jnp.where(m_n, jnp.exp(s_n - m0), 0.0)
        m_scr[...] = m0
        l_scr[...] = jnp.sum(e_n, axis=-1, keepdims=True)
        acc_scr[...] = jnp.dot(e_n.astype(BF16), v_n, preferred_element_type=F32)

        n_win = cw_ref.shape[2]
        m_w = lax.broadcasted_iota(jnp.int32, (1, n_win), 1) > tok + (n_win - WINDOW)
        s_o = jnp.where(m_w, jnp.dot(q, cw_ref[0, 0:half, :].astype(BF16), preferred_element_type=F32), NEG)
        s_n, m_n, v_n = new_rows(nw_ref)
        mx = jnp.maximum(jnp.max(s_o, axis=-1, keepdims=True), jnp.max(s_n, axis=-1, keepdims=True))
        e_o = jnp.where(m_w, jnp.exp(s_o - mx), 0.0)
        e_n = jnp.where(m_n, jnp.exp(s_n - mx), 0.0)
        den = jnp.sum(e_o, axis=-1, keepdims=True) + jnp.sum(e_n, axis=-1, keepdims=True)
        o_w = (_nt(e_o.astype(BF16), cw_ref[0, half:KV_COLS, :].astype(BF16))
               + jnp.dot(e_n.astype(BF16), v_n, preferred_element_type=F32))
        ocw_scr[1] = pick(o_w / den)

    for cp in chunk_copies(step, slot):
        cp.wait()
    kt = kvbuf[slot, 0:half, :].astype(BF16)
    vt = kvbuf[slot, half:KV_COLS, :].astype(BF16)
    s = (jnp.dot(q, kt, preferred_element_type=F32)
         + jnp.dot(msk_scr[...], exp_ref[c], preferred_element_type=F32))
    m_old = m_scr[...]
    m_new = jnp.maximum(m_old, jnp.max(s, axis=-1, keepdims=True))
    alpha = jnp.exp(m_old - m_new)
    e = jnp.exp(s - m_new)
    m_scr[...] = m_new
    l_scr[...] = alpha * l_scr[...] + jnp.sum(e, axis=-1, keepdims=True)
    acc_scr[...] = alpha * acc_scr[...] + _nt(e.astype(BF16), vt)

    @pl.when(c == nc - 1)
    def _():
        gt = gt_ref[0]
        o_s = pick(acc_scr[...] / l_scr[...])
        o_ref[0] = gt[:, 0:1] * ocw_scr[0] + gt[:, 1:2] * o_s + gt[:, 2:3] * ocw_scr[1]


def _nsa_sample(qbd, gt, kvc, cache_t, page_table, new_s, cache_w_t, new_w, *, dec, past, ppc):
    bsz, nrow, _ = qbd.shape
    n_pages = past // PAGE_SIZE
    assert n_pages % ppc == 0 and past // SEL_BLK <= LANES
    nc = n_pages // ppc
    ck = ppc * PAGE_SIZE
    ovt = jnp.asarray(_overlap_t(2 * LANES, kvc.shape[1]))
    kblk = (np.arange(past) // SEL_BLK).reshape(nc, 1, ck)
    expand = jnp.asarray(np.arange(LANES)[None, :, None] == kblk, BF16)
    n_sel8 = (past // SEL_BLK + 1 + 7) // 8 * 8
    b3 = lambda a: pl.BlockSpec((1,) + a.shape[1:], lambda b, c, pt: (b, 0, 0))
    full = lambda a: pl.BlockSpec(a.shape, lambda b, c, pt: (0,) * a.ndim)
    grid_spec = pltpu.PrefetchScalarGridSpec(
        num_scalar_prefetch=1, grid=(bsz, nc),
        in_specs=[b3(qbd), b3(gt), b3(kvc), pl.BlockSpec(memory_space=pl.ANY), b3(new_s), b3(cache_w_t),
                  b3(new_w), full(ovt), full(expand)],
        out_specs=pl.BlockSpec((1, nrow, HEAD_DIM), lambda b, c, pt: (b, 0, 0)),
        scratch_shapes=[pltpu.VMEM((n_sel8, LANES), F32),
                        pltpu.VMEM((2, KV_COLS, ck), F32),
                        pltpu.SemaphoreType.DMA((2,)),
                        pltpu.VMEM((nrow, LANES), BF16),
                        pltpu.VMEM((nrow, 1), F32), pltpu.VMEM((nrow, 1), F32),
                        pltpu.VMEM((nrow, KV_COLS // 2), F32),
                        pltpu.VMEM((2, nrow, HEAD_DIM), F32)])
    return pl.pallas_call(
        functools.partial(_attn_s_body, dec=dec, past=past, ppc=ppc), grid_spec=grid_spec,
        out_shape=jax.ShapeDtypeStruct((bsz, nrow, HEAD_DIM), F32),
        compiler_params=_params(("arbitrary", "arbitrary")),
        name="nsa_sample",
    )(page_table.reshape(-1), qbd, gt, kvc, cache_t, new_s, cache_w_t, new_w, ovt, expand)


def _out_body(o_ref, h_ref, wo_ref, lng_ref, lnb_ref, wr_ref, br_ref, y_ref, lg_ref):
    y = jnp.dot(o_ref[...].astype(BF16), wo_ref[...], preferred_element_type=F32)
    h = _ln(ALPHA * h_ref[...] + y, lng_ref[...], lnb_ref[...])
    y_ref[...] = h
    lg_ref[...] = _router(h, wr_ref, br_ref)


def _out_proj(o, h, wo, lng, lnb, wr, br):
    n, d = h.shape
    return pl.pallas_call(
        _out_body, grid=(1,),
        in_specs=[_full((n, d)), _full((n, d)), _full((d, d)), _full((1, d)), _full((1, d)),
                  _full((d, N_EXPERTS)), _full((1, N_EXPERTS))],
        out_specs=[_full((n, d)), _full((n, N_EXPERTS))],
        out_shape=[jax.ShapeDtypeStruct((n, d), F32), jax.ShapeDtypeStruct((n, N_EXPERTS), F32)],
        compiler_params=_params(("arbitrary",)),
        name="nsa_out_proj",
    )(o, h, wo.astype(BF16), lng, lnb, wr, br)


def kernel(x_prompt, x_sample, state_conv, cache_kv_cmp, cache_kv_sel, cache_kv_win, page_table,
           conv_w_in, conv_w, conv_w_out, kv_w, cmp_pe, cmp_w1, cmp_b1, cmp_w2,
           nsa_wq, nsa_wg, nsa_wo, moe_wr, moe_br, moe_w1, moe_b1, moe_w2, moe_b2, ln_g, ln_b):
    bp, sp, d = x_prompt.shape
    bd, sd, _ = x_sample.shape
    kv_shape = (2, N_KV_HEADS, HEAD_DIM)
    lng = lambda l, s: ln_g[l, s][None, :]
    lnb = lambda l, s: ln_b[l, s][None, :]
    moe = lambda l, h, routed, **kw: _moe_apply(h, routed, moe_w1, moe_b1, moe_w2, moe_b2,
                                                lng(l, 1), lnb(l, 1), layer=l, **kw)
    w_cat = _proj_weights(kv_w, nsa_wq[0], nsa_wg[0])
    cw = _cmp_weights(cmp_pe, cmp_w1, cmp_b1, cmp_w2)
    br = lambda l: moe_br[l][None, :]

    n_p = bp * sp
    xp = x_prompt.reshape(n_p, d)
    h, conv_p, lg = _conv_layer(xp, jnp.zeros((bp, 2, d), F32), conv_w_in[0], conv_w[0], conv_w_out[0],
                                lng(0, 0), lnb(0, 0), moe_wr[0], br(0), seq=sp, tm=256, rows_prev=False)
    routed_p = _moe_route(h, lg, tm_rows=256)
    n_s = bd * sd
    xs = x_sample.reshape(n_s, d)
    prev_rows = jnp.repeat(state_conv[0].transpose(1, 0, 2), sd, axis=1)
    hs, u_s, lgs = _conv_layer(xs, prev_rows, conv_w_in[0], conv_w[0], conv_w_out[0],
                               lng(0, 0), lnb(0, 0), moe_wr[0], br(0), seq=sd, tm=n_s, rows_prev=True)
    conv_s = u_s.reshape(bd, sd, d)[:, sd - 2:]
    routed_s = _moe_route(hs, lgs, tm_rows=64)
    h = moe(0, h, routed_p, tm_rows=256, tm_tok=256)
    hs = moe(0, hs, routed_s, tm_rows=64, tm_tok=n_s)

    pos_s = PAST_LEN + (jnp.arange(n_s, dtype=jnp.int32) % sd)
    rc_s, rs_s, rw_s, gates_s, q_s = _nsa_proj(hs, w_cat, _rope_tables(pos_s), tm=n_s, seq=sd, heads_out=False)
    cols_major = lambda a: a.transpose(0, 2, 3, 4, 1).reshape(a.shape[0], KV_COLS, a.shape[1])
    kvc_s = _compress_paged(cols_major(cache_kv_cmp), page_table, cw)
    q5 = q_s.reshape(bd, sd, N_KV_HEADS, HEADS_PER_GROUP, HEAD_DIM).transpose(0, 3, 2, 1, 4)
    eye = jnp.eye(N_KV_HEADS, dtype=F32)
    qbd = jnp.einsum('bqgtd,gh->bqgthd', q5, eye).reshape(bd, N_HEADS * sd, N_KV_HEADS * HEAD_DIM).astype(BF16)
    g5 = gates_s.reshape(bd, sd, N_KV_HEADS, LANES)[..., :HEADS_PER_GROUP * 3]
    g5 = g5.reshape(bd, sd, N_KV_HEADS, HEADS_PER_GROUP, 3).transpose(0, 3, 2, 1, 4).reshape(bd, N_HEADS * sd, 3)
    gt = jnp.pad(g5, ((0, 0), (0, 0), (0, 5)))
    pad_new = lambda r: jnp.pad(r.reshape(bd, sd, KV_COLS), ((0, 0), (0, 16 - sd), (0, 0)))
    w_buf = cache_kv_win.shape[1]
    o_s = _nsa_sample(qbd, gt, kvc_s, cols_major(cache_kv_sel), page_table, pad_new(rs_s),
                      cols_major(cache_kv_win), pad_new(rw_s), dec=sd, past=PAST_LEN, ppc=16)
    o_s = o_s.reshape(bd, HEADS_PER_GROUP, N_KV_HEADS, sd, HEAD_DIM).transpose(0, 3, 2, 1, 4).reshape(n_s, d)
    hs, lgs = _out_proj(o_s, hs, nsa_wo[0], lng(1, 0), lnb(1, 0), moe_wr[1], br(1))
    routed_s = _moe_route(hs, lgs, tm_rows=64)

    tabs = _rope_tables(jnp.arange(sp, dtype=jnp.int32))
    rc, rs, rw, gates, qh, ks, vs, kw, vw = _nsa_proj(h, w_cat, tabs, tm=256, seq=sp, heads_out=True)
    kvc = _compress_t(rc, cw)
    kvc_h = kvc.reshape(bp, -1, 2, N_KV_HEADS, HEAD_DIM).transpose(2, 0, 3, 1, 4).astype(BF16)
    h, lg = _nsa_prompt(h, qh, kvc_h[0], kvc_h[1], ks, vs, kw, vw, gates, nsa_wo[0],
                        lng(1, 0), lnb(1, 0), moe_wr[1], br(1), tq=256, kc=512)
    routed_p = _moe_route(h, lg, tm_rows=256)
    y_sample = moe(1, hs, routed_s, tm_rows=64, tm_tok=n_s).reshape(bd, sd, d)
    y_prompt = moe(1, h, routed_p, tm_rows=256, tm_tok=256).reshape(bp, sp, d)
    rows_out = lambda r_t: r_t.reshape((bp,) + kv_shape + (r_t.shape[-1],)).transpose(0, 4, 1, 2, 3)
    kv_cmp_p = rows_out(rc)
    kv_sel_p = rows_out(rs)
    kv_win_p = rows_out(rw[:, :, sp - min(WINDOW, sp):])

    kv_cmp_s = rc_s.reshape((bd, sd) + kv_shape)
    kv_sel_s = rs_s.reshape((bd, sd) + kv_shape)
    kv_win_s = jnp.concatenate([cache_kv_win, rw_s.reshape((bd, sd) + kv_shape)], axis=1)[:, -w_buf:]

    return (y_prompt, y_sample, conv_p[None], kv_cmp_p, kv_sel_p, kv_win_p,
            conv_s[None], kv_cmp_s, kv_sel_s, kv_win_s)
```

```python
import functools

import numpy as np
import jax
import jax.numpy as jnp
from jax import lax
from jax.experimental import pallas as pl
from jax.experimental.pallas import tpu as pltpu

F32 = jnp.float32
BF16 = jnp.bfloat16
HIGHEST = lax.Precision.HIGHEST

D_MODEL = 1024
DEPTH = 2
PAST_LEN = 8192
PAGE_SIZE = 128
N_HEADS = 16
N_KV_HEADS = 4
HEADS_PER_GROUP = N_HEADS // N_KV_HEADS
HEAD_DIM = D_MODEL // N_HEADS
ROT_DIM = HEAD_DIM // 4
ROPE_THETA = 500000.0
CMP_BLK = 32
CMP_STRIDE = 16
CMP_HID = 2 * HEAD_DIM
SEL_BLK = 64
SEL_TOPN = 16
WINDOW = 512
N_EXPERTS = 32
TOP_K = 4
D_FF = D_MODEL
SWIGLU_LIMIT = 7.0
SWIGLU_ALPHA = 1.702
ALPHA = (2 * DEPTH) ** 0.25
LN_EPS = 1e-5
NEG = -1e30
FORCE = 1e4

KV_COLS = 2 * N_KV_HEADS * HEAD_DIM
LANES = 128
VMEM_LIMIT = 56 * 2 ** 20


def _params(sem, vmem=VMEM_LIMIT):
    return pltpu.CompilerParams(dimension_semantics=sem, vmem_limit_bytes=vmem)


def _ln(x, g, b):
    mu = jnp.mean(x, axis=-1, keepdims=True)
    xc = x - mu
    var = jnp.mean(xc * xc, axis=-1, keepdims=True)
    return xc * lax.rsqrt(var + LN_EPS) * g + b


def _nt(a, b):
    return lax.dot_general(a, b, (((1,), (1,)), ((), ())), preferred_element_type=F32)


def _router(h, wr_ref, br_ref):
    return jnp.dot(h, wr_ref[...], precision=HIGHEST, preferred_element_type=F32) + br_ref[...]


def _full(shape):
    return pl.BlockSpec(shape, lambda *_: (0,) * len(shape))


def _conv_body(x_ref, p_ref, win_ref, wc_ref, wout_ref, lng_ref, lnb_ref, wr_ref, br_ref,
               h_ref, st_ref, lg_ref, carry_ref, *, seq, tm, rows_prev):
    i = pl.program_id(0)

    @pl.when(i == 0)
    def _():
        carry_ref[...] = jnp.zeros_like(carry_ref)

    x = x_ref[...]
    d = x.shape[1]
    z = jnp.dot(x.astype(BF16), win_ref[...], preferred_element_type=F32)
    bg, c, xh = z[:, :d], z[:, d:2 * d], z[:, 2 * d:]
    u = c * xh
    row = lax.broadcasted_iota(jnp.int32, (tm, 1), 0)
    t = (i * tm + row) & (seq - 1)
    um1 = pltpu.roll(u, 1, 0)
    um2 = pltpu.roll(u, 2, 0)
    c0 = carry_ref[0:1, :]
    c1 = carry_ref[1:2, :]
    um1 = jnp.where(row == 0, c1, um1)
    um2 = jnp.where(row == 0, c0, jnp.where(row == 1, c1, um2))
    if rows_prev:
        p0, p1 = p_ref[0], p_ref[1]
    else:
        p0, p1 = p_ref[0, 0:1, :], p_ref[0, 1:2, :]
    um1 = jnp.where(t >= 1, um1, p1)
    um2 = jnp.where(t >= 2, um2, jnp.where(t == 1, p1, p0))
    conv = wc_ref[0:1, :] * um2 + wc_ref[1:2, :] * um1 + wc_ref[2:3, :] * u
    y = jnp.dot((bg * conv).astype(BF16), wout_ref[...], preferred_element_type=F32)
    h = _ln(ALPHA * x + y, lng_ref[...], lnb_ref[...])
    h_ref[...] = h
    lg_ref[...] = _router(h, wr_ref, br_ref)
    carry_ref[0:2, :] = u[tm - 2:tm, :]
    if rows_prev:
        st_ref[...] = u
    else:
        st_ref[0] = u[tm - 2:tm, :]


def _conv_layer(x, prev, w_in, w_conv, w_out, lng, lnb, wr, br, *, seq, tm, rows_prev):
    n, d = x.shape
    assert n % tm == 0 and seq >= 2 and seq & (seq - 1) == 0
    assert (seq % tm == 0) if not rows_prev else (tm % seq == 0 and n == tm)
    if rows_prev:
        p_spec = pl.BlockSpec((2, tm, d), lambda i: (0, i, 0))
        st_shape = jax.ShapeDtypeStruct((n, d), F32)
        st_spec = pl.BlockSpec((tm, d), lambda i: (i, 0))
    else:
        per = seq // tm
        p_spec = pl.BlockSpec((1, 2, d), lambda i: (i // per, 0, 0))
        st_shape = jax.ShapeDtypeStruct((n // seq, 2, d), F32)
        st_spec = pl.BlockSpec((1, 2, d), lambda i: (i // per, 0, 0))
    return pl.pallas_call(
        functools.partial(_conv_body, seq=seq, tm=tm, rows_prev=rows_prev),
        grid=(n // tm,),
        in_specs=[pl.BlockSpec((tm, d), lambda i: (i, 0)), p_spec,
                  _full((d, 3 * d)), _full((3, d)), _full((d, d)), _full((1, d)), _full((1, d)),
                  _full((d, N_EXPERTS)), _full((1, N_EXPERTS))],
        out_specs=[pl.BlockSpec((tm, d), lambda i: (i, 0)), st_spec,
                   pl.BlockSpec((tm, N_EXPERTS), lambda i: (i, 0))],
        out_shape=[jax.ShapeDtypeStruct((n, d), F32), st_shape,
                   jax.ShapeDtypeStruct((n, N_EXPERTS), F32)],
        scratch_shapes=[pltpu.VMEM((8, d), F32)],
        compiler_params=_params(("arbitrary",)),
        name="conv_mixer",
    )(x, prev, w_in.astype(BF16), w_conv, w_out.astype(BF16), lng, lnb, wr, br)


def _moe_body(te_ref, nv_ref, nx_ref, x_ref, w1_hbm, b1_ref, w2_hbm, b2_ref, o_ref, w1b, w2b, w1s, w2s, sem,
              *, layer):
    i = pl.program_id(0)
    e = te_ref[i]
    prev = te_ref[jnp.maximum(i - 1, 0)]
    valid = i < nv_ref[0]

    def weight_copies(expert):
        return (pltpu.make_async_copy(w1_hbm.at[layer, expert], w1s, sem.at[0]),
                pltpu.make_async_copy(w2_hbm.at[layer, expert], w2s, sem.at[1]))

    @pl.when(i == 0)
    def _():
        for cp in weight_copies(e):
            cp.start()

    @pl.when(valid & ((i == 0) | (e != prev)))
    def _():
        for cp in weight_copies(e):
            cp.wait()
        w1b[...] = w1s[...].astype(BF16)
        w2b[...] = w2s[...].astype(BF16)
        nxt = nx_ref[i]

        @pl.when(nxt >= 0)
        def _():
            for cp in weight_copies(nxt):
                cp.start()

    @pl.when(valid)
    def _():
        hgu = jnp.dot(x_ref[...].astype(BF16), w1b[...], preferred_element_type=F32) + b1_ref[0, 0]
        g = jnp.minimum(hgu[:, :D_FF], SWIGLU_LIMIT)
        u = jnp.clip(hgu[:, D_FF:], -SWIGLU_LIMIT, SWIGLU_LIMIT)
        a = (u + 1.0) * (g * jax.nn.sigmoid(SWIGLU_ALPHA * g))
        o_ref[...] = jnp.dot(a.astype(BF16), w2b[...], preferred_element_type=F32) + b2_ref[0, 0]

    @pl.when(jnp.logical_not(valid))
    def _():
        o_ref[...] = jnp.zeros_like(o_ref)


def _moe_experts(xb, tile_e, n_valid, next_e, w1, b1, w2, b2, *, layer, tm):
    n_rows, d = xb.shape
    n_tiles = n_rows // tm
    bias = lambda w: pl.BlockSpec((1, 1, 1, w), lambda i, te, nv, nx: (layer, te[i], 0, 0))
    grid_spec = pltpu.PrefetchScalarGridSpec(
        num_scalar_prefetch=3,
        grid=(n_tiles,),
        in_specs=[pl.BlockSpec((tm, d), lambda i, te, nv, nx: (i, 0)),
                  pl.BlockSpec(memory_space=pl.ANY), bias(2 * D_FF),
                  pl.BlockSpec(memory_space=pl.ANY), bias(d)],
        out_specs=pl.BlockSpec((tm, d), lambda i, te, nv, nx: (i, 0)),
        scratch_shapes=[pltpu.VMEM((d, 2 * D_FF), BF16), pltpu.VMEM((D_FF, d), BF16),
                        pltpu.VMEM((d, 2 * D_FF), F32), pltpu.VMEM((D_FF, d), F32),
                        pltpu.SemaphoreType.DMA((2,))],
    )
    return pl.pallas_call(
        functools.partial(_moe_body, layer=layer), grid_spec=grid_spec,
        out_shape=jax.ShapeDtypeStruct((n_rows, d), F32),
        compiler_params=_params(("arbitrary",)),
        name="moe_experts",
    )(tile_e, n_valid, next_e, xb, w1, b1[:, :, None, :], w2, b2[:, :, None, :])


def _combine_body(h_ref, yg_ref, gate_ref, lng_ref, lnb_ref, o_ref):
    gate = gate_ref[...]
    f = gate[:, 0:1] * yg_ref[0]
    for k in range(1, TOP_K):
        f = f + gate[:, k:k + 1] * yg_ref[k]
    o_ref[...] = _ln(ALPHA * h_ref[...] + f, lng_ref[...], lnb_ref[...])


def _moe_combine(h, yg, gate, lng, lnb, *, tm):
    n, d = h.shape
    return pl.pallas_call(
        _combine_body, grid=(n // tm,),
        in_specs=[pl.BlockSpec((tm, d), lambda i: (i, 0)),
                  pl.BlockSpec((TOP_K, tm, d), lambda i: (0, i, 0)),
                  pl.BlockSpec((tm, TOP_K), lambda i: (i, 0)),
                  _full((1, d)), _full((1, d))],
        out_specs=pl.BlockSpec((tm, d), lambda i: (i, 0)),
        out_shape=jax.ShapeDtypeStruct((n, d), F32),
        compiler_params=_params(("arbitrary",)),
        name="moe_combine",
    )(h, yg, gate, lng, lnb)


def _route_body(lg_ref, e_ref, g_ref, r_ref, cnt_ref, carry, *, tm):
    i = pl.program_id(0)

    @pl.when(i == 0)
    def _():
        carry[...] = jnp.zeros_like(carry)

    lg = lg_ref[...]
    lane = lax.broadcasted_iota(jnp.int32, lg.shape, 1).astype(F32)
    picks, vals = [], []
    for _ in range(TOP_K):
        m = jnp.max(lg, axis=-1, keepdims=True)
        idx = jnp.min(jnp.where(lg == m, lane, float(N_EXPERTS)), axis=-1, keepdims=True)
        hit = lane == idx
        picks.append((idx.astype(jnp.int32), hit))
        vals.append(m)
        lg = jnp.where(hit, -jnp.inf, lg)
    ex = [jnp.exp(v - vals[0]) for v in vals]
    den = ex[0] + ex[1] + ex[2] + ex[3]
    chosen = picks[0][1] | picks[1][1] | picks[2][1] | picks[3][1]
    onehot = chosen.astype(BF16)
    before = (lax.broadcasted_iota(jnp.int32, (tm, tm), 1)
              < lax.broadcasted_iota(jnp.int32, (tm, tm), 0)).astype(BF16)
    prior = jnp.dot(before, onehot, preferred_element_type=F32) + carry[...]
    for k, (idx, hit) in enumerate(picks):
        e_ref[:, k:k + 1] = idx
        g_ref[:, k:k + 1] = ex[k] / den
        r_ref[:, k:k + 1] = jnp.sum(jnp.where(hit, prior, 0.0), axis=-1, keepdims=True).astype(jnp.int32)
    carry[...] = carry[...] + jnp.sum(chosen.astype(F32), axis=0, keepdims=True)
    cnt_ref[...] = carry[...]


def _route_topk(logits, *, tm):
    n, ne = logits.shape
    col = lambda dt: (pl.BlockSpec((tm, TOP_K), lambda i: (i, 0)), jax.ShapeDtypeStruct((n, TOP_K), dt))
    specs, shapes = zip(col(jnp.int32), col(F32), col(jnp.int32),
                        (_full((1, ne)), jax.ShapeDtypeStruct((1, ne), F32)))
    return pl.pallas_call(
        functools.partial(_route_body, tm=tm), grid=(n // tm,),
        in_specs=[pl.BlockSpec((tm, ne), lambda i: (i, 0))],
        out_specs=list(specs), out_shape=list(shapes),
        scratch_shapes=[pltpu.VMEM((1, ne), F32)],
        compiler_params=_params(("arbitrary",)),
        name="moe_route",
    )(logits)


def _moe_route(h, logits, *, tm_rows):
    n, d = h.shape
    top_e, gate, rank, counts = _route_topk(logits, tm=min(n, 512))
    flat_e = top_e.reshape(-1)
    rank = rank.reshape(-1)
    counts = counts.reshape(-1).astype(jnp.int32)
    nk = n * TOP_K
    padded = (counts + tm_rows - 1) // tm_rows * tm_rows
    pad_end = jnp.cumsum(padded)
    pad_start = pad_end - padded
    dest = (pad_start[flat_e] + rank).astype(jnp.int32)
    n_tiles = (nk + N_EXPERTS * (tm_rows - 1) + tm_rows - 1) // tm_rows
    n_rows = n_tiles * tm_rows
    tile_start = jnp.arange(n_tiles, dtype=jnp.int32) * tm_rows
    tile_e = jnp.minimum(jnp.sum((pad_end[None, :] <= tile_start[:, None]).astype(jnp.int32), axis=1), N_EXPERTS - 1)
    n_valid = (pad_end[-1] // tm_rows).astype(jnp.int32).reshape(1)
    _, tok_sorted = lax.sort_key_val(dest, jnp.arange(nk, dtype=jnp.int32) // TOP_K)
    seg_end = jnp.cumsum(counts)
    shift = pad_start - (seg_end - counts)
    tile_ok = tile_start < pad_end[-1]
    last = jnp.where(tile_ok, seg_end[tile_e] - 1, nk - 1)
    src = jnp.minimum(jnp.arange(n_rows, dtype=jnp.int32) - jnp.repeat(shift[tile_e], tm_rows),
                      jnp.repeat(last, tm_rows))
    row_tok = tok_sorted.at[src].get(indices_are_sorted=True, mode='promise_in_bounds')
    xb = h[row_tok]
    eidx = jnp.arange(N_EXPERTS, dtype=jnp.int32)
    later = jnp.where((eidx[None, :] > eidx[:, None]) & (counts[None, :] > 0), eidx[None, :], N_EXPERTS)
    nxt_tab = jnp.min(later, axis=1)
    next_e = jnp.where(nxt_tab == N_EXPERTS, -1, nxt_tab).astype(jnp.int32)[tile_e]
    return xb, (tile_e, n_valid, next_e), dest.reshape(n, TOP_K).T, gate


def _moe_apply(h, routed, w1, b1, w2, b2, lng, lnb, *, layer, tm_rows, tm_tok):
    xb, tiles, dest_t, gate = routed
    yb = _moe_experts(xb, *tiles, w1, b1, w2, b2, layer=layer, tm=tm_rows)
    yg = yb[dest_t]
    return _moe_combine(h, yg, gate, lng, lnb, tm=tm_tok)


def _proj_body(h_ref, w_ref, rc_ref, rs1_ref, rs2_ref, *out_refs, heads_out, seq, q_scale):
    z = jnp.dot(h_ref[...].astype(BF16), w_ref[...], preferred_element_type=F32)
    cc, s1, s2 = rc_ref[...], rs1_ref[...], rs2_ref[...]

    def rope(x):
        return x * cc + pltpu.roll(x, LANES - ROT_DIM // 2, 1) * s1 + pltpu.roll(x, ROT_DIM // 2, 1) * s2

    def rope_cols(lo, hi):
        return [rope(z[:, c:c + LANES]) for c in range(lo, hi, LANES)]

    if heads_out:
        rows_c, rows_s, rows_w, g_ref, q_ref, ks_ref, vs_ref, kw_ref, vw_ref = out_refs
    else:
        rows_c, rows_s, rows_w, g_ref, q_ref = out_refs
    half = KV_COLS // 2
    for br, rows in enumerate((rows_c, rows_s, rows_w)):
        base = br * KV_COLS
        k = jnp.concatenate(rope_cols(base, base + half), axis=1)
        v = z[:, base + half:base + KV_COLS]
        kv = jnp.concatenate([k, v], axis=1)
        if heads_out:
            rows[0] = kv.T
        else:
            rows[...] = kv
        if heads_out and br >= 1:
            tm = k.shape[0]
            t = (pl.program_id(0) % (seq // tm)) * tm + lax.broadcasted_iota(jnp.int32, (tm, HEAD_DIM), 0)
            lane = lax.broadcasted_iota(jnp.int32, (tm, HEAD_DIM), 1)
            blk_onehot = (lane == lax.shift_right_logical(t, 6)).astype(BF16)
            one_lane = (lane == 0).astype(BF16)
            k_ref, v_ref = (ks_ref, vs_ref) if br == 1 else (kw_ref, vw_ref)
            for g in range(N_KV_HEADS):
                kg = k[:, g * HEAD_DIM:(g + 1) * HEAD_DIM].astype(BF16)
                vg = v[:, g * HEAD_DIM:(g + 1) * HEAD_DIM].astype(BF16)
                k_ref[0, g] = jnp.concatenate([kg, blk_onehot], axis=1) if br == 1 else kg
                v_ref[0, g] = jnp.concatenate([vg, one_lane], axis=1)
    qb = 3 * KV_COLS
    q = jnp.concatenate(rope_cols(qb, qb + D_MODEL), axis=1) * q_scale
    if heads_out:
        for hd in range(N_HEADS):
            q_ref[0, hd] = q[:, hd * HEAD_DIM:(hd + 1) * HEAD_DIM].astype(BF16)
    else:
        q_ref[...] = q
    g_ref[...] = jax.nn.sigmoid(z[:, qb + D_MODEL:])


def _rope_tables(pos):
    half = ROT_DIM // 2
    inv = ROPE_THETA ** (-jnp.arange(half, dtype=F32) * 2.0 / ROT_DIM)
    ang = pos.astype(F32)[:, None] * inv[None, :]
    cos, sin = jnp.cos(ang), jnp.sin(ang)
    zeros = jnp.zeros((pos.shape[0], HEAD_DIM - ROT_DIM), F32)
    ones = jnp.ones_like(zeros)
    z8 = jnp.zeros_like(sin)
    c = jnp.concatenate([cos, cos, ones], axis=1)
    s1 = jnp.concatenate([-sin, z8, zeros], axis=1)
    s2 = jnp.concatenate([z8, sin, zeros], axis=1)
    rep = LANES // HEAD_DIM
    return tuple(jnp.tile(a, (1, rep)) for a in (c, s1, s2))


def _proj_weights(kv_w, wq, wg):
    wg4 = wg.reshape(D_MODEL, N_KV_HEADS, HEADS_PER_GROUP * 3)
    wg4 = jnp.pad(wg4, ((0, 0), (0, 0), (0, LANES - HEADS_PER_GROUP * 3))).reshape(D_MODEL, N_KV_HEADS * LANES)
    return jnp.concatenate([kv_w, wq, wg4], axis=1).astype(BF16)


def _nsa_proj(h, w_cat, tables, *, tm, seq, heads_out):
    n, d = h.shape
    ncol = w_cat.shape[1]
    per = tables[0].shape[0] // tm
    tab_spec = pl.BlockSpec((tm, LANES), lambda i: (i % per, 0))
    row_spec = lambda w: pl.BlockSpec((tm, w), lambda i: (i, 0))
    g_spec, g_shape = row_spec(N_KV_HEADS * LANES), jax.ShapeDtypeStruct((n, N_KV_HEADS * LANES), F32)
    if heads_out:
        bsz = n // seq
        tps = seq // tm
        out_specs = [pl.BlockSpec((1, KV_COLS, tm), lambda i: (i // tps, 0, i % tps))] * 3 + [g_spec]
        out_shape = [jax.ShapeDtypeStruct((bsz, KV_COLS, seq), F32)] * 3 + [g_shape]
        assert seq // SEL_BLK <= HEAD_DIM
        hspec = lambda nh, w: pl.BlockSpec((1, nh, tm, w), lambda i: (i // tps, 0, i % tps, 0))
        hshape = lambda nh, w: jax.ShapeDtypeStruct((bsz, nh, seq, w), BF16)
        widths = (2 * HEAD_DIM, 2 * HEAD_DIM, HEAD_DIM, 2 * HEAD_DIM)
        out_specs += [hspec(N_HEADS, HEAD_DIM)] + [hspec(N_KV_HEADS, w) for w in widths]
        out_shape += [hshape(N_HEADS, HEAD_DIM)] + [hshape(N_KV_HEADS, w) for w in widths]
    else:
        out_specs = [row_spec(KV_COLS)] * 3 + [g_spec, row_spec(d)]
        out_shape = [jax.ShapeDtypeStruct((n, KV_COLS), F32)] * 3 + [g_shape, jax.ShapeDtypeStruct((n, d), F32)]
    q_scale = HEAD_DIM ** -0.5 * (float(np.log2(np.e)) if heads_out else 1.0)
    return pl.pallas_call(
        functools.partial(_proj_body, heads_out=heads_out, seq=seq, q_scale=q_scale), grid=(n // tm,),
        in_specs=[row_spec(d), _full((d, ncol)), tab_spec, tab_spec, tab_spec],
        out_specs=out_specs, out_shape=out_shape,
        compiler_params=_params(("arbitrary",)),
        name="nsa_proj",
    )(h, w_cat, *tables)


def _cmp_math(xt, w1_ref, pe_ref, w2_ref, o_ref, nch):
    k = pl.program_id(1) // 2
    x = jnp.concatenate(
        [xt[pl.ds(c, nch, stride=CMP_STRIDE), :].astype(BF16) for c in range(CMP_STRIDE)],
        axis=1)
    acc = jnp.dot(x, w1_ref[0], preferred_element_type=F32)
    pe = pe_ref[pl.ds(k, 1), :]
    hid = []
    for gs in range(2):
        pa = acc[:, gs * 2 * CMP_HID:gs * 2 * CMP_HID + CMP_HID]
        pb = acc[:, gs * 2 * CMP_HID + CMP_HID:(gs + 1) * 2 * CMP_HID]
        pb_next = pltpu.roll(pb, nch - 1, 0)
        hid.append(jax.nn.gelu(pa + pb_next + pe))
    hid = jnp.concatenate(hid, axis=1).astype(BF16)
    o_ref[0] = jnp.dot(hid, w2_ref[0], preferred_element_type=F32)


def _cmp_t_body(r_ref, w1_ref, pe_ref, w2_ref, o_ref, xt, *, nch):
    for p in range(r_ref.shape[2] // LANES):
        xt[p * LANES:(p + 1) * LANES, :] = r_ref[0, :, p * LANES:(p + 1) * LANES].T
    _cmp_math(xt, w1_ref, pe_ref, w2_ref, o_ref, nch)


def _cmp_paged_body(pt_ref, cache_ref, w1_ref, pe_ref, w2_ref, o_ref, pbuf, xt, sem, *, nch, n_pages):
    ncol = pl.num_programs(1)
    step = pl.program_id(0) * ncol + pl.program_id(1)
    nsteps = pl.num_programs(0) * ncol
    slot = step & 1

    def page_copy(st, p, sl):
        bb = st // ncol
        col = pl.multiple_of((st - bb * ncol) * LANES, LANES)
        return pltpu.make_async_copy(cache_ref.at[pt_ref[bb * n_pages + p], pl.ds(col, LANES), :],
                                     pbuf.at[sl, p], sem.at[sl])

    @pl.when(step == 0)
    def _():
        for p in range(n_pages):
            page_copy(step, p, slot).start()

    @pl.when(step + 1 < nsteps)
    def _():
        for p in range(n_pages):
            page_copy(step + 1, p, 1 - slot).start()

    for p in range(n_pages):
        page_copy(step, p, slot).wait()
    for p in range(n_pages):
        xt[p * PAGE_SIZE:(p + 1) * PAGE_SIZE, :] = pbuf[slot, p].T
    _cmp_math(xt, w1_ref, pe_ref, w2_ref, o_ref, nch)


def _cmp_weights(cmp_pe, cmp_w1, cmp_b1, cmp_w2):
    eye = jnp.eye(2, dtype=F32)
    w1r = cmp_w1.reshape(2, 2, CMP_STRIDE, HEAD_DIM, CMP_HID)
    w1bd = jnp.einsum('ab,kncdh->kcadbnh', eye, w1r).reshape(2, CMP_STRIDE * 2 * HEAD_DIM, 4 * CMP_HID)
    w2bd = jnp.einsum('ab,khd->kahbd', eye, cmp_w2).reshape(2, 2 * CMP_HID, 2 * HEAD_DIM)
    pe_term = jnp.einsum('kcd,kcdh->kh', cmp_pe, cmp_w1, precision=HIGHEST) + cmp_b1
    return w1bd.astype(BF16), pe_term, w2bd.astype(BF16)


def _cmp_specs(cw, nch, imap):
    w1bd, pe_term, w2bd = cw
    return ([pl.BlockSpec((1,) + w1bd.shape[1:], imap(lambda b, j: (j // 2, 0, 0))),
             pl.BlockSpec(pe_term.shape, imap(lambda b, j: (0, 0))),
             pl.BlockSpec((1,) + w2bd.shape[1:], imap(lambda b, j: (j // 2, 0, 0)))],
            pl.BlockSpec((1, nch, LANES), imap(lambda b, j: (b, 0, j))))


def _compress_t(rows_t, cw):
    bsz, _, t = rows_t.shape
    nch = t // CMP_STRIDE
    w_specs, o_spec = _cmp_specs(cw, nch, lambda f: f)
    return pl.pallas_call(
        functools.partial(_cmp_t_body, nch=nch), grid=(bsz, KV_COLS // LANES),
        in_specs=[pl.BlockSpec((1, LANES, t), lambda b, j: (b, j, 0))] + w_specs,
        out_specs=o_spec,
        out_shape=jax.ShapeDtypeStruct((bsz, nch, KV_COLS), F32),
        scratch_shapes=[pltpu.VMEM((t, LANES), F32)],
        compiler_params=_params(("arbitrary", "arbitrary")),
        name="compress_kv",
    )(rows_t, *cw)


def _compress_paged(cache_t, page_table, cw):
    bsz, n_pages = page_table.shape
    nch = n_pages * PAGE_SIZE // CMP_STRIDE
    w_specs, o_spec = _cmp_specs(cw, nch, lambda f: (lambda b, j, pt: f(b, j)))
    grid_spec = pltpu.PrefetchScalarGridSpec(
        num_scalar_prefetch=1, grid=(bsz, KV_COLS // LANES),
        in_specs=[pl.BlockSpec(memory_space=pl.ANY)] + w_specs,
        out_specs=o_spec,
        scratch_shapes=[pltpu.VMEM((2, n_pages, LANES, PAGE_SIZE), F32),
                        pltpu.VMEM((n_pages * PAGE_SIZE, LANES), F32),
                        pltpu.SemaphoreType.DMA((2,))])
    return pl.pallas_call(
        functools.partial(_cmp_paged_body, nch=nch, n_pages=n_pages), grid_spec=grid_spec,
        out_shape=jax.ShapeDtypeStruct((bsz, nch, KV_COLS), F32),
        compiler_params=_params(("arbitrary", "arbitrary")),
        name="compress_kv_paged",
    )(page_table.reshape(-1), cache_t, *cw)


def _softmax_probs(s, m, exp_fn=jnp.exp):
    s = jnp.where(m, s, NEG)
    mx = jnp.max(s, axis=-1, keepdims=True)
    e = jnp.where(m, exp_fn(s - mx), 0.0)
    den = jnp.sum(e, axis=-1, keepdims=True)
    return e / jnp.maximum(den, 1e-30)


def _select_blocks(imp_t, qpos_row, n_rows):
    lanes = imp_t.shape[1]
    j = lax.broadcasted_iota(jnp.int32, (n_rows, lanes), 0)
    cur = lax.shift_right_logical(qpos_row, 6)
    valid = j <= cur
    forced = (j == 0) | (j == cur) | (j == cur - 1)
    sc = jnp.where(valid, jnp.where(forced, FORCE, imp_t[:n_rows]), NEG)
    return sc, j, valid


def _overlap_t(n_sel_pad, n_cmp_pad):
    c0 = np.arange(n_cmp_pad)[None, :] * CMP_STRIDE
    s0 = np.arange(n_sel_pad)[:, None] * SEL_BLK
    ov = np.clip(np.minimum(c0 + CMP_BLK, s0 + SEL_BLK) - np.maximum(c0, s0), 0, None).astype(np.float32) / CMP_BLK
    return ov


def _attn_p_body(q_ref, kck_ref, kcv_ref, ks_ref, vs_ref, kw_ref, vw_ref, g_ref, h_ref,
                 ovt_ref, wo_ref, lng_ref, lnb_ref, wr_ref, br_ref,
                 o_ref, lg_ref, o_scr, *, tq, seq, kc):
    i = pl.program_id(1)
    g = pl.program_id(2)
    t0 = i * tq
    rows = HEADS_PER_GROUP * tq
    n_sel = seq // SEL_BLK
    q = q_ref[0].reshape(rows, HEAD_DIM)
    tpos = t0 + (lax.broadcasted_iota(jnp.int32, (rows, 1), 0) & (tq - 1))

    n_cmp = kck_ref.shape[2]
    s = _nt(q, kck_ref[0, 0])
    cend = lax.broadcasted_iota(jnp.int32, (1, n_cmp), 1) * CMP_STRIDE + (CMP_BLK - 1)
    p_c = _softmax_probs(s, cend <= tpos, jnp.exp2)
    o_c = jnp.dot(p_c.astype(BF16), kcv_ref[0, 0], preferred_element_type=F32)
    psum = p_c[0:tq]
    for qh in range(1, HEADS_PER_GROUP):
        psum = psum + p_c[qh * tq:(qh + 1) * tq]
    imp_t = lax.dot_general(ovt_ref[...], psum, (((1,), (1,)), ((), ())),
                            precision=HIGHEST, preferred_element_type=F32)

    qrow = t0 + lax.broadcasted_iota(jnp.int32, (1, tq), 1)
    sc, j, valid = _select_blocks(imp_t, qrow, n_sel)
    rank = jnp.zeros((n_sel, tq), jnp.int32)
    for jp in range(n_sel):
        r = sc[jp:jp + 1, :]
        beats = (r > sc) | ((r == sc) & (j > jp))
        rank = rank + beats.astype(jnp.int32)
    sel_t = (valid & (rank < SEL_TOPN)).astype(F32)
    sel_t = jnp.concatenate([sel_t, jnp.zeros((LANES - n_sel, tq), F32)], axis=0)
    sel = sel_t.T[:, 0:HEAD_DIM]
    blk_lane = lax.broadcasted_iota(jnp.int32, (1, HEAD_DIM), 1)
    r_i = lax.broadcasted_iota(jnp.int32, (tq, 1), 0)
    tq_pos = t0 + r_i
    t0a = pl.multiple_of(t0, tq)
    causal = jnp.where(lax.broadcasted_iota(jnp.int32, (1, tq), 1) <= r_i, 0.0, NEG)
    causal = jnp.concatenate([causal] * HEADS_PER_GROUP, axis=0)

    s = _nt(q, ks_ref[0, 0, pl.ds(t0a, tq), 0:HEAD_DIM]) + causal
    m0 = jnp.max(s, axis=-1, keepdims=True)
    acc0 = jnp.dot(jnp.exp2(s - m0).astype(BF16), vs_ref[0, 0, pl.ds(t0a, tq), :], preferred_element_type=F32)
    sweep = jnp.where((sel > 0.5) & (blk_lane < lax.shift_right_logical(t0, 6)), 0.0, NEG).astype(BF16)
    q_aug = jnp.concatenate([q, jnp.concatenate([sweep] * HEADS_PER_GROUP, axis=0)], axis=1)

    def sweep_chunk(c, carry):
        m_run, acc = carry
        k0 = pl.multiple_of(c * kc, kc)
        s = _nt(q_aug, ks_ref[0, 0, pl.ds(k0, kc), :])
        m_new = jnp.maximum(m_run, jnp.max(s, axis=-1, keepdims=True))
        pv = jnp.dot(jnp.exp2(s - m_new).astype(BF16), vs_ref[0, 0, pl.ds(k0, kc), :],
                     preferred_element_type=F32)
        return m_new, jnp.exp2(m_run - m_new) * acc + pv

    _, acc = lax.fori_loop(0, (t0 + kc - 1) // kc, sweep_chunk, (m0, acc0))
    o_s = acc[:, 0:HEAD_DIM] / acc[:, HEAD_DIM:HEAD_DIM + 1]

    ws = pl.multiple_of(jnp.maximum(t0 - WINDOW, 0), tq)
    wi = ws + lax.broadcasted_iota(jnp.int32, (1, WINDOW), 1)
    wbias = jnp.where((wi > tq_pos - WINDOW) & (wi < t0), 0.0, NEG)
    s_d = _nt(q, kw_ref[0, 0, pl.ds(t0a, tq), :]) + causal
    s_w = _nt(q, kw_ref[0, 0, pl.ds(ws, WINDOW), :]) + jnp.concatenate([wbias] * HEADS_PER_GROUP, axis=0)
    m = jnp.maximum(jnp.max(s_d, axis=-1, keepdims=True), jnp.max(s_w, axis=-1, keepdims=True))
    acc = (jnp.dot(jnp.exp2(s_d - m).astype(BF16), vw_ref[0, 0, pl.ds(t0a, tq), :], preferred_element_type=F32)
           + jnp.dot(jnp.exp2(s_w - m).astype(BF16), vw_ref[0, 0, pl.ds(ws, WINDOW), :],
                     preferred_element_type=F32))
    o_w = acc[:, 0:HEAD_DIM] / acc[:, HEAD_DIM:HEAD_DIM + 1]

    gates = g_ref[...]
    for qh in range(HEADS_PER_GROUP):
        sl = slice(qh * tq, (qh + 1) * tq)
        o_h = (gates[:, 3 * qh:3 * qh + 1] * o_c[sl] + gates[:, 3 * qh + 1:3 * qh + 2] * o_s[sl]
               + gates[:, 3 * qh + 2:3 * qh + 3] * o_w[sl])
        o_scr[g, :, qh * HEAD_DIM:(qh + 1) * HEAD_DIM] = o_h

    @pl.when(g == N_KV_HEADS - 1)
    def _():
        o = jnp.concatenate([o_scr[gg] for gg in range(N_KV_HEADS)], axis=1).astype(BF16)
        y = jnp.dot(o, wo_ref[...], preferred_element_type=F32)
        h = _ln(ALPHA * h_ref[...] + y, lng_ref[...], lnb_ref[...])
        o_ref[...] = h
        lg_ref[...] = _router(h, wr_ref, br_ref)


def _nsa_prompt(h, qh, kck, kcv, ks, vs, kw, vw, gates, wo, lng, lnb, wr, br, *, tq, kc):
    n, d = h.shape
    bsz, _, seq, _ = ks.shape
    nt = seq // tq
    n_cmp = kck.shape[2]
    ovt = jnp.asarray(_overlap_t(LANES, n_cmp))
    assert kc % tq == 0 and seq % kc == 0 and SEL_BLK % 64 == 0 and tq % SEL_BLK == 0
    grp = lambda a: pl.BlockSpec((1, 1) + a.shape[2:], lambda b, i, g: (b, g, 0, 0))
    row = lambda w: pl.BlockSpec((tq, w), lambda b, i, g: (b * nt + i, 0))
    return pl.pallas_call(
        functools.partial(_attn_p_body, tq=tq, seq=seq, kc=kc),
        grid=(bsz, nt, N_KV_HEADS),
        in_specs=[pl.BlockSpec((1, HEADS_PER_GROUP, tq, HEAD_DIM), lambda b, i, g: (b, g, i, 0)),
                  grp(kck), grp(kcv), grp(ks), grp(vs), grp(kw), grp(vw),
                  pl.BlockSpec((tq, LANES), lambda b, i, g: (b * nt + i, g)), row(d),
                  _full(ovt.shape), _full((d, d)), _full((1, d)), _full((1, d)),
                  _full((d, N_EXPERTS)), _full((1, N_EXPERTS))],
        out_specs=[row(d), row(N_EXPERTS)],
        out_shape=[jax.ShapeDtypeStruct((n, d), F32), jax.ShapeDtypeStruct((n, N_EXPERTS), F32)],
        scratch_shapes=[pltpu.VMEM((N_KV_HEADS, tq, HEADS_PER_GROUP * HEAD_DIM), F32)],
        compiler_params=_params(("arbitrary", "arbitrary", "arbitrary")),
        name="nsa_prompt",
    )(qh, kck, kcv, ks, vs, kw, vw, gates, h, ovt, wo.astype(BF16), lng, lnb, wr, br)


def _attn_s_body(pt_ref, q_ref, gt_ref, kvc_ref, cache_ref, ns_ref, cw_ref, nw_ref, ov_ref, exp_ref,
                 o_ref, kvbuf, sem, msk_scr, m_scr, l_scr, acc_scr, ocw_scr, *, dec, past, ppc):
    nrow = q_ref.shape[1]
    half = KV_COLS // 2
    c = pl.program_id(1)
    nc = pl.num_programs(1)
    step = pl.program_id(0) * nc + c
    nsteps = pl.num_programs(0) * nc
    slot = step & 1

    def chunk_copies(st, sl):
        return [pltpu.make_async_copy(cache_ref.at[pt_ref[st * ppc + p]],
                                      kvbuf.at[sl, :, pl.ds(p * PAGE_SIZE, PAGE_SIZE)], sem.at[sl])
                for p in range(ppc)]

    @pl.when(step == 0)
    def _():
        for cp in chunk_copies(step, slot):
            cp.start()

    @pl.when(step + 1 < nsteps)
    def _():
        for cp in chunk_copies(step + 1, 1 - slot):
            cp.start()

    q = q_ref[0]
    row = lax.broadcasted_iota(jnp.int32, (nrow, 1), 0)
    tok = row & (dec - 1)
    gsel = lax.shift_right_logical(row, 2) & (N_KV_HEADS - 1)

    def pick(o):
        out = jnp.zeros((nrow, HEAD_DIM), F32)
        for gg in range(N_KV_HEADS):
            out = out + jnp.where(gsel == gg, o[:, gg * HEAD_DIM:(gg + 1) * HEAD_DIM], 0.0)
        return out

    def new_rows(new_ref):
        k_new = new_ref[0, :, 0:half].astype(BF16)
        m_new = lax.broadcasted_iota(jnp.int32, (1, new_ref.shape[1]), 1) <= tok
        return jnp.where(m_new, _nt(q, k_new), NEG), m_new, new_ref[0, :, half:KV_COLS].astype(BF16)

    @pl.when(c == 0)
    def _():
        qpos = past + tok
        n_cmp = kvc_ref.shape[1]
        kc = kvc_ref[0, :, 0:half].astype(BF16)
        vc = kvc_ref[0, :, half:KV_COLS].astype(BF16)
        cend = lax.broadcasted_iota(jnp.int32, (1, n_cmp), 1) * CMP_STRIDE + (CMP_BLK - 1)
        p_c = _softmax_probs(_nt(q, kc), cend <= qpos)
        ocw_scr[0] = pick(jnp.dot(p_c.astype(BF16), vc, preferred_element_type=F32))
        ngt = N_KV_HEADS * dec
        psum = p_c[0:ngt]
        for qh in range(1, HEADS_PER_GROUP):
            psum = psum + p_c[qh * ngt:(qh + 1) * ngt]
        imp = jnp.dot(psum, ov_ref[...], precision=HIGHEST, preferred_element_type=F32)

        n_sel = past // SEL_BLK + 1
        j = lax.broadcasted_iota(jnp.int32, imp.shape, 1)
        cur = lax.shift_right_logical(past + (lax.broadcasted_iota(jnp.int32, (ngt, 1), 0) & (dec - 1)), 6)
        valid = j <= cur
        forced = (j == 0) | (j == cur) | (j == cur - 1)
        sc = jnp.where(valid, jnp.where(forced, FORCE, imp), NEG)
        rank = jnp.zeros(imp.shape, jnp.int32)
        for jp in range(n_sel):
            r = sc[:, jp:jp + 1]
            rank = rank + ((r > sc) | ((r == sc) & (j > jp))).astype(jnp.int32)
        mrow = jnp.where(valid & (rank < SEL_TOPN), 0.0, NEG)[:, 0:LANES]
        msk_scr[...] = jnp.concatenate([mrow] * HEADS_PER_GROUP, axis=0).astype(BF16)

        s_n, m_n, v_n = new_rows(ns_ref)
        m0 = jnp.max(s_n, axis=-1, keepdims=True)
        e_n = jnp.where(m_n, jnp.exp(s_n - m0), 0.0)
        m_scr[...] = m0
        l_scr[...] = jnp.sum(e_n, axis=-1, keepdims=True)
        acc_scr[...] = jnp.dot(e_n.astype(BF16), v_n, preferred_element_type=F32)

        n_win = cw_ref.shape[2]
        m_w = lax.broadcasted_iota(jnp.int32, (1, n_win), 1) > tok + (n_win - WINDOW)
        s_o = jnp.where(m_w, jnp.dot(q, cw_ref[0, 0:half, :].astype(BF16), preferred_element_type=F32), NEG)
        s_n, m_n, v_n = new_rows(nw_ref)
        mx = jnp.maximum(jnp.max(s_o, axis=-1, keepdims=True), jnp.max(s_n, axis=-1, keepdims=True))
        e_o = jnp.where(m_w, jnp.exp(s_o - mx), 0.0)
        e_n = jnp.where(m_n, jnp.exp(s_n - mx), 0.0)
        den = jnp.sum(e_o, axis=-1, keepdims=True) + jnp.sum(e_n, axis=-1, keepdims=True)
        o_w = (_nt(e_o.astype(BF16), cw_ref[0, half:KV_COLS, :].astype(BF16))
               + jnp.dot(e_n.astype(BF16), v_n, preferred_element_type=F32))
        ocw_scr[1] = pick(o_w / den)

    for cp in chunk_copies(step, slot):
        cp.wait()
    kt = kvbuf[slot, 0:half, :].astype(BF16)
    vt = kvbuf[slot, half:KV_COLS, :].astype(BF16)
    s = (jnp.dot(q, kt, preferred_element_type=F32)
         + jnp.dot(msk_scr[...], exp_ref[c], preferred_element_type=F32))
    m_old = m_scr[...]
    m_new = jnp.maximum(m_old, jnp.max(s, axis=-1, keepdims=True))
    alpha = jnp.exp(m_old - m_new)
    e = jnp.exp(s - m_new)
    m_scr[...] = m_new
    l_scr[...] = alpha * l_scr[...] + jnp.sum(e, axis=-1, keepdims=True)
    acc_scr[...] = alpha * acc_scr[...] + _nt(e.astype(BF16), vt)

    @pl.when(c == nc - 1)
    def _():
        gt = gt_ref[0]
        o_s = pick(acc_scr[...] / l_scr[...])
        o_ref[0] = gt[:, 0:1] * ocw_scr[0] + gt[:, 1:2] * o_s + gt[:, 2:3] * ocw_scr[1]


def _nsa_sample(qbd, gt, kvc, cache_t, page_table, new_s, cache_w_t, new_w, *, dec, past, ppc):
    bsz, nrow, _ = qbd.shape
    n_pages = past // PAGE_SIZE
    assert n_pages % ppc == 0 and past // SEL_BLK <= LANES
    nc = n_pages // ppc
    ck = ppc * PAGE_SIZE
    ovt = jnp.asarray(_overlap_t(2 * LANES, kvc.shape[1]).T)
    kblk = (np.arange(past) // SEL_BLK).reshape(nc, 1, ck)
    expand = jnp.asarray(np.arange(LANES)[None, :, None] == kblk, BF16)
    assert past // SEL_BLK + 1 <= 2 * LANES
    b3 = lambda a: pl.BlockSpec((1,) + a.shape[1:], lambda b, c, pt: (b, 0, 0))
    full = lambda a: pl.BlockSpec(a.shape, lambda b, c, pt: (0,) * a.ndim)
    grid_spec = pltpu.PrefetchScalarGridSpec(
        num_scalar_prefetch=1, grid=(bsz, nc),
        in_specs=[b3(qbd), b3(gt), b3(kvc), pl.BlockSpec(memory_space=pl.ANY), b3(new_s), b3(cache_w_t),
                  b3(new_w), full(ovt), full(expand)],
        out_specs=pl.BlockSpec((1, nrow, HEAD_DIM), lambda b, c, pt: (b, 0, 0)),
        scratch_shapes=[pltpu.VMEM((2, KV_COLS, ck), F32),
                        pltpu.SemaphoreType.DMA((2,)),
                        pltpu.VMEM((nrow, LANES), BF16),
                        pltpu.VMEM((nrow, 1), F32), pltpu.VMEM((nrow, 1), F32),
                        pltpu.VMEM((nrow, KV_COLS // 2), F32),
                        pltpu.VMEM((2, nrow, HEAD_DIM), F32)])
    return pl.pallas_call(
        functools.partial(_attn_s_body, dec=dec, past=past, ppc=ppc), grid_spec=grid_spec,
        out_shape=jax.ShapeDtypeStruct((bsz, nrow, HEAD_DIM), F32),
        compiler_params=_params(("arbitrary", "arbitrary")),
        name="nsa_sample",
    )(page_table.reshape(-1), qbd, gt, kvc, cache_t, new_s, cache_w_t, new_w, ovt, expand)


def _out_body(o_ref, h_ref, wo_ref, lng_ref, lnb_ref, wr_ref, br_ref, y_ref, lg_ref):
    y = jnp.dot(o_ref[...].astype(BF16), wo_ref[...], preferred_element_type=F32)
    h = _ln(ALPHA * h_ref[...] + y, lng_ref[...], lnb_ref[...])
    y_ref[...] = h
    lg_ref[...] = _router(h, wr_ref, br_ref)


def _out_proj(o, h, wo, lng, lnb, wr, br):
    n, d = h.shape
    return pl.pallas_call(
        _out_body, grid=(1,),
        in_specs=[_full((n, d)), _full((n, d)), _full((d, d)), _full((1, d)), _full((1, d)),
                  _full((d, N_EXPERTS)), _full((1, N_EXPERTS))],
        out_specs=[_full((n, d)), _full((n, N_EXPERTS))],
        out_shape=[jax.ShapeDtypeStruct((n, d), F32), jax.ShapeDtypeStruct((n, N_EXPERTS), F32)],
        compiler_params=_params(("arbitrary",)),
        name="nsa_out_proj",
    )(o, h, wo.astype(BF16), lng, lnb, wr, br)


def kernel(x_prompt, x_sample, state_conv, cache_kv_cmp, cache_kv_sel, cache_kv_win, page_table,
           conv_w_in, conv_w, conv_w_out, kv_w, cmp_pe, cmp_w1, cmp_b1, cmp_w2,
           nsa_wq, nsa_wg, nsa_wo, moe_wr, moe_br, moe_w1, moe_b1, moe_w2, moe_b2, ln_g, ln_b):
    bp, sp, d = x_prompt.shape
    bd, sd, _ = x_sample.shape
    kv_shape = (2, N_KV_HEADS, HEAD_DIM)
    lng = lambda l, s: ln_g[l, s][None, :]
    lnb = lambda l, s: ln_b[l, s][None, :]
    moe = lambda l, h, routed, **kw: _moe_apply(h, routed, moe_w1, moe_b1, moe_w2, moe_b2,
                                                lng(l, 1), lnb(l, 1), layer=l, **kw)
    w_cat = _proj_weights(kv_w, nsa_wq[0], nsa_wg[0])
    cw = _cmp_weights(cmp_pe, cmp_w1, cmp_b1, cmp_w2)
    br = lambda l: moe_br[l][None, :]

    n_p = bp * sp
    xp = x_prompt.reshape(n_p, d)
    h, conv_p, lg = _conv_layer(xp, jnp.zeros((bp, 2, d), F32), conv_w_in[0], conv_w[0], conv_w_out[0],
                                lng(0, 0), lnb(0, 0), moe_wr[0], br(0), seq=sp, tm=256, rows_prev=False)
    routed_p = _moe_route(h, lg, tm_rows=256)
    n_s = bd * sd
    xs = x_sample.reshape(n_s, d)
    prev_rows = jnp.repeat(state_conv[0].transpose(1, 0, 2), sd, axis=1)
    hs, u_s, lgs = _conv_layer(xs, prev_rows, conv_w_in[0], conv_w[0], conv_w_out[0],
                               lng(0, 0), lnb(0, 0), moe_wr[0], br(0), seq=sd, tm=n_s, rows_prev=True)
    conv_s = u_s.reshape(bd, sd, d)[:, sd - 2:]
    routed_s = _moe_route(hs, lgs, tm_rows=64)
    h = moe(0, h, routed_p, tm_rows=256, tm_tok=256)
    hs = moe(0, hs, routed_s, tm_rows=64, tm_tok=n_s)

    pos_s = PAST_LEN + (jnp.arange(n_s, dtype=jnp.int32) % sd)
    rc_s, rs_s, rw_s, gates_s, q_s = _nsa_proj(hs, w_cat, _rope_tables(pos_s), tm=n_s, seq=sd, heads_out=False)
    cols_major = lambda a: a.transpose(0, 2, 3, 4, 1).reshape(a.shape[0], KV_COLS, a.shape[1])
    kvc_s = _compress_paged(cols_major(cache_kv_cmp), page_table, cw)
    q5 = q_s.reshape(bd, sd, N_KV_HEADS, HEADS_PER_GROUP, HEAD_DIM).transpose(0, 3, 2, 1, 4)
    eye = jnp.eye(N_KV_HEADS, dtype=F32)
    qbd = jnp.einsum('bqgtd,gh->bqgthd', q5, eye).reshape(bd, N_HEADS * sd, N_KV_HEADS * HEAD_DIM).astype(BF16)
    g5 = gates_s.reshape(bd, sd, N_KV_HEADS, LANES)[..., :HEADS_PER_GROUP * 3]
    g5 = g5.reshape(bd, sd, N_KV_HEADS, HEADS_PER_GROUP, 3).transpose(0, 3, 2, 1, 4).reshape(bd, N_HEADS * sd, 3)
    gt = jnp.pad(g5, ((0, 0), (0, 0), (0, 5)))
    pad_new = lambda r: jnp.pad(r.reshape(bd, sd, KV_COLS), ((0, 0), (0, 16 - sd), (0, 0)))
    w_buf = cache_kv_win.shape[1]
    o_s = _nsa_sample(qbd, gt, kvc_s, cols_major(cache_kv_sel), page_table, pad_new(rs_s),
                      cols_major(cache_kv_win), pad_new(rw_s), dec=sd, past=PAST_LEN, ppc=16)
    o_s = o_s.reshape(bd, HEADS_PER_GROUP, N_KV_HEADS, sd, HEAD_DIM).transpose(0, 3, 2, 1, 4).reshape(n_s, d)
    hs, lgs = _out_proj(o_s, hs, nsa_wo[0], lng(1, 0), lnb(1, 0), moe_wr[1], br(1))
    routed_s = _moe_route(hs, lgs, tm_rows=64)

    tabs = _rope_tables(jnp.arange(sp, dtype=jnp.int32))
    rc, rs, rw, gates, qh, ks, vs, kw, vw = _nsa_proj(h, w_cat, tabs, tm=256, seq=sp, heads_out=True)
    kvc = _compress_t(rc, cw)
    kvc_h = kvc.reshape(bp, -1, 2, N_KV_HEADS, HEAD_DIM).transpose(2, 0, 3, 1, 4).astype(BF16)
    h, lg = _nsa_prompt(h, qh, kvc_h[0], kvc_h[1], ks, vs, kw, vw, gates, nsa_wo[0],
                        lng(1, 0), lnb(1, 0), moe_wr[1], br(1), tq=256, kc=512)
    routed_p = _moe_route(h, lg, tm_rows=256)
    y_sample = moe(1, hs, routed_s, tm_rows=64, tm_tok=n_s).reshape(bd, sd, d)
    y_prompt = moe(1, h, routed_p, tm_rows=256, tm_tok=256).reshape(bp, sp, d)
    rows_out = lambda r_t: r_t.reshape((bp,) + kv_shape + (r_t.shape[-1],)).transpose(0, 4, 1, 2, 3)
    kv_cmp_p = rows_out(rc)
    kv_sel_p = rows_out(rs)
    kv_win_p = rows_out(rw[:, :, sp - min(WINDOW, sp):])

    kv_cmp_s = rc_s.reshape((bd, sd) + kv_shape)
    kv_sel_s = rs_s.reshape((bd, sd) + kv_shape)
    kv_win_s = jnp.concatenate([cache_kv_win, rw_s.reshape((bd, sd) + kv_shape)], axis=1)[:, -w_buf:]

    return (y_prompt, y_sample, conv_p[None], kv_cmp_p, kv_sel_p, kv_win_p,
            conv_s[None], kv_cmp_s, kv_sel_s, kv_win_s)
```

```python
import functools

import numpy as np
import jax
import jax.numpy as jnp
from jax import lax
from jax.experimental import pallas as pl
from jax.experimental.pallas import tpu as pltpu

F32 = jnp.float32
BF16 = jnp.bfloat16
HIGHEST = lax.Precision.HIGHEST

D_MODEL = 1024
DEPTH = 2
PAST_LEN = 8192
PAGE_SIZE = 128
N_HEADS = 16
N_KV_HEADS = 4
HEADS_PER_GROUP = N_HEADS // N_KV_HEADS
HEAD_DIM = D_MODEL // N_HEADS
ROT_DIM = HEAD_DIM // 4
ROPE_THETA = 500000.0
CMP_BLK = 32
CMP_STRIDE = 16
CMP_HID = 2 * HEAD_DIM
SEL_BLK = 64
SEL_TOPN = 16
WINDOW = 512
N_EXPERTS = 32
TOP_K = 4
D_FF = D_MODEL
SWIGLU_LIMIT = 7.0
SWIGLU_ALPHA = 1.702
ALPHA = (2 * DEPTH) ** 0.25
LN_EPS = 1e-5
NEG = -1e30
FORCE = 1e4

KV_COLS = 2 * N_KV_HEADS * HEAD_DIM
LANES = 128
VMEM_LIMIT = 56 * 2 ** 20


def _params(sem, vmem=VMEM_LIMIT):
    return pltpu.CompilerParams(dimension_semantics=sem, vmem_limit_bytes=vmem)


def _ln(x, g, b):
    mu = jnp.mean(x, axis=-1, keepdims=True)
    xc = x - mu
    var = jnp.mean(xc * xc, axis=-1, keepdims=True)
    return xc * lax.rsqrt(var + LN_EPS) * g + b


def _nt(a, b):
    return lax.dot_general(a, b, (((1,), (1,)), ((), ())), preferred_element_type=F32)


def _router(h, wr_ref, br_ref):
    return jnp.dot(h, wr_ref[...], precision=HIGHEST, preferred_element_type=F32) + br_ref[...]


def _full(shape):
    return pl.BlockSpec(shape, lambda *_: (0,) * len(shape))


def _conv_body(x_ref, p_ref, win_ref, wc_ref, wout_ref, lng_ref, lnb_ref, wr_ref, br_ref,
               h_ref, st_ref, lg_ref, carry_ref, *, seq, tm, rows_prev):
    i = pl.program_id(0)

    @pl.when(i == 0)
    def _():
        carry_ref[...] = jnp.zeros_like(carry_ref)

    x = x_ref[...]
    d = x.shape[1]
    z = jnp.dot(x.astype(BF16), win_ref[...], preferred_element_type=F32)
    bg, c, xh = z[:, :d], z[:, d:2 * d], z[:, 2 * d:]
    u = c * xh
    row = lax.broadcasted_iota(jnp.int32, (tm, 1), 0)
    t = (i * tm + row) & (seq - 1)
    um1 = pltpu.roll(u, 1, 0)
    um2 = pltpu.roll(u, 2, 0)
    c0 = carry_ref[0:1, :]
    c1 = carry_ref[1:2, :]
    um1 = jnp.where(row == 0, c1, um1)
    um2 = jnp.where(row == 0, c0, jnp.where(row == 1, c1, um2))
    if rows_prev:
        p0, p1 = p_ref[0], p_ref[1]
    else:
        p0, p1 = p_ref[0, 0:1, :], p_ref[0, 1:2, :]
    um1 = jnp.where(t >= 1, um1, p1)
    um2 = jnp.where(t >= 2, um2, jnp.where(t == 1, p1, p0))
    conv = wc_ref[0:1, :] * um2 + wc_ref[1:2, :] * um1 + wc_ref[2:3, :] * u
    y = jnp.dot((bg * conv).astype(BF16), wout_ref[...], preferred_element_type=F32)
    h = _ln(ALPHA * x + y, lng_ref[...], lnb_ref[...])
    h_ref[...] = h
    lg_ref[...] = _router(h, wr_ref, br_ref)
    carry_ref[0:2, :] = u[tm - 2:tm, :]
    if rows_prev:
        st_ref[...] = u
    else:
        st_ref[0] = u[tm - 2:tm, :]


def _conv_layer(x, prev, w_in, w_conv, w_out, lng, lnb, wr, br, *, seq, tm, rows_prev):
    n, d = x.shape
    assert n % tm == 0 and seq >= 2 and seq & (seq - 1) == 0
    assert (seq % tm == 0) if not rows_prev else (tm % seq == 0 and n == tm)
    if rows_prev:
        p_spec = pl.BlockSpec((2, tm, d), lambda i: (0, i, 0))
        st_shape = jax.ShapeDtypeStruct((n, d), F32)
        st_spec = pl.BlockSpec((tm, d), lambda i: (i, 0))
    else:
        per = seq // tm
        p_spec = pl.BlockSpec((1, 2, d), lambda i: (i // per, 0, 0))
        st_shape = jax.ShapeDtypeStruct((n // seq, 2, d), F32)
        st_spec = pl.BlockSpec((1, 2, d), lambda i: (i // per, 0, 0))
    return pl.pallas_call(
        functools.partial(_conv_body, seq=seq, tm=tm, rows_prev=rows_prev),
        grid=(n // tm,),
        in_specs=[pl.BlockSpec((tm, d), lambda i: (i, 0)), p_spec,
                  _full((d, 3 * d)), _full((3, d)), _full((d, d)), _full((1, d)), _full((1, d)),
                  _full((d, N_EXPERTS)), _full((1, N_EXPERTS))],
        out_specs=[pl.BlockSpec((tm, d), lambda i: (i, 0)), st_spec,
                   pl.BlockSpec((tm, N_EXPERTS), lambda i: (i, 0))],
        out_shape=[jax.ShapeDtypeStruct((n, d), F32), st_shape,
                   jax.ShapeDtypeStruct((n, N_EXPERTS), F32)],
        scratch_shapes=[pltpu.VMEM((8, d), F32)],
        compiler_params=_params(("arbitrary",)),
        name="conv_mixer",
    )(x, prev, w_in.astype(BF16), w_conv, w_out.astype(BF16), lng, lnb, wr, br)


def _moe_body(te_ref, nv_ref, nx_ref, x_ref, w1_hbm, b1_ref, w2_hbm, b2_ref, o_ref, w1b, w2b, w1s, w2s, sem,
              *, layer):
    i = pl.program_id(0)
    e = te_ref[i]
    prev = te_ref[jnp.maximum(i - 1, 0)]
    valid = i < nv_ref[0]

    def weight_copies(expert):
        return (pltpu.make_async_copy(w1_hbm.at[layer, expert], w1s, sem.at[0]),
                pltpu.make_async_copy(w2_hbm.at[layer, expert], w2s, sem.at[1]))

    @pl.when(i == 0)
    def _():
        for cp in weight_copies(e):
            cp.start()

    @pl.when(valid & ((i == 0) | (e != prev)))
    def _():
        for cp in weight_copies(e):
            cp.wait()
        w1b[...] = w1s[...].astype(BF16)
        w2b[...] = w2s[...].astype(BF16)
        nxt = nx_ref[i]

        @pl.when(nxt >= 0)
        def _():
            for cp in weight_copies(nxt):
                cp.start()

    @pl.when(valid)
    def _():
        hgu = jnp.dot(x_ref[...].astype(BF16), w1b[...], preferred_element_type=F32) + b1_ref[0, 0]
        g = jnp.minimum(hgu[:, :D_FF], SWIGLU_LIMIT)
        u = jnp.clip(hgu[:, D_FF:], -SWIGLU_LIMIT, SWIGLU_LIMIT)
        a = (u + 1.0) * (g * jax.nn.sigmoid(SWIGLU_ALPHA * g))
        o_ref[...] = jnp.dot(a.astype(BF16), w2b[...], preferred_element_type=F32) + b2_ref[0, 0]

    @pl.when(jnp.logical_not(valid))
    def _():
        o_ref[...] = jnp.zeros_like(o_ref)


def _moe_experts(xb, tile_e, n_valid, next_e, w1, b1, w2, b2, *, layer, tm):
    n_rows, d = xb.shape
    n_tiles = n_rows // tm
    bias = lambda w: pl.BlockSpec((1, 1, 1, w), lambda i, te, nv, nx: (layer, te[i], 0, 0))
    grid_spec = pltpu.PrefetchScalarGridSpec(
        num_scalar_prefetch=3,
        grid=(n_tiles,),
        in_specs=[pl.BlockSpec((tm, d), lambda i, te, nv, nx: (i, 0)),
                  pl.BlockSpec(memory_space=pl.ANY), bias(2 * D_FF),
                  pl.BlockSpec(memory_space=pl.ANY), bias(d)],
        out_specs=pl.BlockSpec((tm, d), lambda i, te, nv, nx: (i, 0)),
        scratch_shapes=[pltpu.VMEM((d, 2 * D_FF), BF16), pltpu.VMEM((D_FF, d), BF16),
                        pltpu.VMEM((d, 2 * D_FF), F32), pltpu.VMEM((D_FF, d), F32),
                        pltpu.SemaphoreType.DMA((2,))],
    )
    return pl.pallas_call(
        functools.partial(_moe_body, layer=layer), grid_spec=grid_spec,
        out_shape=jax.ShapeDtypeStruct((n_rows, d), F32),
        compiler_params=_params(("arbitrary",)),
        name="moe_experts",
    )(tile_e, n_valid, next_e, xb, w1, b1[:, :, None, :], w2, b2[:, :, None, :])


def _combine_body(h_ref, yg_ref, gate_ref, lng_ref, lnb_ref, o_ref):
    gate = gate_ref[...]
    f = gate[:, 0:1] * yg_ref[0]
    for k in range(1, TOP_K):
        f = f + gate[:, k:k + 1] * yg_ref[k]
    o_ref[...] = _ln(ALPHA * h_ref[...] + f, lng_ref[...], lnb_ref[...])


def _moe_combine(h, yg, gate, lng, lnb, *, tm):
    n, d = h.shape
    return pl.pallas_call(
        _combine_body, grid=(n // tm,),
        in_specs=[pl.BlockSpec((tm, d), lambda i: (i, 0)),
                  pl.BlockSpec((TOP_K, tm, d), lambda i: (0, i, 0)),
                  pl.BlockSpec((tm, TOP_K), lambda i: (i, 0)),
                  _full((1, d)), _full((1, d))],
        out_specs=pl.BlockSpec((tm, d), lambda i: (i, 0)),
        out_shape=jax.ShapeDtypeStruct((n, d), F32),
        compiler_params=_params(("arbitrary",)),
        name="moe_combine",
    )(h, yg, gate, lng, lnb)


def _route_body(lg_ref, e_ref, g_ref, r_ref, cnt_ref, carry, *, tm):
    i = pl.program_id(0)

    @pl.when(i == 0)
    def _():
        carry[...] = jnp.zeros_like(carry)

    lg = lg_ref[...]
    lane = lax.broadcasted_iota(jnp.int32, lg.shape, 1).astype(F32)
    picks, vals = [], []
    for _ in range(TOP_K):
        m = jnp.max(lg, axis=-1, keepdims=True)
        idx = jnp.min(jnp.where(lg == m, lane, float(N_EXPERTS)), axis=-1, keepdims=True)
        hit = lane == idx
        picks.append((idx.astype(jnp.int32), hit))
        vals.append(m)
        lg = jnp.where(hit, -jnp.inf, lg)
    ex = [jnp.exp(v - vals[0]) for v in vals]
    den = ex[0] + ex[1] + ex[2] + ex[3]
    chosen = picks[0][1] | picks[1][1] | picks[2][1] | picks[3][1]
    onehot = chosen.astype(BF16)
    before = (lax.broadcasted_iota(jnp.int32, (tm, tm), 1)
              < lax.broadcasted_iota(jnp.int32, (tm, tm), 0)).astype(BF16)
    prior = jnp.dot(before, onehot, preferred_element_type=F32) + carry[...]
    for k, (idx, hit) in enumerate(picks):
        e_ref[:, k:k + 1] = idx
        g_ref[:, k:k + 1] = ex[k] / den
        r_ref[:, k:k + 1] = jnp.sum(jnp.where(hit, prior, 0.0), axis=-1, keepdims=True).astype(jnp.int32)
    carry[...] = carry[...] + jnp.sum(chosen.astype(F32), axis=0, keepdims=True)
    cnt_ref[...] = carry[...]


def _route_topk(logits, *, tm):
    n, ne = logits.shape
    col = lambda dt: (pl.BlockSpec((tm, TOP_K), lambda i: (i, 0)), jax.ShapeDtypeStruct((n, TOP_K), dt))
    specs, shapes = zip(col(jnp.int32), col(F32), col(jnp.int32),
                        (_full((1, ne)), jax.ShapeDtypeStruct((1, ne), F32)))
    return pl.pallas_call(
        functools.partial(_route_body, tm=tm), grid=(n // tm,),
        in_specs=[pl.BlockSpec((tm, ne), lambda i: (i, 0))],
        out_specs=list(specs), out_shape=list(shapes),
        scratch_shapes=[pltpu.VMEM((1, ne), F32)],
        compiler_params=_params(("arbitrary",)),
        name="moe_route",
    )(logits)


def _moe_route(h, logits, *, tm_rows):
    n, d = h.shape
    top_e, gate, rank, counts = _route_topk(logits, tm=min(n, 512))
    flat_e = top_e.reshape(-1)
    rank = rank.reshape(-1)
    counts = counts.reshape(-1).astype(jnp.int32)
    nk = n * TOP_K
    padded = (counts + tm_rows - 1) // tm_rows * tm_rows
    pad_end = jnp.cumsum(padded)
    pad_start = pad_end - padded
    dest = (pad_start[flat_e] + rank).astype(jnp.int32)
    n_tiles = (nk + N_EXPERTS * (tm_rows - 1) + tm_rows - 1) // tm_rows
    n_rows = n_tiles * tm_rows
    tile_start = jnp.arange(n_tiles, dtype=jnp.int32) * tm_rows
    tile_e = jnp.minimum(jnp.sum((pad_end[None, :] <= tile_start[:, None]).astype(jnp.int32), axis=1), N_EXPERTS - 1)
    n_valid = (pad_end[-1] // tm_rows).astype(jnp.int32).reshape(1)
    _, tok_sorted = lax.sort_key_val(dest, jnp.arange(nk, dtype=jnp.int32) // TOP_K)
    seg_end = jnp.cumsum(counts)
    shift = pad_start - (seg_end - counts)
    tile_ok = tile_start < pad_end[-1]
    last = jnp.where(tile_ok, seg_end[tile_e] - 1, nk - 1)
    src = jnp.minimum(jnp.arange(n_rows, dtype=jnp.int32) - jnp.repeat(shift[tile_e], tm_rows),
                      jnp.repeat(last, tm_rows))
    row_tok = tok_sorted.at[src].get(indices_are_sorted=True, mode='promise_in_bounds')
    xb = h[row_tok]
    eidx = jnp.arange(N_EXPERTS, dtype=jnp.int32)
    later = jnp.where((eidx[None, :] > eidx[:, None]) & (counts[None, :] > 0), eidx[None, :], N_EXPERTS)
    nxt_tab = jnp.min(later, axis=1)
    next_e = jnp.where(nxt_tab == N_EXPERTS, -1, nxt_tab).astype(jnp.int32)[tile_e]
    return xb, (tile_e, n_valid, next_e), dest.reshape(n, TOP_K).T, gate


def _moe_ffn(routed, w1, b1, w2, b2, *, layer, tm_rows):
    xb, tiles, dest_t, gate = routed
    yb = _moe_experts(xb, *tiles, w1, b1, w2, b2, layer=layer, tm=tm_rows)
    return yb[dest_t], gate


def _proj_body(h_ref, w_ref, rc_ref, rs1_ref, rs2_ref, *out_refs, heads_out, seq, q_scale):
    z = jnp.dot(h_ref[...].astype(BF16), w_ref[...], preferred_element_type=F32)
    cc, s1, s2 = rc_ref[...], rs1_ref[...], rs2_ref[...]

    def rope(x):
        return x * cc + pltpu.roll(x, LANES - ROT_DIM // 2, 1) * s1 + pltpu.roll(x, ROT_DIM // 2, 1) * s2

    def rope_cols(lo, hi):
        return [rope(z[:, c:c + LANES]) for c in range(lo, hi, LANES)]

    if heads_out:
        rows_c, rows_s, rows_w, g_ref, q_ref, ks_ref, vs_ref, kw_ref, vw_ref = out_refs
    else:
        rows_c, rows_s, rows_w, g_ref, q_ref = out_refs
    half = KV_COLS // 2
    for br, rows in enumerate((rows_c, rows_s, rows_w)):
        base = br * KV_COLS
        k = jnp.concatenate(rope_cols(base, base + half), axis=1)
        v = z[:, base + half:base + KV_COLS]
        kv = jnp.concatenate([k, v], axis=1)
        if heads_out:
            rows[0] = kv.T
        else:
            rows[...] = kv
        if heads_out and br >= 1:
            tm = k.shape[0]
            t = (pl.program_id(0) % (seq // tm)) * tm + lax.broadcasted_iota(jnp.int32, (tm, HEAD_DIM), 0)
            lane = lax.broadcasted_iota(jnp.int32, (tm, HEAD_DIM), 1)
            blk_onehot = (lane == lax.shift_right_logical(t, 6)).astype(BF16)
            one_lane = (lane == 0).astype(BF16)
            k_ref, v_ref = (ks_ref, vs_ref) if br == 1 else (kw_ref, vw_ref)
            for g in range(N_KV_HEADS):
                kg = k[:, g * HEAD_DIM:(g + 1) * HEAD_DIM].astype(BF16)
                vg = v[:, g * HEAD_DIM:(g + 1) * HEAD_DIM].astype(BF16)
                k_ref[0, g] = jnp.concatenate([kg, blk_onehot], axis=1) if br == 1 else kg
                v_ref[0, g] = jnp.concatenate([vg, one_lane], axis=1)
    qb = 3 * KV_COLS
    q = jnp.concatenate(rope_cols(qb, qb + D_MODEL), axis=1) * q_scale
    if heads_out:
        for hd in range(N_HEADS):
            q_ref[0, hd] = q[:, hd * HEAD_DIM:(hd + 1) * HEAD_DIM].astype(BF16)
    else:
        q_ref[...] = q
    g_ref[...] = jax.nn.sigmoid(z[:, qb + D_MODEL:])


def _rope_tables(pos):
    half = ROT_DIM // 2
    inv = ROPE_THETA ** (-jnp.arange(half, dtype=F32) * 2.0 / ROT_DIM)
    ang = pos.astype(F32)[:, None] * inv[None, :]
    cos, sin = jnp.cos(ang), jnp.sin(ang)
    zeros = jnp.zeros((pos.shape[0], HEAD_DIM - ROT_DIM), F32)
    ones = jnp.ones_like(zeros)
    z8 = jnp.zeros_like(sin)
    c = jnp.concatenate([cos, cos, ones], axis=1)
    s1 = jnp.concatenate([-sin, z8, zeros], axis=1)
    s2 = jnp.concatenate([z8, sin, zeros], axis=1)
    rep = LANES // HEAD_DIM
    return tuple(jnp.tile(a, (1, rep)) for a in (c, s1, s2))


def _proj_weights(kv_w, wq, wg):
    wg4 = wg.reshape(D_MODEL, N_KV_HEADS, HEADS_PER_GROUP * 3)
    wg4 = jnp.pad(wg4, ((0, 0), (0, 0), (0, LANES - HEADS_PER_GROUP * 3))).reshape(D_MODEL, N_KV_HEADS * LANES)
    return jnp.concatenate([kv_w, wq, wg4], axis=1).astype(BF16)


def _nsa_proj(h, w_cat, tables, *, tm, seq, heads_out):
    n, d = h.shape
    ncol = w_cat.shape[1]
    per = tables[0].shape[0] // tm
    tab_spec = pl.BlockSpec((tm, LANES), lambda i: (i % per, 0))
    row_spec = lambda w: pl.BlockSpec((tm, w), lambda i: (i, 0))
    g_spec, g_shape = row_spec(N_KV_HEADS * LANES), jax.ShapeDtypeStruct((n, N_KV_HEADS * LANES), F32)
    if heads_out:
        bsz = n // seq
        tps = seq // tm
        out_specs = [pl.BlockSpec((1, KV_COLS, tm), lambda i: (i // tps, 0, i % tps))] * 3 + [g_spec]
        out_shape = [jax.ShapeDtypeStruct((bsz, KV_COLS, seq), F32)] * 3 + [g_shape]
        assert seq // SEL_BLK <= HEAD_DIM
        hspec = lambda nh, w: pl.BlockSpec((1, nh, tm, w), lambda i: (i // tps, 0, i % tps, 0))
        hshape = lambda nh, w: jax.ShapeDtypeStruct((bsz, nh, seq, w), BF16)
        widths = (2 * HEAD_DIM, 2 * HEAD_DIM, HEAD_DIM, 2 * HEAD_DIM)
        out_specs += [hspec(N_HEADS, HEAD_DIM)] + [hspec(N_KV_HEADS, w) for w in widths]
        out_shape += [hshape(N_HEADS, HEAD_DIM)] + [hshape(N_KV_HEADS, w) for w in widths]
    else:
        out_specs = [row_spec(KV_COLS)] * 3 + [g_spec, row_spec(d)]
        out_shape = [jax.ShapeDtypeStruct((n, KV_COLS), F32)] * 3 + [g_shape, jax.ShapeDtypeStruct((n, d), F32)]
    q_scale = HEAD_DIM ** -0.5 * (float(np.log2(np.e)) if heads_out else 1.0)
    return pl.pallas_call(
        functools.partial(_proj_body, heads_out=heads_out, seq=seq, q_scale=q_scale), grid=(n // tm,),
        in_specs=[row_spec(d), _full((d, ncol)), tab_spec, tab_spec, tab_spec],
        out_specs=out_specs, out_shape=out_shape,
        compiler_params=_params(("arbitrary",)),
        name="nsa_proj",
    )(h, w_cat, *tables)


def _cmp_math(xt, w1_ref, pe_ref, w2_ref, o_ref, nch):
    k = pl.program_id(1) // 2
    x = jnp.concatenate(
        [xt[pl.ds(c, nch, stride=CMP_STRIDE), :].astype(BF16) for c in range(CMP_STRIDE)],
        axis=1)
    acc = jnp.dot(x, w1_ref[0], preferred_element_type=F32)
    pe = pe_ref[pl.ds(k, 1), :]
    hid = []
    for gs in range(2):
        pa = acc[:, gs * 2 * CMP_HID:gs * 2 * CMP_HID + CMP_HID]
        pb = acc[:, gs * 2 * CMP_HID + CMP_HID:(gs + 1) * 2 * CMP_HID]
        pb_next = pltpu.roll(pb, nch - 1, 0)
        hid.append(jax.nn.gelu(pa + pb_next + pe))
    hid = jnp.concatenate(hid, axis=1).astype(BF16)
    o_ref[0] = jnp.dot(hid, w2_ref[0], preferred_element_type=F32)


def _cmp_t_body(r_ref, w1_ref, pe_ref, w2_ref, o_ref, xt, *, nch):
    for p in range(r_ref.shape[2] // LANES):
        xt[p * LANES:(p + 1) * LANES, :] = r_ref[0, :, p * LANES:(p + 1) * LANES].T
    _cmp_math(xt, w1_ref, pe_ref, w2_ref, o_ref, nch)


def _cmp_paged_body(pt_ref, cache_ref, w1_ref, pe_ref, w2_ref, o_ref, pbuf, xt, sem, *, nch, n_pages):
    ncol = pl.num_programs(1)
    step = pl.program_id(0) * ncol + pl.program_id(1)
    nsteps = pl.num_programs(0) * ncol
    slot = step & 1

    def page_copy(st, p, sl):
        bb = st // ncol
        col = pl.multiple_of((st - bb * ncol) * LANES, LANES)
        return pltpu.make_async_copy(cache_ref.at[pt_ref[bb * n_pages + p], pl.ds(col, LANES), :],
                                     pbuf.at[sl, p], sem.at[sl])

    @pl.when(step == 0)
    def _():
        for p in range(n_pages):
            page_copy(step, p, slot).start()

    @pl.when(step + 1 < nsteps)
    def _():
        for p in range(n_pages):
            page_copy(step + 1, p, 1 - slot).start()

    for p in range(n_pages):
        page_copy(step, p, slot).wait()
    for p in range(n_pages):
        xt[p * PAGE_SIZE:(p + 1) * PAGE_SIZE, :] = pbuf[slot, p].T
    _cmp_math(xt, w1_ref, pe_ref, w2_ref, o_ref, nch)


def _cmp_weights(cmp_pe, cmp_w1, cmp_b1, cmp_w2):
    eye = jnp.eye(2, dtype=F32)
    w1r = cmp_w1.reshape(2, 2, CMP_STRIDE, HEAD_DIM, CMP_HID)
    w1bd = jnp.einsum('ab,kncdh->kcadbnh', eye, w1r).reshape(2, CMP_STRIDE * 2 * HEAD_DIM, 4 * CMP_HID)
    w2bd = jnp.einsum('ab,khd->kahbd', eye, cmp_w2).reshape(2, 2 * CMP_HID, 2 * HEAD_DIM)
    pe_term = jnp.einsum('kcd,kcdh->kh', cmp_pe, cmp_w1, precision=HIGHEST) + cmp_b1
    return w1bd.astype(BF16), pe_term, w2bd.astype(BF16)


def _cmp_specs(cw, nch, imap):
    w1bd, pe_term, w2bd = cw
    return ([pl.BlockSpec((1,) + w1bd.shape[1:], imap(lambda b, j: (j // 2, 0, 0))),
             pl.BlockSpec(pe_term.shape, imap(lambda b, j: (0, 0))),
             pl.BlockSpec((1,) + w2bd.shape[1:], imap(lambda b, j: (j // 2, 0, 0)))],
            pl.BlockSpec((1, nch, LANES), imap(lambda b, j: (b, 0, j))))


def _compress_t(rows_t, cw):
    bsz, _, t = rows_t.shape
    nch = t // CMP_STRIDE
    w_specs, o_spec = _cmp_specs(cw, nch, lambda f: f)
    return pl.pallas_call(
        functools.partial(_cmp_t_body, nch=nch), grid=(bsz, KV_COLS // LANES),
        in_specs=[pl.BlockSpec((1, LANES, t), lambda b, j: (b, j, 0))] + w_specs,
        out_specs=o_spec,
        out_shape=jax.ShapeDtypeStruct((bsz, nch, KV_COLS), F32),
        scratch_shapes=[pltpu.VMEM((t, LANES), F32)],
        compiler_params=_params(("arbitrary", "arbitrary")),
        name="compress_kv",
    )(rows_t, *cw)


def _compress_paged(cache_t, page_table, cw):
    bsz, n_pages = page_table.shape
    nch = n_pages * PAGE_SIZE // CMP_STRIDE
    w_specs, o_spec = _cmp_specs(cw, nch, lambda f: (lambda b, j, pt: f(b, j)))
    grid_spec = pltpu.PrefetchScalarGridSpec(
        num_scalar_prefetch=1, grid=(bsz, KV_COLS // LANES),
        in_specs=[pl.BlockSpec(memory_space=pl.ANY)] + w_specs,
        out_specs=o_spec,
        scratch_shapes=[pltpu.VMEM((2, n_pages, LANES, PAGE_SIZE), F32),
                        pltpu.VMEM((n_pages * PAGE_SIZE, LANES), F32),
                        pltpu.SemaphoreType.DMA((2,))])
    return pl.pallas_call(
        functools.partial(_cmp_paged_body, nch=nch, n_pages=n_pages), grid_spec=grid_spec,
        out_shape=jax.ShapeDtypeStruct((bsz, nch, KV_COLS), F32),
        compiler_params=_params(("arbitrary", "arbitrary")),
        name="compress_kv_paged",
    )(page_table.reshape(-1), cache_t, *cw)


def _softmax_probs(s, m, exp_fn=jnp.exp):
    s = jnp.where(m, s, NEG)
    mx = jnp.max(s, axis=-1, keepdims=True)
    e = jnp.where(m, exp_fn(s - mx), 0.0)
    den = jnp.sum(e, axis=-1, keepdims=True)
    return e / jnp.maximum(den, 1e-30)


def _select_blocks(imp_t, qpos_row, n_rows):
    lanes = imp_t.shape[1]
    j = lax.broadcasted_iota(jnp.int32, (n_rows, lanes), 0)
    cur = lax.shift_right_logical(qpos_row, 6)
    valid = j <= cur
    forced = (j == 0) | (j == cur) | (j == cur - 1)
    sc = jnp.where(valid, jnp.where(forced, FORCE, imp_t[:n_rows]), NEG)
    return sc, j, valid


def _overlap_t(n_sel_pad, n_cmp_pad):
    c0 = np.arange(n_cmp_pad)[None, :] * CMP_STRIDE
    s0 = np.arange(n_sel_pad)[:, None] * SEL_BLK
    ov = np.clip(np.minimum(c0 + CMP_BLK, s0 + SEL_BLK) - np.maximum(c0, s0), 0, None).astype(np.float32) / CMP_BLK
    return ov


def _attn_p_body(q_ref, kck_ref, kcv_ref, ks_ref, vs_ref, kw_ref, vw_ref, g_ref, h_ref,
                 ovt_ref, wo_ref, lng_ref, lnb_ref, wr_ref, br_ref,
                 o_ref, lg_ref, o_scr, *, tq, seq, kc):
    i = pl.program_id(1)
    g = pl.program_id(2)
    t0 = i * tq
    rows = HEADS_PER_GROUP * tq
    n_sel = seq // SEL_BLK
    q = q_ref[0].reshape(rows, HEAD_DIM)
    tpos = t0 + (lax.broadcasted_iota(jnp.int32, (rows, 1), 0) & (tq - 1))

    n_cmp = kck_ref.shape[2]
    s = _nt(q, kck_ref[0, 0])
    cend = lax.broadcasted_iota(jnp.int32, (1, n_cmp), 1) * CMP_STRIDE + (CMP_BLK - 1)
    p_c = _softmax_probs(s, cend <= tpos, jnp.exp2)
    o_c = jnp.dot(p_c.astype(BF16), kcv_ref[0, 0], preferred_element_type=F32)
    psum = p_c[0:tq]
    for qh in range(1, HEADS_PER_GROUP):
        psum = psum + p_c[qh * tq:(qh + 1) * tq]
    imp_t = lax.dot_general(ovt_ref[...], psum, (((1,), (1,)), ((), ())),
                            precision=HIGHEST, preferred_element_type=F32)

    qrow = t0 + lax.broadcasted_iota(jnp.int32, (1, tq), 1)
    sc, j, valid = _select_blocks(imp_t, qrow, n_sel)
    rank = jnp.zeros((n_sel, tq), jnp.int32)
    for jp in range(n_sel):
        r = sc[jp:jp + 1, :]
        beats = (r > sc) | ((r == sc) & (j > jp))
        rank = rank + beats.astype(jnp.int32)
    sel_t = (valid & (rank < SEL_TOPN)).astype(F32)
    sel_t = jnp.concatenate([sel_t, jnp.zeros((LANES - n_sel, tq), F32)], axis=0)
    sel = sel_t.T[:, 0:HEAD_DIM]
    blk_lane = lax.broadcasted_iota(jnp.int32, (1, HEAD_DIM), 1)
    r_i = lax.broadcasted_iota(jnp.int32, (tq, 1), 0)
    tq_pos = t0 + r_i
    t0a = pl.multiple_of(t0, tq)
    causal = jnp.where(lax.broadcasted_iota(jnp.int32, (1, tq), 1) <= r_i, 0.0, NEG)
    causal = jnp.concatenate([causal] * HEADS_PER_GROUP, axis=0)

    s = _nt(q, ks_ref[0, 0, pl.ds(t0a, tq), 0:HEAD_DIM]) + causal
    m0 = jnp.max(s, axis=-1, keepdims=True)
    acc0 = jnp.dot(jnp.exp2(s - m0).astype(BF16), vs_ref[0, 0, pl.ds(t0a, tq), :], preferred_element_type=F32)
    sweep = jnp.where((sel > 0.5) & (blk_lane < lax.shift_right_logical(t0, 6)), 0.0, NEG).astype(BF16)
    q_aug = jnp.concatenate([q, jnp.concatenate([sweep] * HEADS_PER_GROUP, axis=0)], axis=1)

    def sweep_chunk(c, carry):
        m_run, acc = carry
        k0 = pl.multiple_of(c * kc, kc)
        s = _nt(q_aug, ks_ref[0, 0, pl.ds(k0, kc), :])
        m_new = jnp.maximum(m_run, jnp.max(s, axis=-1, keepdims=True))
        pv = jnp.dot(jnp.exp2(s - m_new).astype(BF16), vs_ref[0, 0, pl.ds(k0, kc), :],
                     preferred_element_type=F32)
        return m_new, jnp.exp2(m_run - m_new) * acc + pv

    _, acc = lax.fori_loop(0, (t0 + kc - 1) // kc, sweep_chunk, (m0, acc0))
    o_s = acc[:, 0:HEAD_DIM] / acc[:, HEAD_DIM:HEAD_DIM + 1]

    ws = pl.multiple_of(jnp.maximum(t0 - WINDOW, 0), tq)
    wi = ws + lax.broadcasted_iota(jnp.int32, (1, WINDOW), 1)
    wbias = jnp.where((wi > tq_pos - WINDOW) & (wi < t0), 0.0, NEG)
    s_d = _nt(q, kw_ref[0, 0, pl.ds(t0a, tq), :]) + causal
    s_w = _nt(q, kw_ref[0, 0, pl.ds(ws, WINDOW), :]) + jnp.concatenate([wbias] * HEADS_PER_GROUP, axis=0)
    m = jnp.maximum(jnp.max(s_d, axis=-1, keepdims=True), jnp.max(s_w, axis=-1, keepdims=True))
    acc = (jnp.dot(jnp.exp2(s_d - m).astype(BF16), vw_ref[0, 0, pl.ds(t0a, tq), :], preferred_element_type=F32)
           + jnp.dot(jnp.exp2(s_w - m).astype(BF16), vw_ref[0, 0, pl.ds(ws, WINDOW), :],
                     preferred_element_type=F32))
    o_w = acc[:, 0:HEAD_DIM] / acc[:, HEAD_DIM:HEAD_DIM + 1]

    gates = g_ref[...]
    for qh in range(HEADS_PER_GROUP):
        sl = slice(qh * tq, (qh + 1) * tq)
        o_h = (gates[:, 3 * qh:3 * qh + 1] * o_c[sl] + gates[:, 3 * qh + 1:3 * qh + 2] * o_s[sl]
               + gates[:, 3 * qh + 2:3 * qh + 3] * o_w[sl])
        o_scr[g, :, qh * HEAD_DIM:(qh + 1) * HEAD_DIM] = o_h

    @pl.when(g == N_KV_HEADS - 1)
    def _():
        o = jnp.concatenate([o_scr[gg] for gg in range(N_KV_HEADS)], axis=1).astype(BF16)
        y = jnp.dot(o, wo_ref[...], preferred_element_type=F32)
        h = _ln(ALPHA * h_ref[...] + y, lng_ref[...], lnb_ref[...])
        o_ref[...] = h
        lg_ref[...] = _router(h, wr_ref, br_ref)


def _nsa_prompt(h, qh, kck, kcv, ks, vs, kw, vw, gates, wo, lng, lnb, wr, br, *, tq, kc):
    n, d = h.shape
    bsz, _, seq, _ = ks.shape
    nt = seq // tq
    n_cmp = kck.shape[2]
    ovt = jnp.asarray(_overlap_t(LANES, n_cmp))
    assert kc % tq == 0 and seq % kc == 0 and SEL_BLK % 64 == 0 and tq % SEL_BLK == 0
    grp = lambda a: pl.BlockSpec((1, 1) + a.shape[2:], lambda b, i, g: (b, g, 0, 0))
    row = lambda w: pl.BlockSpec((tq, w), lambda b, i, g: (b * nt + i, 0))
    return pl.pallas_call(
        functools.partial(_attn_p_body, tq=tq, seq=seq, kc=kc),
        grid=(bsz, nt, N_KV_HEADS),
        in_specs=[pl.BlockSpec((1, HEADS_PER_GROUP, tq, HEAD_DIM), lambda b, i, g: (b, g, i, 0)),
                  grp(kck), grp(kcv), grp(ks), grp(vs), grp(kw), grp(vw),
                  pl.BlockSpec((tq, LANES), lambda b, i, g: (b * nt + i, g)), row(d),
                  _full(ovt.shape), _full((d, d)), _full((1, d)), _full((1, d)),
                  _full((d, N_EXPERTS)), _full((1, N_EXPERTS))],
        out_specs=[row(d), row(N_EXPERTS)],
        out_shape=[jax.ShapeDtypeStruct((n, d), F32), jax.ShapeDtypeStruct((n, N_EXPERTS), F32)],
        scratch_shapes=[pltpu.VMEM((N_KV_HEADS, tq, HEADS_PER_GROUP * HEAD_DIM), F32)],
        compiler_params=_params(("arbitrary", "arbitrary", "arbitrary")),
        name="nsa_prompt",
    )(qh, kck, kcv, ks, vs, kw, vw, gates, h, ovt, wo.astype(BF16), lng, lnb, wr, br)


def _attn_s_body(pt_ref, q_ref, gt_ref, kvc_ref, cache_ref, ns_ref, cw_ref, nw_ref, ov_ref, exp_ref,
                 o_ref, kvbuf, sem, msk_scr, m_scr, l_scr, acc_scr, ocw_scr, *, dec, past, ppc):
    nrow = q_ref.shape[1]
    half = KV_COLS // 2
    c = pl.program_id(1)
    nc = pl.num_programs(1)
    step = pl.program_id(0) * nc + c
    nsteps = pl.num_programs(0) * nc
    slot = step & 1

    def chunk_copies(st, sl):
        return [pltpu.make_async_copy(cache_ref.at[pt_ref[st * ppc + p]],
                                      kvbuf.at[sl, :, pl.ds(p * PAGE_SIZE, PAGE_SIZE)], sem.at[sl])
                for p in range(ppc)]

    @pl.when(step == 0)
    def _():
        for cp in chunk_copies(step, slot):
            cp.start()

    @pl.when(step + 1 < nsteps)
    def _():
        for cp in chunk_copies(step + 1, 1 - slot):
            cp.start()

    q = q_ref[0]
    row = lax.broadcasted_iota(jnp.int32, (nrow, 1), 0)
    tok = row & (dec - 1)
    gsel = lax.shift_right_logical(row, 2) & (N_KV_HEADS - 1)

    def pick(o):
        out = jnp.zeros((nrow, HEAD_DIM), F32)
        for gg in range(N_KV_HEADS):
            out = out + jnp.where(gsel == gg, o[:, gg * HEAD_DIM:(gg + 1) * HEAD_DIM], 0.0)
        return out

    def new_rows(new_ref):
        k_new = new_ref[0, :, 0:half].astype(BF16)
        m_new = lax.broadcasted_iota(jnp.int32, (1, new_ref.shape[1]), 1) <= tok
        return jnp.where(m_new, _nt(q, k_new), NEG), m_new, new_ref[0, :, half:KV_COLS].astype(BF16)

    @pl.when(c == 0)
    def _():
        qpos = past + tok
        n_cmp = kvc_ref.shape[1]
        kc = kvc_ref[0, :, 0:half].astype(BF16)
        vc = kvc_ref[0, :, half:KV_COLS].astype(BF16)
        cend = lax.broadcasted_iota(jnp.int32, (1, n_cmp), 1) * CMP_STRIDE + (CMP_BLK - 1)
        p_c = _softmax_probs(_nt(q, kc), cend <= qpos)
        ocw_scr[0] = pick(jnp.dot(p_c.astype(BF16), vc, preferred_element_type=F32))
        ngt = N_KV_HEADS * dec
        psum = p_c[0:ngt]
        for qh in range(1, HEADS_PER_GROUP):
            psum = psum + p_c[qh * ngt:(qh + 1) * ngt]
        imp = jnp.dot(psum, ov_ref[...], precision=HIGHEST, preferred_element_type=F32)

        n_sel = past // SEL_BLK + 1
        j = lax.broadcasted_iota(jnp.int32, imp.shape, 1)
        cur = lax.shift_right_logical(past + (lax.broadcasted_iota(jnp.int32, (ngt, 1), 0) & (dec - 1)), 6)
        valid = j <= cur
        forced = (j == 0) | (j == cur) | (j == cur - 1)
        sc = jnp.where(valid, jnp.where(forced, FORCE, imp), NEG)
        rank = jnp.zeros(imp.shape, jnp.int32)
        for jp in range(n_sel):
            r = sc[:, jp:jp + 1]
            rank = rank + ((r > sc) | ((r == sc) & (j > jp))).astype(jnp.int32)
        mrow = jnp.where(valid & (rank < SEL_TOPN), 0.0, NEG)[:, 0:LANES]
        msk_scr[...] = jnp.concatenate([mrow] * HEADS_PER_GROUP, axis=0).astype(BF16)

        s_n, m_n, v_n = new_rows(ns_ref)
        m0 = jnp.max(s_n, axis=-1, keepdims=True)
        e_n = jnp.where(m_n, jnp.exp(s_n - m0), 0.0)
        m_scr[...] = m0
        l_scr[...] = jnp.sum(e_n, axis=-1, keepdims=True)
        acc_scr[...] = jnp.dot(e_n.astype(BF16), v_n, preferred_element_type=F32)

        n_win = cw_ref.shape[2]
        m_w = lax.broadcasted_iota(jnp.int32, (1, n_win), 1) > tok + (n_win - WINDOW)
        s_o = jnp.where(m_w, jnp.dot(q, cw_ref[0, 0:half, :].astype(BF16), preferred_element_type=F32), NEG)
        s_n, m_n, v_n = new_rows(nw_ref)
        mx = jnp.maximum(jnp.max(s_o, axis=-1, keepdims=True), jnp.max(s_n, axis=-1, keepdims=True))
        e_o = jnp.where(m_w, jnp.exp(s_o - mx), 0.0)
        e_n = jnp.where(m_n, jnp.exp(s_n - mx), 0.0)
        den = jnp.sum(e_o, axis=-1, keepdims=True) + jnp.sum(e_n, axis=-1, keepdims=True)
        o_w = (_nt(e_o.astype(BF16), cw_ref[0, half:KV_COLS, :].astype(BF16))
               + jnp.dot(e_n.astype(BF16), v_n, preferred_element_type=F32))
        ocw_scr[1] = pick(o_w / den)

    for cp in chunk_copies(step, slot):
        cp.wait()
    kt = kvbuf[slot, 0:half, :].astype(BF16)
    vt = kvbuf[slot, half:KV_COLS, :].astype(BF16)
    s = (jnp.dot(q, kt, preferred_element_type=F32)
         + jnp.dot(msk_scr[...], exp_ref[c], preferred_element_type=F32))
    m_old = m_scr[...]
    m_new = jnp.maximum(m_old, jnp.max(s, axis=-1, keepdims=True))
    alpha = jnp.exp(m_old - m_new)
    e = jnp.exp(s - m_new)
    m_scr[...] = m_new
    l_scr[...] = alpha * l_scr[...] + jnp.sum(e, axis=-1, keepdims=True)
    acc_scr[...] = alpha * acc_scr[...] + _nt(e.astype(BF16), vt)

    @pl.when(c == nc - 1)
    def _():
        gt = gt_ref[0]
        o_s = pick(acc_scr[...] / l_scr[...])
        o_ref[0] = gt[:, 0:1] * ocw_scr[0] + gt[:, 1:2] * o_s + gt[:, 2:3] * ocw_scr[1]


def _nsa_sample(qbd, gt, kvc, cache_t, page_table, new_s, cache_w_t, new_w, *, dec, past, ppc):
    bsz, nrow, _ = qbd.shape
    n_pages = past // PAGE_SIZE
    assert n_pages % ppc == 0 and past // SEL_BLK <= LANES
    nc = n_pages // ppc
    ck = ppc * PAGE_SIZE
    ovt = jnp.asarray(_overlap_t(2 * LANES, kvc.shape[1]).T)
    kblk = (np.arange(past) // SEL_BLK).reshape(nc, 1, ck)
    expand = jnp.asarray(np.arange(LANES)[None, :, None] == kblk, BF16)
    assert past // SEL_BLK + 1 <= 2 * LANES
    b3 = lambda a: pl.BlockSpec((1,) + a.shape[1:], lambda b, c, pt: (b, 0, 0))
    full = lambda a: pl.BlockSpec(a.shape, lambda b, c, pt: (0,) * a.ndim)
    grid_spec = pltpu.PrefetchScalarGridSpec(
        num_scalar_prefetch=1, grid=(bsz, nc),
        in_specs=[b3(qbd), b3(gt), b3(kvc), pl.BlockSpec(memory_space=pl.ANY), b3(new_s), b3(cache_w_t),
                  b3(new_w), full(ovt), full(expand)],
        out_specs=pl.BlockSpec((1, nrow, HEAD_DIM), lambda b, c, pt: (b, 0, 0)),
        scratch_shapes=[pltpu.VMEM((2, KV_COLS, ck), F32),
                        pltpu.SemaphoreType.DMA((2,)),
                        pltpu.VMEM((nrow, LANES), BF16),
                        pltpu.VMEM((nrow, 1), F32), pltpu.VMEM((nrow, 1), F32),
                        pltpu.VMEM((nrow, KV_COLS // 2), F32),
                        pltpu.VMEM((2, nrow, HEAD_DIM), F32)])
    return pl.pallas_call(
        functools.partial(_attn_s_body, dec=dec, past=past, ppc=ppc), grid_spec=grid_spec,
        out_shape=jax.ShapeDtypeStruct((bsz, nrow, HEAD_DIM), F32),
        compiler_params=_params(("arbitrary", "arbitrary")),
        name="nsa_sample",
    )(page_table.reshape(-1), qbd, gt, kvc, cache_t, new_s, cache_w_t, new_w, ovt, expand)


def _out_body(o_ref, h_ref, wo_ref, lng_ref, lnb_ref, wr_ref, br_ref, y_ref, lg_ref):
    y = jnp.dot(o_ref[...].astype(BF16), wo_ref[...], preferred_element_type=F32)
    h = _ln(ALPHA * h_ref[...] + y, lng_ref[...], lnb_ref[...])
    y_ref[...] = h
    lg_ref[...] = _router(h, wr_ref, br_ref)


def _out_proj(o, h, wo, lng, lnb, wr, br):
    n, d = h.shape
    return pl.pallas_call(
        _out_body, grid=(1,),
        in_specs=[_full((n, d)), _full((n, d)), _full((d, d)), _full((1, d)), _full((1, d)),
                  _full((d, N_EXPERTS)), _full((1, N_EXPERTS))],
        out_specs=[_full((n, d)), _full((n, N_EXPERTS))],
        out_shape=[jax.ShapeDtypeStruct((n, d), F32), jax.ShapeDtypeStruct((n, N_EXPERTS), F32)],
        compiler_params=_params(("arbitrary",)),
        name="nsa_out_proj",
    )(o, h, wo.astype(BF16), lng, lnb, wr, br)


def kernel(x_prompt, x_sample, state_conv, cache_kv_cmp, cache_kv_sel, cache_kv_win, page_table,
           conv_w_in, conv_w, conv_w_out, kv_w, cmp_pe, cmp_w1, cmp_b1, cmp_w2,
           nsa_wq, nsa_wg, nsa_wo, moe_wr, moe_br, moe_w1, moe_b1, moe_w2, moe_b2, ln_g, ln_b):
    bp, sp, d = x_prompt.shape
    bd, sd, _ = x_sample.shape
    kv_shape = (2, N_KV_HEADS, HEAD_DIM)
    lng = lambda l, s: ln_g[l, s][None, :]
    lnb = lambda l, s: ln_b[l, s][None, :]
    ffn = lambda l, routed, tm_rows: _moe_ffn(routed, moe_w1, moe_b1, moe_w2, moe_b2, layer=l, tm_rows=tm_rows)
    combine = lambda l, h, yg_gate, tm_tok: _moe_combine(h, *yg_gate, lng(l, 1), lnb(l, 1), tm=tm_tok)
    w_cat = _proj_weights(kv_w, nsa_wq[0], nsa_wg[0])
    cw = _cmp_weights(cmp_pe, cmp_w1, cmp_b1, cmp_w2)
    br = lambda l: moe_br[l][None, :]

    n_p = bp * sp
    xp = x_prompt.reshape(n_p, d)
    h, conv_p, lg = _conv_layer(xp, jnp.zeros((bp, 2, d), F32), conv_w_in[0], conv_w[0], conv_w_out[0],
                                lng(0, 0), lnb(0, 0), moe_wr[0], br(0), seq=sp, tm=256, rows_prev=False)
    routed_p = _moe_route(h, lg, tm_rows=256)
    cols_major = lambda a: a.transpose(0, 2, 3, 4, 1).reshape(a.shape[0], KV_COLS, a.shape[1])
    kvc_s = _compress_paged(cols_major(cache_kv_cmp), page_table, cw)
    n_s = bd * sd
    xs = x_sample.reshape(n_s, d)
    prev_rows = jnp.repeat(state_conv[0].transpose(1, 0, 2), sd, axis=1)
    hs, u_s, lgs = _conv_layer(xs, prev_rows, conv_w_in[0], conv_w[0], conv_w_out[0],
                               lng(0, 0), lnb(0, 0), moe_wr[0], br(0), seq=sd, tm=n_s, rows_prev=True)
    conv_s = u_s.reshape(bd, sd, d)[:, sd - 2:]
    routed_s = _moe_route(hs, lgs, tm_rows=64)
    yg_p = ffn(0, routed_p, 256)
    yg_s = ffn(0, routed_s, 64)
    h = combine(0, h, yg_p, 256)
    hs = combine(0, hs, yg_s, n_s)

    pos_s = PAST_LEN + (jnp.arange(n_s, dtype=jnp.int32) % sd)
    rc_s, rs_s, rw_s, gates_s, q_s = _nsa_proj(hs, w_cat, _rope_tables(pos_s), tm=n_s, seq=sd, heads_out=False)
    q5 = q_s.reshape(bd, sd, N_KV_HEADS, HEADS_PER_GROUP, HEAD_DIM).transpose(0, 3, 2, 1, 4)
    eye = jnp.eye(N_KV_HEADS, dtype=F32)
    qbd = jnp.einsum('bqgtd,gh->bqgthd', q5, eye).reshape(bd, N_HEADS * sd, N_KV_HEADS * HEAD_DIM).astype(BF16)
    g5 = gates_s.reshape(bd, sd, N_KV_HEADS, LANES)[..., :HEADS_PER_GROUP * 3]
    g5 = g5.reshape(bd, sd, N_KV_HEADS, HEADS_PER_GROUP, 3).transpose(0, 3, 2, 1, 4).reshape(bd, N_HEADS * sd, 3)
    gt = jnp.pad(g5, ((0, 0), (0, 0), (0, 5)))
    pad_new = lambda r: jnp.pad(r.reshape(bd, sd, KV_COLS), ((0, 0), (0, 16 - sd), (0, 0)))
    w_buf = cache_kv_win.shape[1]
    o_s = _nsa_sample(qbd, gt, kvc_s, cols_major(cache_kv_sel), page_table, pad_new(rs_s),
                      cols_major(cache_kv_win), pad_new(rw_s), dec=sd, past=PAST_LEN, ppc=16)
    o_s = o_s.reshape(bd, HEADS_PER_GROUP, N_KV_HEADS, sd, HEAD_DIM).transpose(0, 3, 2, 1, 4).reshape(n_s, d)
    hs, lgs = _out_proj(o_s, hs, nsa_wo[0], lng(1, 0), lnb(1, 0), moe_wr[1], br(1))
    routed_s = _moe_route(hs, lgs, tm_rows=64)

    tabs = _rope_tables(jnp.arange(sp, dtype=jnp.int32))
    rc, rs, rw, gates, qh, ks, vs, kw, vw = _nsa_proj(h, w_cat, tabs, tm=256, seq=sp, heads_out=True)
    kvc = _compress_t(rc, cw)
    kvc_h = kvc.reshape(bp, -1, 2, N_KV_HEADS, HEAD_DIM).transpose(2, 0, 3, 1, 4).astype(BF16)
    h, lg = _nsa_prompt(h, qh, kvc_h[0], kvc_h[1], ks, vs, kw, vw, gates, nsa_wo[0],
                        lng(1, 0), lnb(1, 0), moe_wr[1], br(1), tq=256, kc=512)
    routed_p = _moe_route(h, lg, tm_rows=256)
    yg_p = ffn(1, routed_p, 256)
    yg_s = ffn(1, routed_s, 64)
    y_prompt = combine(1, h, yg_p, 256).reshape(bp, sp, d)
    y_sample = combine(1, hs, yg_s, n_s).reshape(bd, sd, d)
    rows_out = lambda r_t: r_t.reshape((bp,) + kv_shape + (r_t.shape[-1],)).transpose(0, 4, 1, 2, 3)
    kv_cmp_p = rows_out(rc)
    kv_sel_p = rows_out(rs)
    kv_win_p = rows_out(rw[:, :, sp - min(WINDOW, sp):])

    kv_cmp_s = rc_s.reshape((bd, sd) + kv_shape)
    kv_sel_s = rs_s.reshape((bd, sd) + kv_shape)
    kv_win_s = jnp.concatenate([cache_kv_win, rw_s.reshape((bd, sd) + kv_shape)], axis=1)[:, -w_buf:]

    return (y_prompt, y_sample, conv_p[None], kv_cmp_p, kv_sel_p, kv_win_p,
            conv_s[None], kv_cmp_s, kv_sel_s, kv_win_s)
```
